```python
import jax, jax.numpy as jnp
from jax import lax
import numpy as np

D_MODEL = 1024
BATCH = 8
SEQ = 8192
DEPTH = 1

ATTN_HEADS = 8
ATTN_KV_HEADS = 2
ATTN_HEAD_DIM = 64
WINDOW = 128
ATTN_BLOCK = 128
NUM_BUCKETS = 32
MAX_DISTANCE = 128
GLA_HEADS = 4
GLA_DK = 64
GLA_DV = 128
GLA_GATE_RANK = 16
GLA_GATE_NORM = 16.0
GLA_CHUNK = 64
D_FF = 4 * D_MODEL
EPS = 1e-6
N_MOD = 6

ATTN_Q = ATTN_HEADS * ATTN_HEAD_DIM
ATTN_KV = ATTN_KV_HEADS * ATTN_HEAD_DIM
GLA_K = GLA_HEADS * GLA_DK
GLA_V = GLA_HEADS * GLA_DV
SPLIT_SIZES = (ATTN_Q, ATTN_KV, ATTN_KV, GLA_K, GLA_K, GLA_V, GLA_V, GLA_GATE_RANK)
SPLIT_POINTS = tuple(int(s) for s in np.cumsum(SPLIT_SIZES)[:-1])
D_IN = int(sum(SPLIT_SIZES))
D_MIX = ATTN_Q + GLA_V

kernel_name = "hybrid_swa_gla_adaln_layer"


def rms_norm(x):
    x32 = x.astype(jnp.float32)
    return (x32 * lax.rsqrt(jnp.mean(x32 * x32, axis=-1, keepdims=True) + EPS)).astype(x.dtype)


def t5_causal_bucket(dist):
    max_exact = NUM_BUCKETS // 2
    d = np.maximum(dist, 0)
    large = max_exact + (np.log(np.maximum(d, max_exact) / max_exact)
                         / np.log(MAX_DISTANCE / max_exact) * (NUM_BUCKETS - max_exact)).astype(np.int32)
    large = np.minimum(large, NUM_BUCKETS - 1)
    return np.where(d < max_exact, d, large).astype(np.int32)


def sliding_window_attention(q, k, v, sinks, rel_bias):
    B, T, Hq, dh = q.shape
    Hkv = k.shape[2]
    G = Hq // Hkv
    L = ATTN_BLOCK
    nb = T // L
    qb = q.astype(jnp.float32).reshape(B, nb, L, Hkv, G, dh)
    kb = k.astype(jnp.float32).reshape(B, nb, L, Hkv, dh)
    vb = v.astype(jnp.float32).reshape(B, nb, L, Hkv, dh)
    pad = jnp.zeros_like(kb[:, :1])
    k2 = jnp.concatenate([jnp.concatenate([pad, kb[:, :-1]], axis=1), kb], axis=2)
    v2 = jnp.concatenate([jnp.concatenate([pad, vb[:, :-1]], axis=1), vb], axis=2)
    i = np.arange(L)[:, None]
    j = np.arange(2 * L)[None, :]
    dist = i + L - j
    band = (dist >= 0) & (dist < WINDOW)
    valid = np.broadcast_to(band, (nb, L, 2 * L)).copy()
    valid[0] &= np.broadcast_to(j >= L, (L, 2 * L))
    bucket = t5_causal_bucket(dist)
    bias = rel_bias.astype(jnp.float32)[bucket]
    bias = jnp.transpose(bias, (2, 0, 1)).reshape(Hkv, G, 1, L, 2 * L)
    s = jnp.einsum('bnqhgd,bnkhd->bhgnqk', qb, k2) * (dh ** -0.5) + bias
    s = jnp.where(valid, s, -jnp.inf)
    sink = sinks.astype(jnp.float32).reshape(Hkv, G)[None, :, :, None, None, None]
    m = jnp.maximum(jnp.max(s, axis=-1, keepdims=True), sink)
    p = jnp.exp(s - m)
    p = p / (jnp.sum(p, axis=-1, keepdims=True) + jnp.exp(sink - m))
    o = jnp.einsum('bhgnqk,bnkhd->bnqhgd', p, v2)
    return o.reshape(B, T, Hq * dh)


def gla_chunked(q, k, v, log_a):
    B, T, H, dk = q.shape
    dv = v.shape[-1]
    C = GLA_CHUNK
    n = T // C
    q = q.astype(jnp.float32).reshape(B, n, C, H, dk) * (dk ** -0.5)
    k = k.astype(jnp.float32).reshape(B, n, C, H, dk)
    v = v.astype(jnp.float32).reshape(B, n, C, H, dv)
    b = jnp.cumsum(log_a.astype(jnp.float32).reshape(B, n, C, H, dk), axis=2)
    b_last = b[:, :, -1:]
    q_t = q * jnp.exp(b)
    k_t = k * jnp.exp(-b)
    k_s = k * jnp.exp(b_last - b)
    causal = np.tril(np.ones((C, C), dtype=bool))
    A = jnp.where(causal, jnp.einsum('bnihd,bnjhd->bnhij', q_t, k_t), 0.0)
    o_intra = jnp.einsum('bnhij,bnjhv->bnihv', A, v)
    dS = jnp.einsum('bnjhd,bnjhv->bnhdv', k_s, v)
    decay = jnp.exp(b_last[:, :, 0])

    def step(S, inp):
        dS_n, dec_n = inp
        return dec_n[..., None] * S + dS_n, S

    S0 = jnp.zeros((B, H, dk, dv), jnp.float32)
    _, S_prev = lax.scan(step, S0, (jnp.moveaxis(dS, 1, 0), jnp.moveaxis(decay, 1, 0)))
    S_prev = jnp.moveaxis(S_prev, 0, 1)
    o_inter = jnp.einsum('bnihd,bnhdv->bnihv', q_t, S_prev)
    return (o_intra + o_inter).reshape(B, T, H, dv)


def setup_inputs(seed: int = 0) -> dict:
    key = jax.random.key(seed)
    ks = jax.random.split(key, 14)
    nrm = jax.random.normal
    f32 = jnp.float32
    return {
        "x": nrm(ks[0], (BATCH, SEQ, D_MODEL), f32),
        "c": nrm(ks[1], (BATCH, D_MODEL), f32),
        "w_ada": nrm(ks[2], (DEPTH, D_MODEL, N_MOD * D_MODEL), f32) * (0.5 * D_MODEL ** -0.5),
        "b_ada": nrm(ks[3], (DEPTH, N_MOD * D_MODEL), f32) * 0.01,
        "w_in": nrm(ks[4], (DEPTH, D_MODEL, D_IN), f32) * D_MODEL ** -0.5,
        "w_gate_up": nrm(ks[5], (DEPTH, GLA_GATE_RANK, GLA_K), f32) * GLA_GATE_RANK ** -0.5,
        "b_gate": nrm(ks[6], (DEPTH, GLA_K), f32) * 0.1,
        "gla_norm_g": 1.0 + 0.01 * nrm(ks[7], (DEPTH, GLA_DV), f32),
        "attn_sinks": nrm(ks[8], (DEPTH, ATTN_HEADS), f32) * 0.5,
        "rel_bias": nrm(ks[9], (NUM_BUCKETS, ATTN_HEADS), f32) * 0.5,
        "w_out": nrm(ks[10], (DEPTH, D_MIX, D_MODEL), f32) * D_MIX ** -0.5,
        "w_mlp_in": nrm(ks[11], (DEPTH, D_MODEL, D_FF), f32) * D_MODEL ** -0.5,
        "w_mlp_out": nrm(ks[12], (DEPTH, D_FF, D_MODEL), f32) * D_FF ** -0.5,
        "final_norm_g": 1.0 + 0.01 * nrm(ks[13], (D_MODEL,), f32),
    }


def reference(x, c, w_ada, b_ada, w_in, w_gate_up, b_gate, gla_norm_g, attn_sinks, rel_bias,
              w_out, w_mlp_in, w_mlp_out, final_norm_g):
    B, T, D = x.shape
    cond = jax.nn.silu(c)
    for l in range(DEPTH):
        ada = cond @ w_ada[l] + b_ada[l]
        shift1, scale1, gate1, shift2, scale2, gate2 = jnp.split(ada[:, None, :], N_MOD, axis=-1)

        h = rms_norm(x) * (1.0 + scale1) + shift1
        proj = h @ w_in[l]
        q_a, k_a, v_a, q_g, k_g, v_g, g_out, z_lr = jnp.split(proj, SPLIT_POINTS, axis=-1)

        attn = sliding_window_attention(
            q_a.reshape(B, T, ATTN_HEADS, ATTN_HEAD_DIM),
            k_a.reshape(B, T, ATTN_KV_HEADS, ATTN_HEAD_DIM),
            v_a.reshape(B, T, ATTN_KV_HEADS, ATTN_HEAD_DIM),
            attn_sinks[l], rel_bias).astype(x.dtype)

        log_a = jax.nn.log_sigmoid((z_lr @ w_gate_up[l] + b_gate[l]).astype(jnp.float32)) / GLA_GATE_NORM
        o = gla_chunked(
            q_g.reshape(B, T, GLA_HEADS, GLA_DK),
            k_g.reshape(B, T, GLA_HEADS, GLA_DK),
            v_g.reshape(B, T, GLA_HEADS, GLA_DV),
            log_a.reshape(B, T, GLA_HEADS, GLA_DK))
        o = rms_norm(o) * gla_norm_g[l].astype(jnp.float32)
        gla = (o.reshape(B, T, GLA_V) * jax.nn.silu(g_out.astype(jnp.float32))).astype(x.dtype)

        mix = jnp.concatenate([attn, gla], axis=-1) @ w_out[l]
        x = x + gate1 * mix

        h = rms_norm(x) * (1.0 + scale2) + shift2
        x = x + gate2 * (jnp.square(jax.nn.relu(h @ w_mlp_in[l])) @ w_mlp_out[l])
    return rms_norm(x) * final_norm_g
```

```python
import functools

import numpy as np
import jax
import jax.numpy as jnp
from jax import lax
from jax.experimental import pallas as pl
from jax.experimental.pallas import tpu as pltpu

F32 = jnp.float32
BF16 = jnp.bfloat16

D_MODEL = 1024
ATTN_HEADS = 8
ATTN_KV_HEADS = 2
ATTN_GROUP = ATTN_HEADS // ATTN_KV_HEADS
ATTN_HEAD_DIM = 64
WINDOW = 128
ATTN_BLOCK = 128
NUM_BUCKETS = 32
MAX_DISTANCE = 128
GLA_HEADS = 4
GLA_DK = 64
GLA_DV = 128
GLA_GATE_RANK = 16
GLA_GATE_NORM = 16.0
GLA_CHUNK = 64
D_FF = 4 * D_MODEL
EPS = 1e-6
N_MOD = 6

ATTN_Q = ATTN_HEADS * ATTN_HEAD_DIM
ATTN_KV = ATTN_KV_HEADS * ATTN_HEAD_DIM
GLA_K = GLA_HEADS * GLA_DK
GLA_V = GLA_HEADS * GLA_DV
GLA_PACK = 2 * GLA_K + 2 * GLA_V
LANES = 128
Z_PAD = LANES

VMEM_LIMIT = 56 * 1024 * 1024

NT_DIMS = (((1,), (1,)), ((), ()))
TN_DIMS = (((0,), (0,)), ((), ()))


def _dot(a, b):
    return jnp.dot(a, b, preferred_element_type=F32)


def _rms(x):
    return x * lax.rsqrt(jnp.mean(x * x, axis=-1, keepdims=True) + EPS)


def _ada_kernel(c_ref, w_ref, b_ref, o_ref):
    c = c_ref[...]
    cond = c * jax.nn.sigmoid(c)
    o_ref[...] = jnp.dot(cond, w_ref[...], preferred_element_type=F32,
                         precision=lax.Precision.HIGHEST) + b_ref[...]


def _ada(c, w_ada, b_ada):
    bsz, d = c.shape
    n = w_ada.shape[1]
    tn = 1536
    return pl.pallas_call(
        _ada_kernel,
        grid=(n // tn,),
        in_specs=[pl.BlockSpec((bsz, d), lambda j: (0, 0)),
                  pl.BlockSpec((d, tn), lambda j: (0, j)),
                  pl.BlockSpec((1, tn), lambda j: (0, j))],
        out_specs=pl.BlockSpec((bsz, tn), lambda j: (0, j)),
        out_shape=jax.ShapeDtypeStruct((bsz, n), F32),
        name="ada",
    )(c, w_ada, b_ada.reshape(1, n))


def _t5_causal_bucket(dist):
    max_exact = NUM_BUCKETS // 2
    d = np.maximum(dist, 0)
    large = max_exact + (np.log(np.maximum(d, max_exact) / max_exact)
                         / np.log(MAX_DISTANCE / max_exact) * (NUM_BUCKETS - max_exact)).astype(np.int32)
    large = np.minimum(large, NUM_BUCKETS - 1)
    return np.where(d < max_exact, d, large).astype(np.int32)


def _bias_kernel(bucket_ref, valid_ref, rb_ref, o_ref):
    bucket = bucket_ref[...]
    valid = valid_ref[...]
    for h in range(ATTN_HEADS):
        acc = jnp.zeros(bucket.shape, F32)
        for b in range(NUM_BUCKETS):
            acc = jnp.where(bucket == b, rb_ref[b, h], acc)
        for first in range(2):
            o_ref[first, h] = jnp.where(valid[first] != 0, acc, -jnp.inf)


def _bias_table(rel_bias):
    L = ATTN_BLOCK
    i = np.arange(L)[:, None]
    j = np.arange(2 * L)[None, :]
    dist = i + L - j
    band = (dist >= 0) & (dist < WINDOW)
    valid = np.stack([band & (j >= L), band]).astype(np.int32)
    bucket = _t5_causal_bucket(dist)
    return pl.pallas_call(
        _bias_kernel,
        in_specs=[pl.BlockSpec(memory_space=pltpu.VMEM),
                  pl.BlockSpec(memory_space=pltpu.VMEM),
                  pl.BlockSpec(memory_space=pltpu.SMEM)],
        out_specs=pl.BlockSpec(memory_space=pltpu.VMEM),
        out_shape=jax.ShapeDtypeStruct((2, ATTN_HEADS, L, 2 * L), F32),
        name="bias_table",
    )(jnp.asarray(bucket), jnp.asarray(valid), rel_bias.astype(F32))


def _inproj_kernel(x_ref, sc_ref, sh_ref, wq_ref, wkv_ref, wg_ref, wz_ref, wgu_ref, bg_ref,
                   qa_ref, kva_ref, g_ref, la_ref):
    x = x_ref[0]
    h = _rms(x) * (1.0 + sc_ref[0]) + sh_ref[0]
    hb = h.astype(BF16)
    qa_ref[0] = (_dot(hb, wq_ref[...]) * (ATTN_HEAD_DIM ** -0.5)).astype(BF16)
    kva_ref[0] = _dot(hb, wkv_ref[...]).astype(BF16)
    g = _dot(hb, wg_ref[...])
    g_ref[0, :, :GLA_K] = (g[:, :GLA_K] * (GLA_DK ** -0.5)).astype(BF16)
    g_ref[0, :, GLA_K:] = g[:, GLA_K:].astype(BF16)
    z = _dot(hb, wz_ref[...])
    gp = _dot(z.astype(BF16), wgu_ref[...]) + bg_ref[...]
    log_sig = jnp.minimum(gp, 0.0) - jnp.log1p(jnp.exp(-jnp.abs(gp)))
    la_ref[0] = log_sig * (1.0 / GLA_GATE_NORM)


def _inproj(x, scale1, shift1, wq, wkv, wg, wz, wgu, bg, tm):
    bsz, t, d = x.shape
    const = lambda shape: pl.BlockSpec(shape, lambda b, i: (0,) * len(shape),
                                       pipeline_mode=pl.Buffered(1))
    tile = lambda n: pl.BlockSpec((1, tm, n), lambda b, i: (b, i, 0))
    mod = pl.BlockSpec((1, 1, d), lambda b, i: (b, 0, 0))
    return pl.pallas_call(
        _inproj_kernel,
        grid=(bsz, t // tm),
        in_specs=[tile(d), mod, mod, const(wq.shape), const(wkv.shape), const(wg.shape),
                  const(wz.shape), const(wgu.shape), const(bg.shape)],
        out_specs=[tile(ATTN_Q), tile(2 * ATTN_KV), tile(GLA_PACK), tile(GLA_K)],
        out_shape=[jax.ShapeDtypeStruct((bsz, t, ATTN_Q), BF16),
                   jax.ShapeDtypeStruct((bsz, t, 2 * ATTN_KV), BF16),
                   jax.ShapeDtypeStruct((bsz, t, GLA_PACK), BF16),
                   jax.ShapeDtypeStruct((bsz, t, GLA_K), F32)],
        compiler_params=pltpu.CompilerParams(
            dimension_semantics=("arbitrary", "arbitrary"), vmem_limit_bytes=VMEM_LIMIT),
        name="inproj",
    )(x, scale1, shift1, wq, wkv, wg, wz, wgu, bg)


def _swa_kernel(q_ref, kv_ref, kvp_ref, bias_ref, sink_ref, o_ref, *, nblk):
    L = ATTN_BLOCK
    dh = ATTN_HEAD_DIM
    t = pl.program_id(1)
    for n in range(nblk):
        rows = slice(n * L, (n + 1) * L)
        q = q_ref[0, rows, :]
        kv_prev = kvp_ref[0] if n == 0 else kv_ref[0, (n - 1) * L:n * L, :]
        kv2 = jnp.concatenate([kv_prev, kv_ref[0, rows, :]], axis=0)
        sel = jnp.where(t == 0, 0, 1) if n == 0 else 1
        outs = []
        for h in range(ATTN_HEADS):
            g = h // ATTN_GROUP
            k2 = kv2[:, g * dh:(g + 1) * dh]
            v2 = kv2[:, ATTN_KV + g * dh:ATTN_KV + (g + 1) * dh]
            s = lax.dot_general(q[:, h * dh:(h + 1) * dh], k2, NT_DIMS,
                                preferred_element_type=F32)
            s = s + bias_ref[sel, h]
            sink = sink_ref[h]
            m = jnp.maximum(jnp.max(s, axis=-1, keepdims=True), sink)
            p = jnp.exp(s - m)
            denom = jnp.sum(p, axis=-1, keepdims=True) + jnp.exp(sink - m)
            outs.append(_dot(p.astype(BF16), v2) / denom)
        o_ref[0, rows, :] = jnp.concatenate(outs, axis=-1).astype(BF16)


def _swa(qa, kva, bias, sinks, tq):
    bsz, t, _ = qa.shape
    nblk = tq // ATTN_BLOCK
    return pl.pallas_call(
        functools.partial(_swa_kernel, nblk=nblk),
        grid=(bsz, t // tq),
        in_specs=[pl.BlockSpec((1, tq, ATTN_Q), lambda b, i: (b, i, 0)),
                  pl.BlockSpec((1, tq, 2 * ATTN_KV), lambda b, i: (b, i, 0)),
                  pl.BlockSpec((1, ATTN_BLOCK, 2 * ATTN_KV),
                               lambda b, i: (b, jnp.maximum(i * nblk - 1, 0), 0)),
                  pl.BlockSpec(bias.shape, lambda b, i: (0, 0, 0, 0), pipeline_mode=pl.Buffered(1)),
                  pl.BlockSpec(memory_space=pltpu.SMEM)],
        out_specs=pl.BlockSpec((1, tq, ATTN_Q), lambda b, i: (b, i, 0)),
        out_shape=jax.ShapeDtypeStruct((bsz, t, ATTN_Q), BF16),
        compiler_params=pltpu.CompilerParams(
            dimension_semantics=("arbitrary", "arbitrary"), vmem_limit_bytes=VMEM_LIMIT),
        name="swa",
    )(qa, kva, kva, bias, sinks)


def _gla_kernel(g_ref, la_ref, gn_ref, o_ref, st_ref, *, nchunk):
    C = GLA_CHUNK
    dk, dv = GLA_DK, GLA_DV

    @pl.when(pl.program_id(1) == 0)
    def _():
        st_ref[...] = jnp.zeros_like(st_ref)

    ri = lax.broadcasted_iota(jnp.int32, (C, C), 0)
    ci = lax.broadcasted_iota(jnp.int32, (C, C), 1)
    causal = ri >= ci
    tril = causal.astype(F32)
    for c in range(nchunk):
        rows = slice(c * C, (c + 1) * C)
        la = la_ref[0, rows, :]
        b = jnp.dot(tril, la, preferred_element_type=F32, precision=lax.Precision.HIGHEST)
        b_last = b[C - 1:C, :]
        q = g_ref[0, rows, 0:GLA_K].astype(F32)
        k = g_ref[0, rows, GLA_K:2 * GLA_K].astype(F32)
        v = g_ref[0, rows, 2 * GLA_K:2 * GLA_K + GLA_V]
        go = g_ref[0, rows, 2 * GLA_K + GLA_V:].astype(F32)
        q_t = (q * jnp.exp(b)).astype(BF16)
        k_t = (k * jnp.exp(-b)).astype(BF16)
        k_s = (k * jnp.exp(b_last - b)).astype(BF16)
        st = st_ref[...]
        st_b = st.astype(BF16)
        outs, dst = [], []
        for h in range(GLA_HEADS):
            hs = slice(h * dk, (h + 1) * dk)
            v_h = v[:, h * dv:(h + 1) * dv]
            a = lax.dot_general(q_t[:, hs], k_t[:, hs], NT_DIMS, preferred_element_type=F32)
            a = jnp.where(causal, a, 0.0).astype(BF16)
            o_h = _dot(a, v_h) + lax.dot_general(q_t[:, hs], st_b[:, hs], NT_DIMS,
                                                 preferred_element_type=F32)
            outs.append(_rms(o_h) * gn_ref[...])
            dst.append(lax.dot_general(v_h, k_s[:, hs], TN_DIMS, preferred_element_type=F32))
        st_ref[...] = st * jnp.exp(b_last) + jnp.concatenate(dst, axis=-1)
        o = jnp.concatenate(outs, axis=-1)
        o_ref[0, rows, :] = (o * (go * jax.nn.sigmoid(go))).astype(BF16)


def _gla(gpack, la, gn, tg):
    bsz, t, _ = gpack.shape
    return pl.pallas_call(
        functools.partial(_gla_kernel, nchunk=tg // GLA_CHUNK),
        grid=(bsz, t // tg),
        in_specs=[pl.BlockSpec((1, tg, GLA_PACK), lambda b, i: (b, i, 0)),
                  pl.BlockSpec((1, tg, GLA_K), lambda b, i: (b, i, 0)),
                  pl.BlockSpec((1, GLA_DV), lambda b, i: (0, 0))],
        out_specs=pl.BlockSpec((1, tg, GLA_V), lambda b, i: (b, i, 0)),
        out_shape=jax.ShapeDtypeStruct((bsz, t, GLA_V), BF16),
        scratch_shapes=[pltpu.VMEM((GLA_DV, GLA_K), F32)],
        compiler_params=pltpu.CompilerParams(
            dimension_semantics=("arbitrary", "arbitrary"), vmem_limit_bytes=VMEM_LIMIT),
        name="gla",
    )(gpack, la, gn)


def _mlp_kernel(x_ref, attn_ref, gla_ref, g1_ref, sh2_ref, sc2_ref, g2_ref, wo_ref, w1_ref, w2_ref,
                fg_ref, o_ref, u_ref, *, fchunk, final):
    mix = _dot(attn_ref[0], wo_ref[:ATTN_Q, :]) + _dot(gla_ref[0], wo_ref[ATTN_Q:, :])
    x1 = x_ref[0] + g1_ref[0] * mix
    hb = (_rms(x1) * (1.0 + sc2_ref[0]) + sh2_ref[0]).astype(BF16)
    for c in range(D_FF // fchunk):
        cols = slice(c * fchunk, (c + 1) * fchunk)
        u = jnp.maximum(_dot(hb, w1_ref[:, cols]), 0.0)
        u_ref[:, cols] = (u * u).astype(BF16)
    x2 = x1 + g2_ref[0] * _dot(u_ref[...], w2_ref[...])
    o_ref[0] = _rms(x2) * fg_ref[...] if final else x2


def _mlp(x, attn, gla, gate1, shift2, scale2, gate2, wo, w1, w2, fg, tm, final):
    bsz, t, d = x.shape
    const = lambda shape: pl.BlockSpec(shape, lambda b, i: (0,) * len(shape),
                                       pipeline_mode=pl.Buffered(1))
    tile = lambda n: pl.BlockSpec((1, tm, n), lambda b, i: (b, i, 0))
    mod = pl.BlockSpec((1, 1, d), lambda b, i: (b, 0, 0))
    return pl.pallas_call(
        functools.partial(_mlp_kernel, fchunk=1024, final=final),
        grid=(bsz, t // tm),
        in_specs=[tile(d), tile(ATTN_Q), tile(GLA_V), mod, mod, mod, mod,
                  const(wo.shape), const(w1.shape), const(w2.shape), const(fg.shape)],
        out_specs=tile(d),
        out_shape=jax.ShapeDtypeStruct((bsz, t, d), F32),
        scratch_shapes=[pltpu.VMEM((tm, D_FF), BF16)],
        compiler_params=pltpu.CompilerParams(
            dimension_semantics=("arbitrary", "arbitrary"), vmem_limit_bytes=VMEM_LIMIT),
        name="mlp",
    )(x, attn, gla, gate1, shift2, scale2, gate2, wo, w1, w2, fg)


def kernel(x, c, w_ada, b_ada, w_in, w_gate_up, b_gate, gla_norm_g, attn_sinks, rel_bias,
           w_out, w_mlp_in, w_mlp_out, final_norm_g):
    bsz, t, d = x.shape
    depth = w_ada.shape[0]
    tm = min(512, t)
    bias = _bias_table(rel_bias)
    p_kv = ATTN_Q
    p_g = ATTN_Q + 2 * ATTN_KV
    p_z = p_g + GLA_PACK
    for l in range(depth):
        ada = _ada(c, w_ada[l], b_ada[l])
        shift1, scale1, gate1, shift2, scale2, gate2 = [
            ada[:, None, i * d:(i + 1) * d] for i in range(N_MOD)]
        w = w_in[l]
        wq = w[:, :p_kv].astype(BF16)
        wkv = w[:, p_kv:p_g].astype(BF16)
        wg = w[:, p_g:p_z].astype(BF16)
        wz = jnp.pad(w[:, p_z:], ((0, 0), (0, Z_PAD - GLA_GATE_RANK))).astype(BF16)
        wgu = jnp.pad(w_gate_up[l], ((0, Z_PAD - GLA_GATE_RANK), (0, 0))).astype(BF16)
        qa, kva, gpack, la = _inproj(x, scale1, shift1, wq, wkv, wg, wz, wgu,
                                     b_gate[l].reshape(1, GLA_K), tm)
        attn = _swa(qa, kva, bias, attn_sinks[l], tm)
        gla = _gla(gpack, la, gla_norm_g[l].reshape(1, GLA_DV), tm)
        x = _mlp(x, attn, gla, gate1, shift2, scale2, gate2,
                 w_out[l].astype(BF16), w_mlp_in[l].astype(BF16), w_mlp_out[l].astype(BF16),
                 final_norm_g.reshape(1, d), tm, final=(l == depth - 1))
    return x
```

```python
import functools

import numpy as np
import jax
import jax.numpy as jnp
from jax import lax
from jax.experimental import pallas as pl
from jax.experimental.pallas import tpu as pltpu

F32 = jnp.float32
BF16 = jnp.bfloat16

D_MODEL = 1024
ATTN_HEADS = 8
ATTN_KV_HEADS = 2
ATTN_GROUP = ATTN_HEADS // ATTN_KV_HEADS
ATTN_HEAD_DIM = 64
WINDOW = 128
ATTN_BLOCK = 128
NUM_BUCKETS = 32
MAX_DISTANCE = 128
GLA_HEADS = 4
GLA_PAIRS = GLA_HEADS // 2
GLA_DK = 64
GLA_DV = 128
GLA_GATE_RANK = 16
GLA_GATE_NORM = 16.0
GLA_CHUNK = 128
D_FF = 4 * D_MODEL
EPS = 1e-6
N_MOD = 6

ATTN_Q = ATTN_HEADS * ATTN_HEAD_DIM
ATTN_KV = ATTN_KV_HEADS * ATTN_HEAD_DIM
GLA_K = GLA_HEADS * GLA_DK
GLA_V = GLA_HEADS * GLA_DV
GLA_PACK = 2 * GLA_K + 2 * GLA_V
LANES = 128
HALF = LANES // 2
Z_PAD = LANES
LOG2E = 1.4426950408889634

VMEM_LIMIT = 56 * 1024 * 1024

NT_DIMS = (((1,), (1,)), ((), ()))
TN_DIMS = (((0,), (0,)), ((), ()))

ATTN_HEAD_ORDER = tuple(h for p in range(ATTN_GROUP) for h in (p, p + ATTN_GROUP))


def _dot(a, b):
    return jnp.dot(a, b, preferred_element_type=F32)


def _rms(x):
    return x * lax.rsqrt(jnp.mean(x * x, axis=-1, keepdims=True) + EPS)


def _ada_kernel(c_ref, w_ref, b_ref, o_ref):
    c = c_ref[...]
    cond = c * jax.nn.sigmoid(c)
    o_ref[...] = jnp.dot(cond, w_ref[...], preferred_element_type=F32,
                         precision=lax.Precision.HIGHEST) + b_ref[...]


def _ada(c, w_ada, b_ada):
    bsz, d = c.shape
    n = w_ada.shape[1]
    tn = 1536
    return pl.pallas_call(
        _ada_kernel,
        grid=(n // tn,),
        in_specs=[pl.BlockSpec((bsz, d), lambda j: (0, 0)),
                  pl.BlockSpec((d, tn), lambda j: (0, j)),
                  pl.BlockSpec((1, tn), lambda j: (0, j))],
        out_specs=pl.BlockSpec((bsz, tn), lambda j: (0, j)),
        out_shape=jax.ShapeDtypeStruct((bsz, n), F32),
        name="ada",
    )(c, w_ada, b_ada.reshape(1, n))


def _t5_causal_bucket(dist):
    max_exact = NUM_BUCKETS // 2
    d = np.maximum(dist, 0)
    large = max_exact + (np.log(np.maximum(d, max_exact) / max_exact)
                         / np.log(MAX_DISTANCE / max_exact) * (NUM_BUCKETS - max_exact)).astype(np.int32)
    large = np.minimum(large, NUM_BUCKETS - 1)
    return np.where(d < max_exact, d, large).astype(np.int32)


def _bias_kernel(bucket_ref, rb_ref, sink_ref, bias_ref, sinkrow_ref):
    L = ATTN_BLOCK
    bucket = bucket_ref[...]
    key = lax.broadcasted_iota(jnp.int32, (L, L), 0)
    qry = lax.broadcasted_iota(jnp.int32, (L, L), 1)
    prev_key = key > qry
    for g in range(ATTN_KV_HEADS):
        for r in range(ATTN_GROUP):
            h = g * ATTN_GROUP + r
            cols = slice(r * L, (r + 1) * L)
            acc = jnp.zeros((L, L), F32)
            for b in range(NUM_BUCKETS):
                acc = jnp.where(bucket == b, rb_ref[b, h], acc)
            acc = acc * LOG2E
            bias_ref[1, g, :, cols] = acc
            bias_ref[0, g, :, cols] = jnp.where(prev_key, -jnp.inf, acc)
            sinkrow_ref[g, :, cols] = jnp.full((1, L), sink_ref[h] * LOG2E, F32)


def _bias_table(rel_bias, sinks):
    L = ATTN_BLOCK
    c = np.arange(L)[:, None]
    i = np.arange(L)[None, :]
    bucket = _t5_causal_bucket((i - c) % L)
    return pl.pallas_call(
        _bias_kernel,
        in_specs=[pl.BlockSpec(memory_space=pltpu.VMEM),
                  pl.BlockSpec(memory_space=pltpu.SMEM),
                  pl.BlockSpec(memory_space=pltpu.SMEM)],
        out_specs=[pl.BlockSpec(memory_space=pltpu.VMEM), pl.BlockSpec(memory_space=pltpu.VMEM)],
        out_shape=[jax.ShapeDtypeStruct((2, ATTN_KV_HEADS, L, ATTN_GROUP * L), F32),
                   jax.ShapeDtypeStruct((ATTN_KV_HEADS, 1, ATTN_GROUP * L), F32)],
        name="bias_table",
    )(jnp.asarray(bucket), rel_bias.astype(F32), sinks.astype(F32))


def _inproj_kernel(x_ref, sc_ref, sh_ref, wq_ref, wkv_ref, wg_ref, wz_ref, wgu_ref, bg_ref,
                   qa_ref, kva_ref, g_ref, la_ref):
    x = x_ref[0]
    h = _rms(x) * (1.0 + sc_ref[0]) + sh_ref[0]
    hb = h.astype(BF16)
    qa_ref[0] = (_dot(hb, wq_ref[...]) * (ATTN_HEAD_DIM ** -0.5 * LOG2E)).astype(BF16)
    kva_ref[0] = _dot(hb, wkv_ref[...]).astype(BF16)
    g = _dot(hb, wg_ref[...])
    g_ref[0, :, :GLA_K] = (g[:, :GLA_K] * (GLA_DK ** -0.5)).astype(BF16)
    g_ref[0, :, GLA_K:] = g[:, GLA_K:].astype(BF16)
    z = _dot(hb, wz_ref[...])
    gp = _dot(z.astype(BF16), wgu_ref[...]) + bg_ref[...]
    log_sig = jnp.minimum(gp, 0.0) - jnp.log1p(jnp.exp(-jnp.abs(gp)))
    la_ref[0] = log_sig * (1.0 / GLA_GATE_NORM)


def _inproj(x, scale1, shift1, wq, wkv, wg, wz, wgu, bg, tm):
    bsz, t, d = x.shape
    const = lambda shape: pl.BlockSpec(shape, lambda b, i: (0,) * len(shape),
                                       pipeline_mode=pl.Buffered(1))
    tile = lambda n: pl.BlockSpec((1, tm, n), lambda b, i: (b, i, 0))
    mod = pl.BlockSpec((1, 1, d), lambda b, i: (b, 0, 0))
    return pl.pallas_call(
        _inproj_kernel,
        grid=(bsz, t // tm),
        in_specs=[tile(d), mod, mod, const(wq.shape), const(wkv.shape), const(wg.shape),
                  const(wz.shape), const(wgu.shape), const(bg.shape)],
        out_specs=[tile(ATTN_Q), tile(2 * ATTN_KV), tile(GLA_PACK), tile(GLA_K)],
        out_shape=[jax.ShapeDtypeStruct((bsz, t, ATTN_Q), BF16),
                   jax.ShapeDtypeStruct((bsz, t, 2 * ATTN_KV), BF16),
                   jax.ShapeDtypeStruct((bsz, t, GLA_PACK), BF16),
                   jax.ShapeDtypeStruct((bsz, t, GLA_K), F32)],
        compiler_params=pltpu.CompilerParams(
            dimension_semantics=("arbitrary", "arbitrary"), vmem_limit_bytes=VMEM_LIMIT),
        name="inproj",
    )(x, scale1, shift1, wq, wkv, wg, wz, wgu, bg)


def _swa_kernel(q_ref, kv_ref, kvp_ref, bias_ref, sink_ref, o_ref, *, nblk):
    L = ATTN_BLOCK
    G = ATTN_GROUP
    dh = ATTN_HEAD_DIM
    t = pl.program_id(1)
    low = lax.broadcasted_iota(jnp.int32, (L, LANES), 1) < HALF
    key = lax.broadcasted_iota(jnp.int32, (L, G * L), 0)
    qry = lax.broadcasted_iota(jnp.int32, (L, G * L), 1) & (L - 1)
    prev_key = key > qry
    zero = jnp.zeros((), BF16)
    for n in range(nblk):
        rows = slice(n * L, (n + 1) * L)
        kv_prev = kvp_ref[0] if n == 0 else kv_ref[0, (n - 1) * L:n * L, :]
        kv2 = jnp.concatenate([kv_prev, kv_ref[0, rows, :]], axis=0)
        k2 = kv2[:, :ATTN_KV]
        v2 = kv2[:, ATTN_KV:]
        sel = jnp.where(t == 0, 0, 1) if n == 0 else 1
        for g in range(ATTN_KV_HEADS):
            keep = low if g == 0 else jnp.logical_not(low)
            qs = jnp.concatenate(
                [jnp.where(keep, q_ref[0, rows, p * LANES:(p + 1) * LANES], zero) for p in range(G)],
                axis=0)
            s = lax.dot_general(k2, qs, NT_DIMS, preferred_element_type=F32)
            sc = jnp.where(prev_key, s[:L], s[L:]) + bias_ref[sel, g]
            sink = sink_ref[g]
            m = jnp.maximum(jnp.max(sc, axis=0, keepdims=True), sink)
            p_ = jnp.exp2(sc - m)
            denom = jnp.sum(p_, axis=0, keepdims=True) + jnp.exp2(sink - m)
            p2 = jnp.concatenate([jnp.where(prev_key, p_, 0.0), jnp.where(prev_key, 0.0, p_)],
                                 axis=0).astype(BF16)
            o = lax.dot_general(v2, p2, TN_DIMS, preferred_element_type=F32)
            o = (o[g * dh:(g + 1) * dh] * (1.0 / denom)).astype(BF16)
            for r in range(G):
                h = g * G + r
                o_ref[0, h * dh:(h + 1) * dh, rows] = o[:, r * L:(r + 1) * L]


def _swa(qa, kva, bias, sinkrow, tq):
    bsz, t, _ = qa.shape
    nblk = tq // ATTN_BLOCK
    whole = lambda a: pl.BlockSpec(a.shape, lambda b, i: (0,) * a.ndim, pipeline_mode=pl.Buffered(1))
    return pl.pallas_call(
        functools.partial(_swa_kernel, nblk=nblk),
        grid=(bsz, t // tq),
        in_specs=[pl.BlockSpec((1, tq, ATTN_Q), lambda b, i: (b, i, 0)),
                  pl.BlockSpec((1, tq, 2 * ATTN_KV), lambda b, i: (b, i, 0)),
                  pl.BlockSpec((1, ATTN_BLOCK, 2 * ATTN_KV),
                               lambda b, i: (b, jnp.maximum(i * nblk - 1, 0), 0)),
                  whole(bias), whole(sinkrow)],
        out_specs=pl.BlockSpec((1, ATTN_Q, tq), lambda b, i: (b, 0, i)),
        out_shape=jax.ShapeDtypeStruct((bsz, ATTN_Q, t), BF16),
        compiler_params=pltpu.CompilerParams(
            dimension_semantics=("arbitrary", "arbitrary"), vmem_limit_bytes=VMEM_LIMIT),
        name="swa",
    )(qa, kva, kva, bias, sinkrow)


def _gla_kernel(g_ref, la_ref, gn_ref, o_ref, st_ref, *, nchunk):
    C = GLA_CHUNK
    dk, dv = GLA_DK, GLA_DV

    @pl.when(pl.program_id(1) == 0)
    def _():
        st_ref[...] = jnp.zeros_like(st_ref)

    ri = lax.broadcasted_iota(jnp.int32, (C, C), 0)
    ci = lax.broadcasted_iota(jnp.int32, (C, C), 1)
    tril = (ri >= ci).astype(BF16)
    ri2 = lax.broadcasted_iota(jnp.int32, (C, 2 * C), 0)
    ci2 = lax.broadcasted_iota(jnp.int32, (C, 2 * C), 1) & (C - 1)
    causal2 = ri2 >= ci2
    low_k = lax.broadcasted_iota(jnp.int32, (C, 2 * dk), 1) < dk
    low_v = lax.broadcasted_iota(jnp.int32, (C, 2 * dv), 1) < dv
    diag = (lax.broadcasted_iota(jnp.int32, (2 * dv, 2 * dk), 0) < dv) == \
           (lax.broadcasted_iota(jnp.int32, (2 * dv, 2 * dk), 1) < dk)
    zero = jnp.zeros((), BF16)
    st = [st_ref[p] for p in range(GLA_PAIRS)]
    for c in range(nchunk):
        rows = slice(c * C, (c + 1) * C)
        la = la_ref[0, rows, :]
        la_hi = la.astype(BF16)
        rem = la - la_hi.astype(F32)
        la_mid = rem.astype(BF16)
        la_lo = (rem - la_mid.astype(F32)).astype(BF16)
        b = _dot(tril, la_hi) + _dot(tril, la_mid) + _dot(tril, la_lo)
        b_mid = b[C // 2 - 1:C // 2, :]
        b_last = b[C - 1:C, :]
        q = g_ref[0, rows, 0:GLA_K].astype(F32)
        k = g_ref[0, rows, GLA_K:2 * GLA_K].astype(F32)
        q_in = (q * jnp.exp(b)).astype(BF16)
        q_t = (q * jnp.exp(b - b_mid)).astype(BF16)
        k_t = (k * jnp.exp(b_mid - b)).astype(BF16)
        k_s = (k * jnp.exp(b_last - b)).astype(BF16)
        decay = jnp.exp(b_last)
        for p in range(GLA_PAIRS):
            ks = slice(p * 2 * dk, (p + 1) * 2 * dk)
            vs = slice(2 * GLA_K + p * 2 * dv, 2 * GLA_K + (p + 1) * 2 * dv)
            gs = slice(2 * GLA_K + GLA_V + p * 2 * dv, 2 * GLA_K + GLA_V + (p + 1) * 2 * dv)
            kt_p = k_t[:, ks]
            kbd = jnp.concatenate([jnp.where(low_k, kt_p, zero), jnp.where(low_k, zero, kt_p)], axis=0)
            a = lax.dot_general(q_t[:, ks], kbd, NT_DIMS, preferred_element_type=F32)
            a = jnp.where(causal2, a, 0.0).astype(BF16)
            v_p = g_ref[0, rows, vs]
            vbd = jnp.concatenate([jnp.where(low_v, v_p, zero), jnp.where(low_v, zero, v_p)], axis=0)
            o_p = _dot(a, vbd) + lax.dot_general(q_in[:, ks], st[p].astype(BF16), NT_DIMS,
                                                 preferred_element_type=F32)
            dst = lax.dot_general(v_p, k_s[:, ks], TN_DIMS, preferred_element_type=F32)
            st[p] = st[p] * decay[:, ks] + jnp.where(diag, dst, 0.0)
            go = g_ref[0, rows, gs].astype(F32)
            o_n = jnp.concatenate([_rms(o_p[:, :dv]) * gn_ref[...], _rms(o_p[:, dv:]) * gn_ref[...]],
                                  axis=1)
            o_ref[0, rows, p * 2 * dv:(p + 1) * 2 * dv] = (o_n * (go * jax.nn.sigmoid(go))).astype(BF16)
    for p in range(GLA_PAIRS):
        st_ref[p] = st[p]


def _gla(gpack, la, gn, tg):
    bsz, t, _ = gpack.shape
    return pl.pallas_call(
        functools.partial(_gla_kernel, nchunk=tg // GLA_CHUNK),
        grid=(bsz, t // tg),
        in_specs=[pl.BlockSpec((1, tg, GLA_PACK), lambda b, i: (b, i, 0)),
                  pl.BlockSpec((1, tg, GLA_K), lambda b, i: (b, i, 0)),
                  pl.BlockSpec((1, GLA_DV), lambda b, i: (0, 0))],
        out_specs=pl.BlockSpec((1, tg, GLA_V), lambda b, i: (b, i, 0)),
        out_shape=jax.ShapeDtypeStruct((bsz, t, GLA_V), BF16),
        scratch_shapes=[pltpu.VMEM((GLA_PAIRS, 2 * GLA_DV, 2 * GLA_DK), F32)],
        compiler_params=pltpu.CompilerParams(
            dimension_semantics=("arbitrary", "arbitrary"), vmem_limit_bytes=VMEM_LIMIT),
        name="gla",
    )(gpack, la, gn)


def _mlp_kernel(x_ref, attn_ref, gla_ref, g1_ref, sh2_ref, sc2_ref, g2_ref, wo_ref, w1_ref, w2_ref,
                fg_ref, o_ref, u_ref, *, fchunk, final):
    mix = (lax.dot_general(attn_ref[0], wo_ref[:ATTN_Q, :], TN_DIMS, preferred_element_type=F32)
           + _dot(gla_ref[0], wo_ref[ATTN_Q:, :]))
    x1 = x_ref[0] + g1_ref[0] * mix
    hb = (_rms(x1) * (1.0 + sc2_ref[0]) + sh2_ref[0]).astype(BF16)
    for c in range(D_FF // fchunk):
        cols = slice(c * fchunk, (c + 1) * fchunk)
        u = jnp.maximum(_dot(hb, w1_ref[:, cols]), 0.0)
        u_ref[:, cols] = (u * u).astype(BF16)
    x2 = x1 + g2_ref[0] * _dot(u_ref[...], w2_ref[...])
    o_ref[0] = _rms(x2) * fg_ref[...] if final else x2


def _mlp(x, attn, gla, gate1, shift2, scale2, gate2, wo, w1, w2, fg, tm, final):
    bsz, t, d = x.shape
    const = lambda shape: pl.BlockSpec(shape, lambda b, i: (0,) * len(shape),
                                       pipeline_mode=pl.Buffered(1))
    tile = lambda n: pl.BlockSpec((1, tm, n), lambda b, i: (b, i, 0))
    mod = pl.BlockSpec((1, 1, d), lambda b, i: (b, 0, 0))
    return pl.pallas_call(
        functools.partial(_mlp_kernel, fchunk=1024, final=final),
        grid=(bsz, t // tm),
        in_specs=[tile(d), pl.BlockSpec((1, ATTN_Q, tm), lambda b, i: (b, 0, i)), tile(GLA_V),
                  mod, mod, mod, mod,
                  const(wo.shape), const(w1.shape), const(w2.shape), const(fg.shape)],
        out_specs=tile(d),
        out_shape=jax.ShapeDtypeStruct((bsz, t, d), F32),
        scratch_shapes=[pltpu.VMEM((tm, D_FF), BF16)],
        compiler_params=pltpu.CompilerParams(
            dimension_semantics=("arbitrary", "arbitrary"), vmem_limit_bytes=VMEM_LIMIT),
        name="mlp",
    )(x, attn, gla, gate1, shift2, scale2, gate2, wo, w1, w2, fg)


def kernel(x, c, w_ada, b_ada, w_in, w_gate_up, b_gate, gla_norm_g, attn_sinks, rel_bias,
           w_out, w_mlp_in, w_mlp_out, final_norm_g):
    bsz, t, d = x.shape
    depth = w_ada.shape[0]
    tm = min(512, t)
    p_kv = ATTN_Q
    p_g = ATTN_Q + 2 * ATTN_KV
    p_z = p_g + GLA_PACK
    head_cols = np.concatenate([np.arange(h * ATTN_HEAD_DIM, (h + 1) * ATTN_HEAD_DIM)
                                for h in ATTN_HEAD_ORDER])
    for l in range(depth):
        ada = _ada(c, w_ada[l], b_ada[l])
        shift1, scale1, gate1, shift2, scale2, gate2 = [
            ada[:, None, i * d:(i + 1) * d] for i in range(N_MOD)]
        bias, sinkrow = _bias_table(rel_bias, attn_sinks[l])
        w = w_in[l]
        wq = w[:, :p_kv][:, head_cols].astype(BF16)
        wkv = w[:, p_kv:p_g].astype(BF16)
        wg = w[:, p_g:p_z].astype(BF16)
        wz = jnp.pad(w[:, p_z:], ((0, 0), (0, Z_PAD - GLA_GATE_RANK))).astype(BF16)
        wgu = jnp.pad(w_gate_up[l], ((0, Z_PAD - GLA_GATE_RANK), (0, 0))).astype(BF16)
        qa, kva, gpack, la = _inproj(x, scale1, shift1, wq, wkv, wg, wz, wgu,
                                     b_gate[l].reshape(1, GLA_K), tm)
        attn = _swa(qa, kva, bias, sinkrow, tm)
        gla = _gla(gpack, la, gla_norm_g[l].reshape(1, GLA_DV), tm)
        x = _mlp(x, attn, gla, gate1, shift2, scale2, gate2,
                 w_out[l].astype(BF16), w_mlp_in[l].astype(BF16), w_mlp_out[l].astype(BF16),
                 final_norm_g.reshape(1, d), tm, final=(l == depth - 1))
    return x
```

```python
import functools

import numpy as np
import jax
import jax.numpy as jnp
from jax import lax
from jax.experimental import pallas as pl
from jax.experimental.pallas import tpu as pltpu

F32 = jnp.float32
BF16 = jnp.bfloat16

D_MODEL = 1024
ATTN_HEADS = 8
ATTN_KV_HEADS = 2
ATTN_GROUP = ATTN_HEADS // ATTN_KV_HEADS
ATTN_HEAD_DIM = 64
WINDOW = 128
ATTN_BLOCK = 128
NUM_BUCKETS = 32
MAX_DISTANCE = 128
GLA_HEADS = 4
GLA_PAIRS = GLA_HEADS // 2
GLA_DK = 64
GLA_DV = 128
GLA_GATE_RANK = 16
GLA_GATE_NORM = 16.0
GLA_CHUNK = 128
D_FF = 4 * D_MODEL
EPS = 1e-6
N_MOD = 6

ATTN_Q = ATTN_HEADS * ATTN_HEAD_DIM
ATTN_KV = ATTN_KV_HEADS * ATTN_HEAD_DIM
GLA_K = GLA_HEADS * GLA_DK
GLA_V = GLA_HEADS * GLA_DV
GLA_PACK = 2 * GLA_K + 2 * GLA_V
LANES = 128
HALF = LANES // 2
Z_PAD = LANES
LOG2E = 1.4426950408889634

VMEM_LIMIT = 56 * 1024 * 1024

NT_DIMS = (((1,), (1,)), ((), ()))
TN_DIMS = (((0,), (0,)), ((), ()))

ATTN_HEAD_ORDER = tuple(h for p in range(ATTN_GROUP) for h in (p, p + ATTN_GROUP))


def _dot(a, b):
    return jnp.dot(a, b, preferred_element_type=F32)


def _rms(x):
    return x * lax.rsqrt(jnp.mean(x * x, axis=-1, keepdims=True) + EPS)


def _ada_kernel(c_ref, w_ref, b_ref, o_ref):
    c = c_ref[...]
    cond = c * jax.nn.sigmoid(c)
    o_ref[...] = jnp.dot(cond, w_ref[...], preferred_element_type=F32,
                         precision=lax.Precision.HIGHEST) + b_ref[...]


def _ada(c, w_ada, b_ada):
    bsz, d = c.shape
    n = w_ada.shape[1]
    tn = 1536
    return pl.pallas_call(
        _ada_kernel,
        grid=(n // tn,),
        in_specs=[pl.BlockSpec((bsz, d), lambda j: (0, 0)),
                  pl.BlockSpec((d, tn), lambda j: (0, j)),
                  pl.BlockSpec((1, tn), lambda j: (0, j))],
        out_specs=pl.BlockSpec((bsz, tn), lambda j: (0, j)),
        out_shape=jax.ShapeDtypeStruct((bsz, n), F32),
        name="ada",
    )(c, w_ada, b_ada.reshape(1, n))


def _t5_causal_bucket(dist):
    max_exact = NUM_BUCKETS // 2
    d = np.maximum(dist, 0)
    large = max_exact + (np.log(np.maximum(d, max_exact) / max_exact)
                         / np.log(MAX_DISTANCE / max_exact) * (NUM_BUCKETS - max_exact)).astype(np.int32)
    large = np.minimum(large, NUM_BUCKETS - 1)
    return np.where(d < max_exact, d, large).astype(np.int32)


def _bias_kernel(bucket_ref, rb_ref, sink_ref, bias_ref, sinkrow_ref):
    L = ATTN_BLOCK
    bucket = bucket_ref[...]
    key = lax.broadcasted_iota(jnp.int32, (L, L), 0)
    qry = lax.broadcasted_iota(jnp.int32, (L, L), 1)
    prev_key = key > qry
    for g in range(ATTN_KV_HEADS):
        for r in range(ATTN_GROUP):
            h = g * ATTN_GROUP + r
            cols = slice(r * L, (r + 1) * L)
            acc = jnp.zeros((L, L), F32)
            for b in range(NUM_BUCKETS):
                acc = jnp.where(bucket == b, rb_ref[b, h], acc)
            acc = acc * LOG2E
            bias_ref[1, g, :, cols] = acc
            bias_ref[0, g, :, cols] = jnp.where(prev_key, -jnp.inf, acc)
            sinkrow_ref[g, :, cols] = jnp.full((1, L), sink_ref[h] * LOG2E, F32)


def _bias_table(rel_bias, sinks):
    L = ATTN_BLOCK
    c = np.arange(L)[:, None]
    i = np.arange(L)[None, :]
    bucket = _t5_causal_bucket((i - c) % L)
    return pl.pallas_call(
        _bias_kernel,
        in_specs=[pl.BlockSpec(memory_space=pltpu.VMEM),
                  pl.BlockSpec(memory_space=pltpu.SMEM),
                  pl.BlockSpec(memory_space=pltpu.SMEM)],
        out_specs=[pl.BlockSpec(memory_space=pltpu.VMEM), pl.BlockSpec(memory_space=pltpu.VMEM)],
        out_shape=[jax.ShapeDtypeStruct((2, ATTN_KV_HEADS, L, ATTN_GROUP * L), F32),
                   jax.ShapeDtypeStruct((ATTN_KV_HEADS, 1, ATTN_GROUP * L), F32)],
        name="bias_table",
    )(jnp.asarray(bucket), rel_bias.astype(F32), sinks.astype(F32))


def _inproj_kernel(x_ref, sc_ref, sh_ref, wq_ref, wkv_ref, wg_ref, wz_ref, wgu_ref, bg_ref, gn_ref,
                   qa_ref, kva_ref, g_ref, la_ref):
    x = x_ref[0]
    h = _rms(x) * (1.0 + sc_ref[0]) + sh_ref[0]
    hb = h.astype(BF16)
    qa_ref[0] = (_dot(hb, wq_ref[...]) * (ATTN_HEAD_DIM ** -0.5 * LOG2E)).astype(BF16)
    kva_ref[0] = _dot(hb, wkv_ref[...]).astype(BF16)
    g = _dot(hb, wg_ref[...])
    g_ref[0, :, :GLA_K] = (g[:, :GLA_K] * (GLA_DK ** -0.5)).astype(BF16)
    g_ref[0, :, GLA_K:2 * GLA_K + GLA_V] = g[:, GLA_K:2 * GLA_K + GLA_V].astype(BF16)
    go = g[:, 2 * GLA_K + GLA_V:]
    g_ref[0, :, 2 * GLA_K + GLA_V:] = (go * jax.nn.sigmoid(go) * gn_ref[...]).astype(BF16)
    z = _dot(hb, wz_ref[...])
    gp = _dot(z.astype(BF16), wgu_ref[...]) + bg_ref[...]
    log_sig = jnp.minimum(gp, 0.0) - jnp.log1p(jnp.exp(-jnp.abs(gp)))
    la = log_sig * (1.0 / GLA_GATE_NORM)
    la_hi = la.astype(BF16)
    la_ref[0, :, :GLA_K] = la_hi
    la_ref[0, :, GLA_K:] = (la - la_hi.astype(F32)).astype(BF16)


def _inproj(x, scale1, shift1, wq, wkv, wg, wz, wgu, bg, gn, tm):
    bsz, t, d = x.shape
    const = lambda shape: pl.BlockSpec(shape, lambda b, i: (0,) * len(shape),
                                       pipeline_mode=pl.Buffered(1))
    tile = lambda n: pl.BlockSpec((1, tm, n), lambda b, i: (b, i, 0))
    mod = pl.BlockSpec((1, 1, d), lambda b, i: (b, 0, 0))
    return pl.pallas_call(
        _inproj_kernel,
        grid=(bsz, t // tm),
        in_specs=[tile(d), mod, mod, const(wq.shape), const(wkv.shape), const(wg.shape),
                  const(wz.shape), const(wgu.shape), const(bg.shape), const(gn.shape)],
        out_specs=[tile(ATTN_Q), tile(2 * ATTN_KV), tile(GLA_PACK), tile(2 * GLA_K)],
        out_shape=[jax.ShapeDtypeStruct((bsz, t, ATTN_Q), BF16),
                   jax.ShapeDtypeStruct((bsz, t, 2 * ATTN_KV), BF16),
                   jax.ShapeDtypeStruct((bsz, t, GLA_PACK), BF16),
                   jax.ShapeDtypeStruct((bsz, t, 2 * GLA_K), BF16)],
        compiler_params=pltpu.CompilerParams(
            dimension_semantics=("arbitrary", "arbitrary"), vmem_limit_bytes=VMEM_LIMIT),
        name="inproj",
    )(x, scale1, shift1, wq, wkv, wg, wz, wgu, bg, gn)


def _swa_scores(n, g, q_ref, kv_ref, kvp_ref, bias_ref):
    L = ATTN_BLOCK
    G = ATTN_GROUP
    low = lax.broadcasted_iota(jnp.int32, (L, LANES), 1) < HALF
    key = lax.broadcasted_iota(jnp.int32, (L, G * L), 0)
    qry = lax.broadcasted_iota(jnp.int32, (L, G * L), 1) & (L - 1)
    prev_key = key > qry
    zero = jnp.zeros((), BF16)
    rows = slice(n * L, (n + 1) * L)
    k_prev = kvp_ref[0, :, :ATTN_KV] if n == 0 else kv_ref[0, (n - 1) * L:n * L, :ATTN_KV]
    k2 = jnp.concatenate([k_prev, kv_ref[0, rows, :ATTN_KV]], axis=0)
    sel = jnp.where(pl.program_id(1) == 0, 0, 1) if n == 0 else 1
    keep = low if g == 0 else jnp.logical_not(low)
    qs = jnp.concatenate(
        [jnp.where(keep, q_ref[0, rows, p * LANES:(p + 1) * LANES], zero) for p in range(G)],
        axis=0)
    s = lax.dot_general(k2, qs, NT_DIMS, preferred_element_type=F32)
    return jnp.where(prev_key, s[:L], s[L:]) + bias_ref[sel, g]


def _swa_softmax(g, sc, sink_ref):
    L = ATTN_BLOCK
    G = ATTN_GROUP
    key = lax.broadcasted_iota(jnp.int32, (L, G * L), 0)
    qry = lax.broadcasted_iota(jnp.int32, (L, G * L), 1) & (L - 1)
    prev_key = key > qry
    sink = sink_ref[g]
    m = jnp.maximum(jnp.max(sc, axis=0, keepdims=True), sink)
    p_ = jnp.exp2(sc - m)
    denom = jnp.sum(p_, axis=0, keepdims=True) + jnp.exp2(sink - m)
    p2 = jnp.concatenate([jnp.where(prev_key, p_, 0.0), jnp.where(prev_key, 0.0, p_)],
                         axis=0).astype(BF16)
    return p2, denom


def _swa_pv(n, g, p2, denom, kv_ref, kvp_ref, o_ref):
    L = ATTN_BLOCK
    G = ATTN_GROUP
    dh = ATTN_HEAD_DIM
    rows = slice(n * L, (n + 1) * L)
    v_prev = kvp_ref[0, :, ATTN_KV:] if n == 0 else kv_ref[0, (n - 1) * L:n * L, ATTN_KV:]
    v2 = jnp.concatenate([v_prev, kv_ref[0, rows, ATTN_KV:]], axis=0)
    o = lax.dot_general(v2, p2, TN_DIMS, preferred_element_type=F32)
    o = (o[g * dh:(g + 1) * dh] * (1.0 / denom)).astype(BF16)
    for r in range(G):
        h = g * G + r
        o_ref[0, h * dh:(h + 1) * dh, rows] = o[:, r * L:(r + 1) * L]


def _gla_prep(c, g_ref, la_ref):
    C = GLA_CHUNK
    ri = lax.broadcasted_iota(jnp.int32, (C, C), 0)
    ci = lax.broadcasted_iota(jnp.int32, (C, C), 1)
    tril = (ri >= ci).astype(BF16)
    rows = slice(c * C, (c + 1) * C)
    b = _dot(tril, la_ref[0, rows, :GLA_K]) + _dot(tril, la_ref[0, rows, GLA_K:])
    b_mid = b[C // 2 - 1:C // 2, :]
    b_last = b[C - 1:C, :]
    q = g_ref[0, rows, 0:GLA_K].astype(F32)
    k = g_ref[0, rows, GLA_K:2 * GLA_K].astype(F32)
    q_t = q * jnp.exp(b - b_mid)
    k_t = k * jnp.exp(b_mid - b)
    q_in = (q_t * jnp.exp(b_mid)).astype(BF16)
    k_s = (k_t * jnp.exp(b_last - b_mid)).astype(BF16)
    return q_in, q_t.astype(BF16), k_t.astype(BF16), k_s, jnp.exp(b_last)


def _gla_intra(prep):
    C = GLA_CHUNK
    dk = GLA_DK
    _, q_t, k_t, _, _ = prep
    ri2 = lax.broadcasted_iota(jnp.int32, (C, 2 * C), 0)
    ci2 = lax.broadcasted_iota(jnp.int32, (C, 2 * C), 1) & (C - 1)
    causal2 = ri2 >= ci2
    low_k = lax.broadcasted_iota(jnp.int32, (C, 2 * dk), 1) < dk
    zero = jnp.zeros((), BF16)
    out = []
    for p in range(GLA_PAIRS):
        ks = slice(p * 2 * dk, (p + 1) * 2 * dk)
        kt_p = k_t[:, ks]
        kbd = jnp.concatenate([jnp.where(low_k, kt_p, zero), jnp.where(low_k, zero, kt_p)], axis=0)
        a = lax.dot_general(q_t[:, ks], kbd, NT_DIMS, preferred_element_type=F32)
        out.append(jnp.where(causal2, a, 0.0).astype(BF16))
    return out


def _gla_out(c, prep, a, st, g_ref, o_ref):
    C = GLA_CHUNK
    dk, dv = GLA_DK, GLA_DV
    q_in, _, _, k_s, decay = prep
    low_v = lax.broadcasted_iota(jnp.int32, (C, 2 * dv), 1) < dv
    diag = (lax.broadcasted_iota(jnp.int32, (2 * dv, 2 * dk), 0) < dv) == \
           (lax.broadcasted_iota(jnp.int32, (2 * dv, 2 * dk), 1) < dk)
    zero = jnp.zeros((), BF16)
    rows = slice(c * C, (c + 1) * C)
    st_new = []
    for p in range(GLA_PAIRS):
        ks = slice(p * 2 * dk, (p + 1) * 2 * dk)
        vs = slice(2 * GLA_K + p * 2 * dv, 2 * GLA_K + (p + 1) * 2 * dv)
        gs = slice(2 * GLA_K + GLA_V + p * 2 * dv, 2 * GLA_K + GLA_V + (p + 1) * 2 * dv)
        v_p = g_ref[0, rows, vs]
        vbd = jnp.concatenate([jnp.where(low_v, v_p, zero), jnp.where(low_v, zero, v_p)], axis=0)
        o_p = _dot(a[p], vbd) + lax.dot_general(q_in[:, ks], st[p].astype(BF16), NT_DIMS,
                                                preferred_element_type=F32)
        dst = lax.dot_general(v_p, k_s[:, ks], TN_DIMS, preferred_element_type=F32)
        st_new.append(st[p] * decay[:, ks] + jnp.where(diag, dst, 0.0))
        o_n = jnp.concatenate([_rms(o_p[:, :dv]), _rms(o_p[:, dv:])], axis=1)
        o_ref[0, rows, p * 2 * dv:(p + 1) * 2 * dv] = (o_n * g_ref[0, rows, gs].astype(F32)).astype(BF16)
    return st_new


def _mixer_kernel(q_ref, kv_ref, kvp_ref, bias_ref, sink_ref, g_ref, la_ref,
                  attn_ref, gla_ref, st_ref, *, nblk):
    @pl.when(pl.program_id(1) == 0)
    def _():
        st_ref[...] = jnp.zeros_like(st_ref)

    st = [st_ref[p] for p in range(GLA_PAIRS)]
    units = [(n, g) for n in range(nblk) for g in range(ATTN_KV_HEADS)]
    nu = len(units)
    sc, pd, prep, intra = {}, {}, {}, {}

    def emit(kind, i):
        nonlocal st
        if kind == "prep" and 0 <= i < nblk:
            prep[i] = _gla_prep(i, g_ref, la_ref)
        elif kind == "intra" and 0 <= i < nblk:
            intra[i] = _gla_intra(prep[i])
        elif kind == "out" and 0 <= i < nblk:
            st = _gla_out(i, prep.pop(i), intra.pop(i), st, g_ref, gla_ref)
        elif kind == "scores" and 0 <= i < nu:
            sc[i] = _swa_scores(*units[i], q_ref, kv_ref, kvp_ref, bias_ref)
        elif kind == "softmax" and 0 <= i < nu:
            pd[i] = _swa_softmax(units[i][1], sc.pop(i), sink_ref)
        elif kind == "pv" and 0 <= i < nu:
            _swa_pv(*units[i], *pd.pop(i), kv_ref, kvp_ref, attn_ref)

    for slot in range(nu + 2 * ATTN_KV_HEADS):
        if slot % ATTN_KV_HEADS == 0:
            c = slot // ATTN_KV_HEADS
            emit("out", c - 2)
            emit("intra", c - 1)
            emit("prep", c)
        emit("pv", slot - 2)
        emit("softmax", slot - 1)
        emit("scores", slot)
    for p in range(GLA_PAIRS):
        st_ref[p] = st[p]


def _mixer(qa, kva, bias, sinkrow, gpack, la, tq):
    assert ATTN_BLOCK == GLA_CHUNK
    bsz, t, _ = qa.shape
    nblk = tq // ATTN_BLOCK
    whole = lambda a: pl.BlockSpec(a.shape, lambda b, i: (0,) * a.ndim, pipeline_mode=pl.Buffered(1))
    tile = lambda n: pl.BlockSpec((1, tq, n), lambda b, i: (b, i, 0))
    return pl.pallas_call(
        functools.partial(_mixer_kernel, nblk=nblk),
        grid=(bsz, t // tq),
        in_specs=[tile(ATTN_Q), tile(2 * ATTN_KV),
                  pl.BlockSpec((1, ATTN_BLOCK, 2 * ATTN_KV),
                               lambda b, i: (b, jnp.maximum(i * nblk - 1, 0), 0)),
                  whole(bias), whole(sinkrow), tile(GLA_PACK), tile(2 * GLA_K)],
        out_specs=[pl.BlockSpec((1, ATTN_Q, tq), lambda b, i: (b, 0, i)), tile(GLA_V)],
        out_shape=[jax.ShapeDtypeStruct((bsz, ATTN_Q, t), BF16),
                   jax.ShapeDtypeStruct((bsz, t, GLA_V), BF16)],
        scratch_shapes=[pltpu.VMEM((GLA_PAIRS, 2 * GLA_DV, 2 * GLA_DK), F32)],
        compiler_params=pltpu.CompilerParams(
            dimension_semantics=("arbitrary", "arbitrary"), vmem_limit_bytes=VMEM_LIMIT),
        name="mixer",
    )(qa, kva, kva, bias, sinkrow, gpack, la)


def _mlp_kernel(x_ref, attn_ref, gla_ref, g1_ref, sh2_ref, sc2_ref, g2_ref, wo_ref, w1_ref, w2_ref,
                fg_ref, o_ref, u_ref, *, fchunk, final):
    mix = (lax.dot_general(attn_ref[0], wo_ref[:ATTN_Q, :], TN_DIMS, preferred_element_type=F32)
           + _dot(gla_ref[0], wo_ref[ATTN_Q:, :]))
    x1 = x_ref[0] + g1_ref[0] * mix
    hb = (_rms(x1) * (1.0 + sc2_ref[0]) + sh2_ref[0]).astype(BF16)
    for c in range(D_FF // fchunk):
        cols = slice(c * fchunk, (c + 1) * fchunk)
        u = jnp.maximum(_dot(hb, w1_ref[:, cols]), 0.0)
        u_ref[:, cols] = (u * u).astype(BF16)
    x2 = x1 + g2_ref[0] * _dot(u_ref[...], w2_ref[...])
    o_ref[0] = _rms(x2) * fg_ref[...] if final else x2


def _mlp(x, attn, gla, gate1, shift2, scale2, gate2, wo, w1, w2, fg, tm, final):
    bsz, t, d = x.shape
    const = lambda shape: pl.BlockSpec(shape, lambda b, i: (0,) * len(shape),
                                       pipeline_mode=pl.Buffered(1))
    tile = lambda n: pl.BlockSpec((1, tm, n), lambda b, i: (b, i, 0))
    mod = pl.BlockSpec((1, 1, d), lambda b, i: (b, 0, 0))
    return pl.pallas_call(
        functools.partial(_mlp_kernel, fchunk=1024, final=final),
        grid=(bsz, t // tm),
        in_specs=[tile(d), pl.BlockSpec((1, ATTN_Q, tm), lambda b, i: (b, 0, i)), tile(GLA_V),
                  mod, mod, mod, mod,
                  const(wo.shape), const(w1.shape), const(w2.shape), const(fg.shape)],
        out_specs=tile(d),
        out_shape=jax.ShapeDtypeStruct((bsz, t, d), F32),
        scratch_shapes=[pltpu.VMEM((tm, D_FF), BF16)],
        compiler_params=pltpu.CompilerParams(
            dimension_semantics=("arbitrary", "arbitrary"), vmem_limit_bytes=VMEM_LIMIT),
        name="mlp",
    )(x, attn, gla, gate1, shift2, scale2, gate2, wo, w1, w2, fg)


def kernel(x, c, w_ada, b_ada, w_in, w_gate_up, b_gate, gla_norm_g, attn_sinks, rel_bias,
           w_out, w_mlp_in, w_mlp_out, final_norm_g):
    bsz, t, d = x.shape
    depth = w_ada.shape[0]
    tm = min(512, t)
    p_kv = ATTN_Q
    p_g = ATTN_Q + 2 * ATTN_KV
    p_z = p_g + GLA_PACK
    head_cols = np.concatenate([np.arange(h * ATTN_HEAD_DIM, (h + 1) * ATTN_HEAD_DIM)
                                for h in ATTN_HEAD_ORDER])
    for l in range(depth):
        ada = _ada(c, w_ada[l], b_ada[l])
        shift1, scale1, gate1, shift2, scale2, gate2 = [
            ada[:, None, i * d:(i + 1) * d] for i in range(N_MOD)]
        bias, sinkrow = _bias_table(rel_bias, attn_sinks[l])
        w = w_in[l]
        wq = w[:, :p_kv][:, head_cols].astype(BF16)
        wkv = w[:, p_kv:p_g].astype(BF16)
        wg = w[:, p_g:p_z].astype(BF16)
        wz = jnp.pad(w[:, p_z:], ((0, 0), (0, Z_PAD - GLA_GATE_RANK))).astype(BF16)
        wgu = jnp.pad(w_gate_up[l], ((0, Z_PAD - GLA_GATE_RANK), (0, 0))).astype(BF16)
        qa, kva, gpack, la = _inproj(x, scale1, shift1, wq, wkv, wg, wz, wgu,
                                     b_gate[l].reshape(1, GLA_K),
                                     jnp.tile(gla_norm_g[l], GLA_HEADS).reshape(1, GLA_V), tm)
        attn, gla = _mixer(qa, kva, bias, sinkrow, gpack, la, tm)
        x = _mlp(x, attn, gla, gate1, shift2, scale2, gate2,
                 w_out[l].astype(BF16), w_mlp_in[l].astype(BF16), w_mlp_out[l].astype(BF16),
                 final_norm_g.reshape(1, d), tm, final=(l == depth - 1))
    return x
```

```python
import functools

import numpy as np
import jax
import jax.numpy as jnp
from jax import lax
from jax.experimental import pallas as pl
from jax.experimental.pallas import tpu as pltpu

F32 = jnp.float32
BF16 = jnp.bfloat16

D_MODEL = 1024
ATTN_HEADS = 8
ATTN_KV_HEADS = 2
ATTN_GROUP = ATTN_HEADS // ATTN_KV_HEADS
ATTN_HEAD_DIM = 64
WINDOW = 128
ATTN_BLOCK = 128
NUM_BUCKETS = 32
MAX_DISTANCE = 128
GLA_HEADS = 4
GLA_PAIRS = GLA_HEADS // 2
GLA_DK = 64
GLA_DV = 128
GLA_GATE_RANK = 16
GLA_GATE_NORM = 16.0
GLA_CHUNK = 128
D_FF = 4 * D_MODEL
EPS = 1e-6
N_MOD = 6

ATTN_Q = ATTN_HEADS * ATTN_HEAD_DIM
ATTN_KV = ATTN_KV_HEADS * ATTN_HEAD_DIM
GLA_K = GLA_HEADS * GLA_DK
GLA_V = GLA_HEADS * GLA_DV
GLA_PACK = 2 * GLA_K + 2 * GLA_V
LANES = 128
HALF = LANES // 2
Z_PAD = LANES
LOG2E = 1.4426950408889634

VMEM_LIMIT = 56 * 1024 * 1024
TILE_TOKENS = 1024
SUB_TILE_TOKENS = 512

NT_DIMS = (((1,), (1,)), ((), ()))
TN_DIMS = (((0,), (0,)), ((), ()))

ATTN_HEAD_ORDER = tuple(h for p in range(ATTN_GROUP) for h in (p, p + ATTN_GROUP))


def _dot(a, b):
    return jnp.dot(a, b, preferred_element_type=F32)


def _rms(x):
    return x * lax.rsqrt(jnp.mean(x * x, axis=-1, keepdims=True) + EPS)


def _ada_kernel(c_ref, w_ref, b_ref, o_ref):
    c = c_ref[...]
    cond = c * jax.nn.sigmoid(c)
    o_ref[...] = jnp.dot(cond, w_ref[...], preferred_element_type=F32,
                         precision=lax.Precision.HIGHEST) + b_ref[...]


def _ada(c, w_ada, b_ada):
    bsz, d = c.shape
    n = w_ada.shape[1]
    tn = 1536
    return pl.pallas_call(
        _ada_kernel,
        grid=(n // tn,),
        in_specs=[pl.BlockSpec((bsz, d), lambda j: (0, 0)),
                  pl.BlockSpec((d, tn), lambda j: (0, j)),
                  pl.BlockSpec((1, tn), lambda j: (0, j))],
        out_specs=pl.BlockSpec((bsz, tn), lambda j: (0, j)),
        out_shape=jax.ShapeDtypeStruct((bsz, n), F32),
        name="ada",
    )(c, w_ada, b_ada.reshape(1, n))


def _t5_causal_bucket(dist):
    max_exact = NUM_BUCKETS // 2
    d = np.maximum(dist, 0)
    large = max_exact + (np.log(np.maximum(d, max_exact) / max_exact)
                         / np.log(MAX_DISTANCE / max_exact) * (NUM_BUCKETS - max_exact)).astype(np.int32)
    large = np.minimum(large, NUM_BUCKETS - 1)
    return np.where(d < max_exact, d, large).astype(np.int32)


def _bias_kernel(bucket_ref, rb_ref, sink_ref, bias_ref, sinkrow_ref):
    L = ATTN_BLOCK
    bucket = bucket_ref[...]
    key = lax.broadcasted_iota(jnp.int32, (L, L), 0)
    qry = lax.broadcasted_iota(jnp.int32, (L, L), 1)
    prev_key = key > qry
    for g in range(ATTN_KV_HEADS):
        for r in range(ATTN_GROUP):
            h = g * ATTN_GROUP + r
            cols = slice(r * L, (r + 1) * L)
            acc = jnp.zeros((L, L), F32)
            for b in range(NUM_BUCKETS):
                acc = jnp.where(bucket == b, rb_ref[b, h], acc)
            acc = acc * LOG2E
            bias_ref[1, g, :, cols] = acc
            bias_ref[0, g, :, cols] = jnp.where(prev_key, -jnp.inf, acc)
            sinkrow_ref[g, :, cols] = jnp.full((1, L), sink_ref[h] * LOG2E, F32)


def _bias_table(rel_bias, sinks):
    L = ATTN_BLOCK
    c = np.arange(L)[:, None]
    i = np.arange(L)[None, :]
    bucket = _t5_causal_bucket((i - c) % L)
    return pl.pallas_call(
        _bias_kernel,
        in_specs=[pl.BlockSpec(memory_space=pltpu.VMEM),
                  pl.BlockSpec(memory_space=pltpu.SMEM),
                  pl.BlockSpec(memory_space=pltpu.SMEM)],
        out_specs=[pl.BlockSpec(memory_space=pltpu.VMEM), pl.BlockSpec(memory_space=pltpu.VMEM)],
        out_shape=[jax.ShapeDtypeStruct((2, ATTN_KV_HEADS, L, ATTN_GROUP * L), F32),
                   jax.ShapeDtypeStruct((ATTN_KV_HEADS, 1, ATTN_GROUP * L), F32)],
        name="bias_table",
    )(jnp.asarray(bucket), rel_bias.astype(F32), sinks.astype(F32))


def _inproj_kernel(x_ref, sc_ref, sh_ref, w_ref, wgu_ref, bg_ref, gn_ref,
                   qa_ref, kva_ref, g_ref, la_ref, *, nsub):
    rs = x_ref.shape[1] // nsub
    p_kv = ATTN_Q
    p_g = p_kv + 2 * ATTN_KV
    p_go = p_g + 2 * GLA_K + GLA_V
    p_z = p_g + GLA_PACK

    def norm(s):
        x = x_ref[0, s * rs:(s + 1) * rs, :]
        return (_rms(x) * (1.0 + sc_ref[0]) + sh_ref[0]).astype(BF16)

    def finish(s, y):
        rows = slice(s * rs, (s + 1) * rs)
        gp = _dot(y[:, p_z:].astype(BF16), wgu_ref[...]) + bg_ref[...]
        log_sig = jnp.minimum(gp, 0.0) - jnp.log1p(jnp.exp(-jnp.abs(gp)))
        la = log_sig * (1.0 / GLA_GATE_NORM)
        la_hi = la.astype(BF16)
        la_ref[0, rows, :GLA_K] = la_hi
        la_ref[0, rows, GLA_K:] = (la - la_hi.astype(F32)).astype(BF16)
        go = y[:, p_go:p_z]
        g_ref[0, rows, 2 * GLA_K + GLA_V:] = (go * jax.nn.sigmoid(go) * gn_ref[...]).astype(BF16)
        g_ref[0, rows, :GLA_K] = (y[:, p_g:p_g + GLA_K] * (GLA_DK ** -0.5)).astype(BF16)
        g_ref[0, rows, GLA_K:2 * GLA_K + GLA_V] = y[:, p_g + GLA_K:p_go].astype(BF16)
        qa_ref[0, rows, :] = (y[:, :p_kv] * (ATTN_HEAD_DIM ** -0.5 * LOG2E)).astype(BF16)
        kva_ref[0, rows, :] = y[:, p_kv:p_g].astype(BF16)

    hb = norm(0)
    y_prev = None
    for s in range(nsub):
        y = _dot(hb, w_ref[...])
        if s + 1 < nsub:
            hb = norm(s + 1)
        if y_prev is not None:
            finish(s - 1, y_prev)
        y_prev = y
    finish(nsub - 1, y_prev)


def _inproj(x, scale1, shift1, w, wgu, bg, gn, tm, nsub):
    bsz, t, d = x.shape
    const = lambda shape: pl.BlockSpec(shape, lambda b, i: (0,) * len(shape),
                                       pipeline_mode=pl.Buffered(1))
    tile = lambda n: pl.BlockSpec((1, tm, n), lambda b, i: (b, i, 0))
    mod = pl.BlockSpec((1, 1, d), lambda b, i: (b, 0, 0))
    return pl.pallas_call(
        functools.partial(_inproj_kernel, nsub=nsub),
        grid=(bsz, t // tm),
        in_specs=[tile(d), mod, mod, const(w.shape), const(wgu.shape), const(bg.shape), const(gn.shape)],
        out_specs=[tile(ATTN_Q), tile(2 * ATTN_KV), tile(GLA_PACK), tile(2 * GLA_K)],
        out_shape=[jax.ShapeDtypeStruct((bsz, t, ATTN_Q), BF16),
                   jax.ShapeDtypeStruct((bsz, t, 2 * ATTN_KV), BF16),
                   jax.ShapeDtypeStruct((bsz, t, GLA_PACK), BF16),
                   jax.ShapeDtypeStruct((bsz, t, 2 * GLA_K), BF16)],
        compiler_params=pltpu.CompilerParams(
            dimension_semantics=("arbitrary", "arbitrary"), vmem_limit_bytes=VMEM_LIMIT),
        name="inproj",
    )(x, scale1, shift1, w, wgu, bg, gn)


def _swa_scores(n, g, q_ref, kv_ref, kvp_ref, bias_ref):
    L = ATTN_BLOCK
    G = ATTN_GROUP
    low = lax.broadcasted_iota(jnp.int32, (L, LANES), 1) < HALF
    key = lax.broadcasted_iota(jnp.int32, (L, G * L), 0)
    qry = lax.broadcasted_iota(jnp.int32, (L, G * L), 1) & (L - 1)
    prev_key = key > qry
    zero = jnp.zeros((), BF16)
    rows = slice(n * L, (n + 1) * L)
    k_prev = kvp_ref[0, :, :ATTN_KV] if n == 0 else kv_ref[0, (n - 1) * L:n * L, :ATTN_KV]
    k2 = jnp.concatenate([k_prev, kv_ref[0, rows, :ATTN_KV]], axis=0)
    sel = jnp.where(pl.program_id(1) == 0, 0, 1) if n == 0 else 1
    keep = low if g == 0 else jnp.logical_not(low)
    qs = jnp.concatenate(
        [jnp.where(keep, q_ref[0, rows, p * LANES:(p + 1) * LANES], zero) for p in range(G)],
        axis=0)
    s = lax.dot_general(k2, qs, NT_DIMS, preferred_element_type=F32)
    return jnp.where(prev_key, s[:L], s[L:]) + bias_ref[sel, g]


def _swa_softmax(g, sc, sink_ref):
    L = ATTN_BLOCK
    G = ATTN_GROUP
    key = lax.broadcasted_iota(jnp.int32, (L, G * L), 0)
    qry = lax.broadcasted_iota(jnp.int32, (L, G * L), 1) & (L - 1)
    prev_key = key > qry
    sink = sink_ref[g]
    m = jnp.maximum(jnp.max(sc, axis=0, keepdims=True), sink)
    p_ = jnp.exp2(sc - m)
    denom = jnp.sum(p_, axis=0, keepdims=True) + jnp.exp2(sink - m)
    p2 = jnp.concatenate([jnp.where(prev_key, p_, 0.0), jnp.where(prev_key, 0.0, p_)],
                         axis=0).astype(BF16)
    return p2, denom


def _swa_pv(n, g, p2, denom, kv_ref, kvp_ref, o_ref):
    L = ATTN_BLOCK
    G = ATTN_GROUP
    dh = ATTN_HEAD_DIM
    rows = slice(n * L, (n + 1) * L)
    v_prev = kvp_ref[0, :, ATTN_KV:] if n == 0 else kv_ref[0, (n - 1) * L:n * L, ATTN_KV:]
    v2 = jnp.concatenate([v_prev, kv_ref[0, rows, ATTN_KV:]], axis=0)
    o = lax.dot_general(v2, p2, TN_DIMS, preferred_element_type=F32)
    o = (o[g * dh:(g + 1) * dh] * (1.0 / denom)).astype(BF16)
    for r in range(G):
        h = g * G + r
        o_ref[0, h * dh:(h + 1) * dh, rows] = o[:, r * L:(r + 1) * L]


def _gla_prep(c, g_ref, la_ref):
    C = GLA_CHUNK
    ri = lax.broadcasted_iota(jnp.int32, (C, C), 0)
    ci = lax.broadcasted_iota(jnp.int32, (C, C), 1)
    tril = (ri >= ci).astype(BF16)
    rows = slice(c * C, (c + 1) * C)
    b = _dot(tril, la_ref[0, rows, :GLA_K]) + _dot(tril, la_ref[0, rows, GLA_K:])
    b_mid = b[C // 2 - 1:C // 2, :]
    b_last = b[C - 1:C, :]
    q = g_ref[0, rows, 0:GLA_K].astype(F32)
    k = g_ref[0, rows, GLA_K:2 * GLA_K].astype(F32)
    q_t = q * jnp.exp(b - b_mid)
    k_t = k * jnp.exp(b_mid - b)
    q_in = (q_t * jnp.exp(b_mid)).astype(BF16)
    k_s = (k_t * jnp.exp(b_last - b_mid)).astype(BF16)
    return q_in, q_t.astype(BF16), k_t.astype(BF16), k_s, jnp.exp(b_last)


def _gla_intra(prep):
    C = GLA_CHUNK
    dk = GLA_DK
    _, q_t, k_t, _, _ = prep
    ri2 = lax.broadcasted_iota(jnp.int32, (C, 2 * C), 0)
    ci2 = lax.broadcasted_iota(jnp.int32, (C, 2 * C), 1) & (C - 1)
    causal2 = ri2 >= ci2
    low_k = lax.broadcasted_iota(jnp.int32, (C, 2 * dk), 1) < dk
    zero = jnp.zeros((), BF16)
    out = []
    for p in range(GLA_PAIRS):
        ks = slice(p * 2 * dk, (p + 1) * 2 * dk)
        kt_p = k_t[:, ks]
        kbd = jnp.concatenate([jnp.where(low_k, kt_p, zero), jnp.where(low_k, zero, kt_p)], axis=0)
        a = lax.dot_general(q_t[:, ks], kbd, NT_DIMS, preferred_element_type=F32)
        out.append(jnp.where(causal2, a, 0.0).astype(BF16))
    return out


def _gla_out(c, prep, a, st, g_ref, o_ref):
    C = GLA_CHUNK
    dk, dv = GLA_DK, GLA_DV
    q_in, _, _, k_s, decay = prep
    low_v = lax.broadcasted_iota(jnp.int32, (C, 2 * dv), 1) < dv
    diag = (lax.broadcasted_iota(jnp.int32, (2 * dv, 2 * dk), 0) < dv) == \
           (lax.broadcasted_iota(jnp.int32, (2 * dv, 2 * dk), 1) < dk)
    zero = jnp.zeros((), BF16)
    rows = slice(c * C, (c + 1) * C)
    st_new = []
    for p in range(GLA_PAIRS):
        ks = slice(p * 2 * dk, (p + 1) * 2 * dk)
        vs = slice(2 * GLA_K + p * 2 * dv, 2 * GLA_K + (p + 1) * 2 * dv)
        gs = slice(2 * GLA_K + GLA_V + p * 2 * dv, 2 * GLA_K + GLA_V + (p + 1) * 2 * dv)
        v_p = g_ref[0, rows, vs]
        vbd = jnp.concatenate([jnp.where(low_v, v_p, zero), jnp.where(low_v, zero, v_p)], axis=0)
        o_p = _dot(a[p], vbd) + lax.dot_general(q_in[:, ks], st[p].astype(BF16), NT_DIMS,
                                                preferred_element_type=F32)
        dst = lax.dot_general(v_p, k_s[:, ks], TN_DIMS, preferred_element_type=F32)
        st_new.append(st[p] * decay[:, ks] + jnp.where(diag, dst, 0.0))
        o_n = jnp.concatenate([_rms(o_p[:, :dv]), _rms(o_p[:, dv:])], axis=1)
        o_ref[0, rows, p * 2 * dv:(p + 1) * 2 * dv] = (o_n * g_ref[0, rows, gs].astype(F32)).astype(BF16)
    return st_new


def _mixer_kernel(q_ref, kv_ref, kvp_ref, bias_ref, sink_ref, g_ref, la_ref,
                  attn_ref, gla_ref, st_ref, *, nblk):
    @pl.when(pl.program_id(1) == 0)
    def _():
        st_ref[...] = jnp.zeros_like(st_ref)

    st = [st_ref[p] for p in range(GLA_PAIRS)]
    units = [(n, g) for n in range(nblk) for g in range(ATTN_KV_HEADS)]
    nu = len(units)
    sc, pd, prep, intra = {}, {}, {}, {}

    def emit(kind, i):
        nonlocal st
        if kind == "prep" and 0 <= i < nblk:
            prep[i] = _gla_prep(i, g_ref, la_ref)
        elif kind == "intra" and 0 <= i < nblk:
            intra[i] = _gla_intra(prep[i])
        elif kind == "out" and 0 <= i < nblk:
            st = _gla_out(i, prep.pop(i), intra.pop(i), st, g_ref, gla_ref)
        elif kind == "scores" and 0 <= i < nu:
            sc[i] = _swa_scores(*units[i], q_ref, kv_ref, kvp_ref, bias_ref)
        elif kind == "softmax" and 0 <= i < nu:
            pd[i] = _swa_softmax(units[i][1], sc.pop(i), sink_ref)
        elif kind == "pv" and 0 <= i < nu:
            _swa_pv(*units[i], *pd.pop(i), kv_ref, kvp_ref, attn_ref)

    for slot in range(nu + 2 * ATTN_KV_HEADS):
        if slot % ATTN_KV_HEADS == 0:
            c = slot // ATTN_KV_HEADS
            emit("out", c - 2)
            emit("intra", c - 1)
            emit("prep", c)
        emit("pv", slot - 2)
        emit("softmax", slot - 1)
        emit("scores", slot)
    for p in range(GLA_PAIRS):
        st_ref[p] = st[p]


def _mixer(qa, kva, bias, sinkrow, gpack, la, tq):
    assert ATTN_BLOCK == GLA_CHUNK
    bsz, t, _ = qa.shape
    nblk = tq // ATTN_BLOCK
    whole = lambda a: pl.BlockSpec(a.shape, lambda b, i: (0,) * a.ndim, pipeline_mode=pl.Buffered(1))
    tile = lambda n: pl.BlockSpec((1, tq, n), lambda b, i: (b, i, 0))
    return pl.pallas_call(
        functools.partial(_mixer_kernel, nblk=nblk),
        grid=(bsz, t // tq),
        in_specs=[tile(ATTN_Q), tile(2 * ATTN_KV),
                  pl.BlockSpec((1, ATTN_BLOCK, 2 * ATTN_KV),
                               lambda b, i: (b, jnp.maximum(i * nblk - 1, 0), 0)),
                  whole(bias), whole(sinkrow), tile(GLA_PACK), tile(2 * GLA_K)],
        out_specs=[pl.BlockSpec((1, ATTN_Q, tq), lambda b, i: (b, 0, i)), tile(GLA_V)],
        out_shape=[jax.ShapeDtypeStruct((bsz, ATTN_Q, t), BF16),
                   jax.ShapeDtypeStruct((bsz, t, GLA_V), BF16)],
        scratch_shapes=[pltpu.VMEM((GLA_PAIRS, 2 * GLA_DV, 2 * GLA_DK), F32)],
        compiler_params=pltpu.CompilerParams(
            dimension_semantics=("arbitrary", "arbitrary"), vmem_limit_bytes=VMEM_LIMIT),
        name="mixer",
    )(qa, kva, kva, bias, sinkrow, gpack, la)


def _mlp_kernel(x_ref, attn_ref, gla_ref, g1_ref, sh2_ref, sc2_ref, g2_ref, wo_ref, w1_ref, w2_ref,
                fg_ref, o_ref, u_ref, *, fchunk, final, nsub):
    rs = x_ref.shape[1] // nsub
    nf = D_FF // fchunk

    def pre(s):
        rows = slice(s * rs, (s + 1) * rs)
        mix = (lax.dot_general(attn_ref[0, :, rows], wo_ref[:ATTN_Q, :], TN_DIMS, preferred_element_type=F32)
               + _dot(gla_ref[0, rows, :], wo_ref[ATTN_Q:, :]))
        x1 = x_ref[0, rows, :] + g1_ref[0] * mix
        return x1, (_rms(x1) * (1.0 + sc2_ref[0]) + sh2_ref[0]).astype(BF16)

    def up(s, hb, c):
        cols = slice(c * fchunk, (c + 1) * fchunk)
        u = jnp.maximum(_dot(hb, w1_ref[:, cols]), 0.0)
        u_ref[s, :, cols] = (u * u).astype(BF16)

    def post(s, x1, y):
        x2 = x1 + g2_ref[0] * y
        o_ref[0, s * rs:(s + 1) * rs, :] = _rms(x2) * fg_ref[...] if final else x2

    x1, hb = pre(0)
    pending = None
    for s in range(nsub):
        for c in range(nf // 2):
            up(s, hb, c)
        if s + 1 < nsub:
            nxt = pre(s + 1)
        for c in range(nf // 2, nf):
            up(s, hb, c)
        if pending is not None:
            post(*pending)
        pending = (s, x1, _dot(u_ref[s], w2_ref[...]))
        if s + 1 < nsub:
            x1, hb = nxt
    post(*pending)


def _mlp(x, attn, gla, gate1, shift2, scale2, gate2, wo, w1, w2, fg, tm, final, nsub):
    bsz, t, d = x.shape
    const = lambda shape: pl.BlockSpec(shape, lambda b, i: (0,) * len(shape),
                                       pipeline_mode=pl.Buffered(1))
    tile = lambda n: pl.BlockSpec((1, tm, n), lambda b, i: (b, i, 0))
    mod = pl.BlockSpec((1, 1, d), lambda b, i: (b, 0, 0))
    return pl.pallas_call(
        functools.partial(_mlp_kernel, fchunk=1024, final=final, nsub=nsub),
        grid=(bsz, t // tm),
        in_specs=[tile(d), pl.BlockSpec((1, ATTN_Q, tm), lambda b, i: (b, 0, i)), tile(GLA_V),
                  mod, mod, mod, mod,
                  const(wo.shape), const(w1.shape), const(w2.shape), const(fg.shape)],
        out_specs=tile(d),
        out_shape=jax.ShapeDtypeStruct((bsz, t, d), F32),
        scratch_shapes=[pltpu.VMEM((nsub, tm // nsub, D_FF), BF16)],
        compiler_params=pltpu.CompilerParams(
            dimension_semantics=("arbitrary", "arbitrary"), vmem_limit_bytes=VMEM_LIMIT),
        name="mlp",
    )(x, attn, gla, gate1, shift2, scale2, gate2, wo, w1, w2, fg)


def kernel(x, c, w_ada, b_ada, w_in, w_gate_up, b_gate, gla_norm_g, attn_sinks, rel_bias,
           w_out, w_mlp_in, w_mlp_out, final_norm_g):
    bsz, t, d = x.shape
    depth = w_ada.shape[0]
    tm = min(TILE_TOKENS, t)
    nsub = max(1, tm // SUB_TILE_TOKENS)
    p_kv = ATTN_Q
    p_g = ATTN_Q + 2 * ATTN_KV
    p_z = p_g + GLA_PACK
    head_cols = np.concatenate([np.arange(h * ATTN_HEAD_DIM, (h + 1) * ATTN_HEAD_DIM)
                                for h in ATTN_HEAD_ORDER])
    for l in range(depth):
        ada = _ada(c, w_ada[l], b_ada[l])
        shift1, scale1, gate1, shift2, scale2, gate2 = [
            ada[:, None, i * d:(i + 1) * d] for i in range(N_MOD)]
        bias, sinkrow = _bias_table(rel_bias, attn_sinks[l])
        w = w_in[l]
        win = jnp.concatenate(
            [w[:, :p_kv][:, head_cols], w[:, p_kv:p_z],
             jnp.pad(w[:, p_z:], ((0, 0), (0, Z_PAD - GLA_GATE_RANK)))], axis=1).astype(BF16)
        wgu = jnp.pad(w_gate_up[l], ((0, Z_PAD - GLA_GATE_RANK), (0, 0))).astype(BF16)
        qa, kva, gpack, la = _inproj(x, scale1, shift1, win, wgu, b_gate[l].reshape(1, GLA_K),
                                     jnp.tile(gla_norm_g[l], GLA_HEADS).reshape(1, GLA_V), tm, nsub)
        attn, gla = _mixer(qa, kva, bias, sinkrow, gpack, la, tm)
        x = _mlp(x, attn, gla, gate1, shift2, scale2, gate2,
                 w_out[l].astype(BF16), w_mlp_in[l].astype(BF16), w_mlp_out[l].astype(BF16),
                 final_norm_g.reshape(1, d), tm, final=(l == depth - 1), nsub=nsub)
    return x
```

```python
import functools

import numpy as np
import jax
import jax.numpy as jnp
from jax import lax
from jax.experimental import pallas as pl
from jax.experimental.pallas import tpu as pltpu

F32 = jnp.float32
BF16 = jnp.bfloat16

D_MODEL = 1024
ATTN_HEADS = 8
ATTN_KV_HEADS = 2
ATTN_GROUP = ATTN_HEADS // ATTN_KV_HEADS
ATTN_HEAD_DIM = 64
WINDOW = 128
ATTN_BLOCK = 128
NUM_BUCKETS = 32
MAX_DISTANCE = 128
GLA_HEADS = 4
GLA_PAIRS = GLA_HEADS // 2
GLA_DK = 64
GLA_DV = 128
GLA_GATE_RANK = 16
GLA_GATE_NORM = 16.0
GLA_CHUNK = 128
D_FF = 4 * D_MODEL
EPS = 1e-6
N_MOD = 6

ATTN_Q = ATTN_HEADS * ATTN_HEAD_DIM
ATTN_KV = ATTN_KV_HEADS * ATTN_HEAD_DIM
GLA_K = GLA_HEADS * GLA_DK
GLA_V = GLA_HEADS * GLA_DV
GLA_PACK = 2 * GLA_K + 2 * GLA_V
LANES = 128
HALF = LANES // 2
Z_PAD = LANES
LOG2E = 1.4426950408889634

VMEM_LIMIT = 56 * 1024 * 1024
INPROJ_TILE, INPROJ_SUB_TILE = 2048, 256
MIXER_TILE = 2048
MLP_TILE, MLP_SUB_TILE = 1024, 512

NT_DIMS = (((1,), (1,)), ((), ()))
TN_DIMS = (((0,), (0,)), ((), ()))

ATTN_HEAD_ORDER = tuple(h for p in range(ATTN_GROUP) for h in (p, p + ATTN_GROUP))


def _dot(a, b):
    return jnp.dot(a, b, preferred_element_type=F32)


def _rms(x):
    return x * lax.rsqrt(jnp.mean(x * x, axis=-1, keepdims=True) + EPS)


def _ada_kernel(c_ref, w_ref, b_ref, o_ref):
    c = c_ref[...]
    cond = c * jax.nn.sigmoid(c)
    o_ref[...] = jnp.dot(cond, w_ref[...], preferred_element_type=F32,
                         precision=lax.Precision.HIGHEST) + b_ref[...]


def _ada(c, w_ada, b_ada):
    bsz, d = c.shape
    n = w_ada.shape[1]
    tn = 1536
    return pl.pallas_call(
        _ada_kernel,
        grid=(n // tn,),
        in_specs=[pl.BlockSpec((bsz, d), lambda j: (0, 0)),
                  pl.BlockSpec((d, tn), lambda j: (0, j)),
                  pl.BlockSpec((1, tn), lambda j: (0, j))],
        out_specs=pl.BlockSpec((bsz, tn), lambda j: (0, j)),
        out_shape=jax.ShapeDtypeStruct((bsz, n), F32),
        name="ada",
    )(c, w_ada, b_ada.reshape(1, n))


def _t5_causal_bucket(dist):
    max_exact = NUM_BUCKETS // 2
    d = np.maximum(dist, 0)
    large = max_exact + (np.log(np.maximum(d, max_exact) / max_exact)
                         / np.log(MAX_DISTANCE / max_exact) * (NUM_BUCKETS - max_exact)).astype(np.int32)
    large = np.minimum(large, NUM_BUCKETS - 1)
    return np.where(d < max_exact, d, large).astype(np.int32)


def _bias_kernel(bucket_ref, rb_ref, sink_ref, bias_ref, sinkrow_ref):
    L = ATTN_BLOCK
    bucket = bucket_ref[...]
    key = lax.broadcasted_iota(jnp.int32, (L, L), 0)
    qry = lax.broadcasted_iota(jnp.int32, (L, L), 1)
    prev_key = key > qry
    for g in range(ATTN_KV_HEADS):
        for r in range(ATTN_GROUP):
            h = g * ATTN_GROUP + r
            cols = slice(r * L, (r + 1) * L)
            acc = jnp.zeros((L, L), F32)
            for b in range(NUM_BUCKETS):
                acc = jnp.where(bucket == b, rb_ref[b, h], acc)
            acc = acc * LOG2E
            cur = jnp.where(prev_key, -jnp.inf, acc)
            bias_ref[0, g, :L, cols] = jnp.full((L, L), -jnp.inf, F32)
            bias_ref[1, g, :L, cols] = jnp.where(prev_key, acc, -jnp.inf)
            bias_ref[0, g, L:, cols] = cur
            bias_ref[1, g, L:, cols] = cur
            sinkrow_ref[g, :, cols] = jnp.full((1, L), sink_ref[h] * LOG2E, F32)


def _bias_table(rel_bias, sinks):
    L = ATTN_BLOCK
    c = np.arange(L)[:, None]
    i = np.arange(L)[None, :]
    bucket = _t5_causal_bucket((i - c) % L)
    return pl.pallas_call(
        _bias_kernel,
        in_specs=[pl.BlockSpec(memory_space=pltpu.VMEM),
                  pl.BlockSpec(memory_space=pltpu.SMEM),
                  pl.BlockSpec(memory_space=pltpu.SMEM)],
        out_specs=[pl.BlockSpec(memory_space=pltpu.VMEM), pl.BlockSpec(memory_space=pltpu.VMEM)],
        out_shape=[jax.ShapeDtypeStruct((2, ATTN_KV_HEADS, 2 * L, ATTN_GROUP * L), F32),
                   jax.ShapeDtypeStruct((ATTN_KV_HEADS, 1, ATTN_GROUP * L), F32)],
        name="bias_table",
    )(jnp.asarray(bucket), rel_bias.astype(F32), sinks.astype(F32))


def _inproj_kernel(x_ref, sc_ref, sh_ref, w_ref, wgu_ref, bg_ref, gn_ref,
                   qa_ref, kva_ref, g_ref, la_ref, *, nsub):
    rs = x_ref.shape[1] // nsub
    p_kv = ATTN_Q
    p_g = p_kv + 2 * ATTN_KV
    p_go = p_g + 2 * GLA_K + GLA_V
    p_z = p_g + GLA_PACK

    def norm(s):
        x = x_ref[0, s * rs:(s + 1) * rs, :]
        return (_rms(x) * (1.0 + sc_ref[0]) + sh_ref[0]).astype(BF16)

    def finish(s, y):
        rows = slice(s * rs, (s + 1) * rs)
        gp = _dot(y[:, p_z:].astype(BF16), wgu_ref[...]) + bg_ref[...]
        log_sig = jnp.minimum(gp, 0.0) - jnp.log1p(jnp.exp(-jnp.abs(gp)))
        la = log_sig * (LOG2E / GLA_GATE_NORM)
        la_hi = la.astype(BF16)
        la_ref[0, rows, :GLA_K] = la_hi
        la_ref[0, rows, GLA_K:] = (la - la_hi.astype(F32)).astype(BF16)
        go = y[:, p_go:p_z]
        g_ref[0, rows, 2 * GLA_K + GLA_V:] = (go * jax.nn.sigmoid(go) * gn_ref[...]).astype(BF16)
        g_ref[0, rows, :GLA_K] = (y[:, p_g:p_g + GLA_K] * (GLA_DK ** -0.5)).astype(BF16)
        g_ref[0, rows, GLA_K:2 * GLA_K + GLA_V] = y[:, p_g + GLA_K:p_go].astype(BF16)
        qa_ref[0, rows, :] = (y[:, :p_kv] * (ATTN_HEAD_DIM ** -0.5 * LOG2E)).astype(BF16)
        kva_ref[0, rows, :] = y[:, p_kv:p_g].astype(BF16)

    hb = norm(0)
    y_prev = None
    for s in range(nsub):
        y = _dot(hb, w_ref[...])
        if s + 1 < nsub:
            hb = norm(s + 1)
        if y_prev is not None:
            finish(s - 1, y_prev)
        y_prev = y
    finish(nsub - 1, y_prev)


def _inproj(x, scale1, shift1, w, wgu, bg, gn, tm, nsub):
    bsz, t, d = x.shape
    const = lambda shape: pl.BlockSpec(shape, lambda b, i: (0,) * len(shape),
                                       pipeline_mode=pl.Buffered(1))
    tile = lambda n: pl.BlockSpec((1, tm, n), lambda b, i: (b, i, 0))
    mod = pl.BlockSpec((1, 1, d), lambda b, i: (b, 0, 0))
    return pl.pallas_call(
        functools.partial(_inproj_kernel, nsub=nsub),
        grid=(bsz, t // tm),
        in_specs=[tile(d), mod, mod, const(w.shape), const(wgu.shape), const(bg.shape), const(gn.shape)],
        out_specs=[tile(ATTN_Q), tile(2 * ATTN_KV), tile(GLA_PACK), tile(2 * GLA_K)],
        out_shape=[jax.ShapeDtypeStruct((bsz, t, ATTN_Q), BF16),
                   jax.ShapeDtypeStruct((bsz, t, 2 * ATTN_KV), BF16),
                   jax.ShapeDtypeStruct((bsz, t, GLA_PACK), BF16),
                   jax.ShapeDtypeStruct((bsz, t, 2 * GLA_K), BF16)],
        compiler_params=pltpu.CompilerParams(
            dimension_semantics=("arbitrary", "arbitrary"), vmem_limit_bytes=VMEM_LIMIT),
        name="inproj",
    )(x, scale1, shift1, w, wgu, bg, gn)


def _swa_scores(n, g, q_ref, kv_ref, kvp_ref, bias_ref):
    L = ATTN_BLOCK
    G = ATTN_GROUP
    low = lax.broadcasted_iota(jnp.int32, (L, LANES), 1) < HALF
    zero = jnp.zeros((), BF16)
    rows = slice(n * L, (n + 1) * L)
    k_prev = kvp_ref[0, :, :ATTN_KV] if n == 0 else kv_ref[0, (n - 1) * L:n * L, :ATTN_KV]
    k2 = jnp.concatenate([k_prev, kv_ref[0, rows, :ATTN_KV]], axis=0)
    sel = jnp.where(pl.program_id(1) == 0, 0, 1) if n == 0 else 1
    keep = low if g == 0 else jnp.logical_not(low)
    qs = jnp.concatenate(
        [jnp.where(keep, q_ref[0, rows, p * LANES:(p + 1) * LANES], zero) for p in range(G)],
        axis=0)
    s = lax.dot_general(k2, qs, NT_DIMS, preferred_element_type=F32) + bias_ref[sel, g]
    return jnp.maximum(s[:L], s[L:])


def _swa_softmax(g, sc, sink_ref):
    L = ATTN_BLOCK
    G = ATTN_GROUP
    key = lax.broadcasted_iota(jnp.int32, (L, G * L), 0)
    qry = lax.broadcasted_iota(jnp.int32, (L, G * L), 1) & (L - 1)
    prev_key = key > qry
    sink = sink_ref[g]
    m = jnp.maximum(jnp.max(sc, axis=0, keepdims=True), sink)
    p_ = jnp.exp2(sc - m)
    denom = jnp.sum(p_, axis=0, keepdims=True) + jnp.exp2(sink - m)
    pb = p_.astype(BF16)
    zero = jnp.zeros((), BF16)
    p2 = jnp.concatenate([jnp.where(prev_key, pb, zero), jnp.where(prev_key, zero, pb)], axis=0)
    return p2, denom


def _swa_pv(n, g, p2, denom, kv_ref, kvp_ref, o_ref):
    L = ATTN_BLOCK
    G = ATTN_GROUP
    dh = ATTN_HEAD_DIM
    rows = slice(n * L, (n + 1) * L)
    v_prev = kvp_ref[0, :, ATTN_KV:] if n == 0 else kv_ref[0, (n - 1) * L:n * L, ATTN_KV:]
    v2 = jnp.concatenate([v_prev, kv_ref[0, rows, ATTN_KV:]], axis=0)
    o = lax.dot_general(v2, p2, TN_DIMS, preferred_element_type=F32)
    o = (o[g * dh:(g + 1) * dh] * (1.0 / denom)).astype(BF16)
    for r in range(G):
        h = g * G + r
        o_ref[0, h * dh:(h + 1) * dh, rows] = o[:, r * L:(r + 1) * L]


def _gla_prep(c, g_ref, la_ref):
    C = GLA_CHUNK
    ri = lax.broadcasted_iota(jnp.int32, (C, C), 0)
    ci = lax.broadcasted_iota(jnp.int32, (C, C), 1)
    tril = (ri >= ci).astype(BF16)
    rows = slice(c * C, (c + 1) * C)
    b = _dot(tril, la_ref[0, rows, :GLA_K]) + _dot(tril, la_ref[0, rows, GLA_K:])
    b_mid = b[C // 2 - 1:C // 2, :]
    b_last = b[C - 1:C, :]
    q = g_ref[0, rows, 0:GLA_K].astype(F32)
    k = g_ref[0, rows, GLA_K:2 * GLA_K].astype(F32)
    q_t = q * jnp.exp2(b - b_mid)
    k_t = k * jnp.exp2(b_mid - b)
    q_in = (q_t * jnp.exp2(b_mid)).astype(BF16)
    k_s = (k_t * jnp.exp2(b_last - b_mid)).astype(BF16)
    return q_in, q_t.astype(BF16), k_t.astype(BF16), k_s, jnp.exp2(b_last)


def _gla_intra(prep):
    C = GLA_CHUNK
    dk = GLA_DK
    _, q_t, k_t, _, _ = prep
    ri2 = lax.broadcasted_iota(jnp.int32, (C, 2 * C), 0)
    ci2 = lax.broadcasted_iota(jnp.int32, (C, 2 * C), 1) & (C - 1)
    causal2 = ri2 >= ci2
    low_k = lax.broadcasted_iota(jnp.int32, (C, 2 * dk), 1) < dk
    zero = jnp.zeros((), BF16)
    out = []
    for p in range(GLA_PAIRS):
        ks = slice(p * 2 * dk, (p + 1) * 2 * dk)
        kt_p = k_t[:, ks]
        kbd = jnp.concatenate([jnp.where(low_k, kt_p, zero), jnp.where(low_k, zero, kt_p)], axis=0)
        a = lax.dot_general(q_t[:, ks], kbd, NT_DIMS, preferred_element_type=F32)
        out.append(jnp.where(causal2, a.astype(BF16), zero))
    return out


def _gla_out(c, prep, a, st, g_ref, o_ref):
    C = GLA_CHUNK
    dk, dv = GLA_DK, GLA_DV
    q_in, _, _, k_s, decay = prep
    low_k = lax.broadcasted_iota(jnp.int32, (C, 2 * dk), 1) < dk
    zero = jnp.zeros((), BF16)
    zeros_v = jnp.zeros((C, dv), BF16)
    rows = slice(c * C, (c + 1) * C)
    st_new = []
    for p in range(GLA_PAIRS):
        ks = slice(p * 2 * dk, (p + 1) * 2 * dk)
        vs = slice(2 * GLA_K + p * 2 * dv, 2 * GLA_K + (p + 1) * 2 * dv)
        gs = slice(2 * GLA_K + GLA_V + p * 2 * dv, 2 * GLA_K + GLA_V + (p + 1) * 2 * dv)
        v_0 = g_ref[0, rows, vs.start:vs.start + dv]
        v_1 = g_ref[0, rows, vs.start + dv:vs.stop]
        vbd = jnp.concatenate([jnp.concatenate([v_0, zeros_v], axis=1),
                               jnp.concatenate([zeros_v, v_1], axis=1)], axis=0)
        o_p = _dot(a[p], vbd) + lax.dot_general(q_in[:, ks], st[p].astype(BF16), NT_DIMS,
                                                preferred_element_type=F32)
        ks_p = k_s[:, ks]
        dst = jnp.concatenate(
            [lax.dot_general(v_0, jnp.where(low_k, ks_p, zero), TN_DIMS, preferred_element_type=F32),
             lax.dot_general(v_1, jnp.where(low_k, zero, ks_p), TN_DIMS, preferred_element_type=F32)],
            axis=0)
        st_new.append(st[p] * decay[:, ks] + dst)
        o_n = jnp.concatenate([_rms(o_p[:, :dv]), _rms(o_p[:, dv:])], axis=1)
        o_ref[0, rows, p * 2 * dv:(p + 1) * 2 * dv] = (o_n * g_ref[0, rows, gs].astype(F32)).astype(BF16)
    return st_new


def _mixer_kernel(q_ref, kv_ref, kvp_ref, bias_ref, sink_ref, g_ref, la_ref,
                  attn_ref, gla_ref, st_ref, *, nblk):
    @pl.when(pl.program_id(1) == 0)
    def _():
        st_ref[...] = jnp.zeros_like(st_ref)

    st = [st_ref[p] for p in range(GLA_PAIRS)]
    units = [(n, g) for n in range(nblk) for g in range(ATTN_KV_HEADS)]
    nu = len(units)
    sc, pd, prep, intra = {}, {}, {}, {}

    def emit(kind, i):
        nonlocal st
        if kind == "prep" and 0 <= i < nblk:
            prep[i] = _gla_prep(i, g_ref, la_ref)
        elif kind == "intra" and 0 <= i < nblk:
            intra[i] = _gla_intra(prep[i])
        elif kind == "out" and 0 <= i < nblk:
            st = _gla_out(i, prep.pop(i), intra.pop(i), st, g_ref, gla_ref)
        elif kind == "scores" and 0 <= i < nu:
            sc[i] = _swa_scores(*units[i], q_ref, kv_ref, kvp_ref, bias_ref)
        elif kind == "softmax" and 0 <= i < nu:
            pd[i] = _swa_softmax(units[i][1], sc.pop(i), sink_ref)
        elif kind == "pv" and 0 <= i < nu:
            _swa_pv(*units[i], *pd.pop(i), kv_ref, kvp_ref, attn_ref)

    for slot in range(nu + 2 * ATTN_KV_HEADS):
        if slot % ATTN_KV_HEADS == 0:
            c = slot // ATTN_KV_HEADS
            emit("out", c - 2)
            emit("intra", c - 1)
            emit("prep", c)
        emit("pv", slot - 2)
        emit("softmax", slot - 1)
        emit("scores", slot)
    for p in range(GLA_PAIRS):
        st_ref[p] = st[p]


def _mixer(qa, kva, bias, sinkrow, gpack, la, tq):
    assert ATTN_BLOCK == GLA_CHUNK
    bsz, t, _ = qa.shape
    nblk = tq // ATTN_BLOCK
    whole = lambda a: pl.BlockSpec(a.shape, lambda b, i: (0,) * a.ndim, pipeline_mode=pl.Buffered(1))
    tile = lambda n: pl.BlockSpec((1, tq, n), lambda b, i: (b, i, 0))
    return pl.pallas_call(
        functools.partial(_mixer_kernel, nblk=nblk),
        grid=(bsz, t // tq),
        in_specs=[tile(ATTN_Q), tile(2 * ATTN_KV),
                  pl.BlockSpec((1, ATTN_BLOCK, 2 * ATTN_KV),
                               lambda b, i: (b, jnp.maximum(i * nblk - 1, 0), 0)),
                  whole(bias), whole(sinkrow), tile(GLA_PACK), tile(2 * GLA_K)],
        out_specs=[pl.BlockSpec((1, ATTN_Q, tq), lambda b, i: (b, 0, i)), tile(GLA_V)],
        out_shape=[jax.ShapeDtypeStruct((bsz, ATTN_Q, t), BF16),
                   jax.ShapeDtypeStruct((bsz, t, GLA_V), BF16)],
        scratch_shapes=[pltpu.VMEM((GLA_PAIRS, 2 * GLA_DV, 2 * GLA_DK), F32)],
        compiler_params=pltpu.CompilerParams(
            dimension_semantics=("arbitrary", "arbitrary"), vmem_limit_bytes=VMEM_LIMIT),
        name="mixer",
    )(qa, kva, kva, bias, sinkrow, gpack, la)


def _mlp_kernel(x_ref, attn_ref, gla_ref, g1_ref, sh2_ref, sc2_ref, g2_ref, wo_ref, w1_ref, w2_ref,
                fg_ref, o_ref, u_ref, *, fchunk, final, nsub):
    rs = x_ref.shape[1] // nsub
    nf = D_FF // fchunk

    def pre(s):
        rows = slice(s * rs, (s + 1) * rs)
        mix = (lax.dot_general(attn_ref[0, :, rows], wo_ref[:ATTN_Q, :], TN_DIMS, preferred_element_type=F32)
               + _dot(gla_ref[0, rows, :], wo_ref[ATTN_Q:, :]))
        x1 = x_ref[0, rows, :] + g1_ref[0] * mix
        return x1, (_rms(x1) * (1.0 + sc2_ref[0]) + sh2_ref[0]).astype(BF16)

    def up(s, hb, c):
        cols = slice(c * fchunk, (c + 1) * fchunk)
        u = jnp.maximum(_dot(hb, w1_ref[:, cols]), 0.0)
        u_ref[s, :, cols] = (u * u).astype(BF16)

    def post(s, x1, y):
        x2 = x1 + g2_ref[0] * y
        o_ref[0, s * rs:(s + 1) * rs, :] = _rms(x2) * fg_ref[...] if final else x2

    x1, hb = pre(0)
    pending = None
    for s in range(nsub):
        for c in range(nf // 2):
            up(s, hb, c)
        if s + 1 < nsub:
            nxt = pre(s + 1)
        for c in range(nf // 2, nf):
            up(s, hb, c)
        if pending is not None:
            post(*pending)
        pending = (s, x1, _dot(u_ref[s], w2_ref[...]))
        if s + 1 < nsub:
            x1, hb = nxt
    post(*pending)


def _mlp(x, attn, gla, gate1, shift2, scale2, gate2, wo, w1, w2, fg, tm, final, nsub):
    bsz, t, d = x.shape
    const = lambda shape: pl.BlockSpec(shape, lambda b, i: (0,) * len(shape),
                                       pipeline_mode=pl.Buffered(1))
    tile = lambda n: pl.BlockSpec((1, tm, n), lambda b, i: (b, i, 0))
    mod = pl.BlockSpec((1, 1, d), lambda b, i: (b, 0, 0))
    return pl.pallas_call(
        functools.partial(_mlp_kernel, fchunk=1024, final=final, nsub=nsub),
        grid=(bsz, t // tm),
        in_specs=[tile(d), pl.BlockSpec((1, ATTN_Q, tm), lambda b, i: (b, 0, i)), tile(GLA_V),
                  mod, mod, mod, mod,
                  const(wo.shape), const(w1.shape), const(w2.shape), const(fg.shape)],
        out_specs=tile(d),
        out_shape=jax.ShapeDtypeStruct((bsz, t, d), F32),
        scratch_shapes=[pltpu.VMEM((nsub, tm // nsub, D_FF), BF16)],
        compiler_params=pltpu.CompilerParams(
            dimension_semantics=("arbitrary", "arbitrary"), vmem_limit_bytes=VMEM_LIMIT),
        name="mlp",
    )(x, attn, gla, gate1, shift2, scale2, gate2, wo, w1, w2, fg)


def kernel(x, c, w_ada, b_ada, w_in, w_gate_up, b_gate, gla_norm_g, attn_sinks, rel_bias,
           w_out, w_mlp_in, w_mlp_out, final_norm_g):
    bsz, t, d = x.shape
    depth = w_ada.shape[0]
    tile_in, tile_mix, tile_mlp = (min(v, t) for v in (INPROJ_TILE, MIXER_TILE, MLP_TILE))
    p_kv = ATTN_Q
    p_g = ATTN_Q + 2 * ATTN_KV
    p_z = p_g + GLA_PACK
    head_cols = np.concatenate([np.arange(h * ATTN_HEAD_DIM, (h + 1) * ATTN_HEAD_DIM)
                                for h in ATTN_HEAD_ORDER])
    for l in range(depth):
        ada = _ada(c, w_ada[l], b_ada[l])
        shift1, scale1, gate1, shift2, scale2, gate2 = [
            ada[:, None, i * d:(i + 1) * d] for i in range(N_MOD)]
        bias, sinkrow = _bias_table(rel_bias, attn_sinks[l])
        w = w_in[l]
        win = jnp.concatenate(
            [w[:, :p_kv][:, head_cols], w[:, p_kv:p_z],
             jnp.pad(w[:, p_z:], ((0, 0), (0, Z_PAD - GLA_GATE_RANK)))], axis=1).astype(BF16)
        wgu = jnp.pad(w_gate_up[l], ((0, Z_PAD - GLA_GATE_RANK), (0, 0))).astype(BF16)
        qa, kva, gpack, la = _inproj(x, scale1, shift1, win, wgu, b_gate[l].reshape(1, GLA_K),
                                     jnp.tile(gla_norm_g[l], GLA_HEADS).reshape(1, GLA_V),
                                     tile_in, max(1, tile_in // INPROJ_SUB_TILE))
        attn, gla = _mixer(qa, kva, bias, sinkrow, gpack, la, tile_mix)
        x = _mlp(x, attn, gla, gate1, shift2, scale2, gate2,
                 w_out[l].astype(BF16), w_mlp_in[l].astype(BF16), w_mlp_out[l].astype(BF16),
                 final_norm_g.reshape(1, d), tile_mlp, final=(l == depth - 1),
                 nsub=max(1, tile_mlp // MLP_SUB_TILE))
    return x
```

```python
import functools

import numpy as np
import jax
import jax.numpy as jnp
from jax import lax
from jax.experimental import pallas as pl
from jax.experimental.pallas import tpu as pltpu

F32 = jnp.float32
BF16 = jnp.bfloat16

D_MODEL = 1024
ATTN_HEADS = 8
ATTN_KV_HEADS = 2
ATTN_GROUP = ATTN_HEADS // ATTN_KV_HEADS
ATTN_HEAD_DIM = 64
WINDOW = 128
ATTN_BLOCK = 128
NUM_BUCKETS = 32
MAX_DISTANCE = 128
GLA_HEADS = 4
GLA_PAIRS = GLA_HEADS // 2
GLA_DK = 64
GLA_DV = 128
GLA_GATE_RANK = 16
GLA_GATE_NORM = 16.0
GLA_CHUNK = 128
D_FF = 4 * D_MODEL
EPS = 1e-6
N_MOD = 6

ATTN_Q = ATTN_HEADS * ATTN_HEAD_DIM
ATTN_KV = ATTN_KV_HEADS * ATTN_HEAD_DIM
GLA_K = GLA_HEADS * GLA_DK
GLA_V = GLA_HEADS * GLA_DV
GLA_PACK = 2 * GLA_K + 2 * GLA_V
LANES = 128
HALF = LANES // 2
Z_PAD = LANES
LOG2E = 1.4426950408889634

VMEM_LIMIT = 56 * 1024 * 1024
INPROJ_TILE, INPROJ_SUB_TILE = 1024, 512
MIXER_TILE = 2048
MLP_TILE, MLP_SUB_TILE = 1024, 512

NT_DIMS = (((1,), (1,)), ((), ()))
TN_DIMS = (((0,), (0,)), ((), ()))

MOD_SHIFT1, MOD_SCALE1, MOD_GATE1, MOD_SHIFT2, MOD_SCALE2, MOD_GATE2 = range(N_MOD)


def _dot(a, b):
    return jnp.dot(a, b, preferred_element_type=F32)


def _pair_heads(wq):
    d = wq.shape[0]
    return wq.reshape(d, ATTN_KV_HEADS, ATTN_GROUP, ATTN_HEAD_DIM).transpose(0, 2, 1, 3).reshape(d, ATTN_Q)


def _mod_spec(d):
    return pl.BlockSpec((1, N_MOD, 1, d), lambda b, i: (b, 0, 0, 0))


def _rms(x):
    return x * lax.rsqrt(jnp.mean(x * x, axis=-1, keepdims=True) + EPS)


def _ada_kernel(c_ref, w_ref, b_ref, o_ref):
    c = c_ref[...]
    cond = c * jax.nn.sigmoid(c)
    o_ref[...] = jnp.dot(cond, w_ref[...], preferred_element_type=F32,
                         precision=lax.Precision.HIGHEST) + b_ref[...]


def _ada(c, w_ada, b_ada):
    bsz, d = c.shape
    n = w_ada.shape[1]
    tn = 1536
    return pl.pallas_call(
        _ada_kernel,
        grid=(n // tn,),
        in_specs=[pl.BlockSpec((bsz, d), lambda j: (0, 0)),
                  pl.BlockSpec((d, tn), lambda j: (0, j)),
                  pl.BlockSpec((1, tn), lambda j: (0, j))],
        out_specs=pl.BlockSpec((bsz, tn), lambda j: (0, j)),
        out_shape=jax.ShapeDtypeStruct((bsz, n), F32),
        name="ada",
    )(c, w_ada, b_ada.reshape(1, n))


def _t5_causal_bucket(dist):
    max_exact = NUM_BUCKETS // 2
    d = np.maximum(dist, 0)
    large = max_exact + (np.log(np.maximum(d, max_exact) / max_exact)
                         / np.log(MAX_DISTANCE / max_exact) * (NUM_BUCKETS - max_exact)).astype(np.int32)
    large = np.minimum(large, NUM_BUCKETS - 1)
    return np.where(d < max_exact, d, large).astype(np.int32)


def _bias_kernel(bucket_ref, rb_ref, sink_ref, bias_ref, sinkrow_ref):
    L = ATTN_BLOCK
    bucket = bucket_ref[...]
    key = lax.broadcasted_iota(jnp.int32, (L, L), 0)
    qry = lax.broadcasted_iota(jnp.int32, (L, L), 1)
    prev_key = key > qry
    for g in range(ATTN_KV_HEADS):
        for r in range(ATTN_GROUP):
            h = g * ATTN_GROUP + r
            cols = slice(r * L, (r + 1) * L)
            acc = jnp.zeros((L, L), F32)
            for b in range(NUM_BUCKETS):
                acc = jnp.where(bucket == b, rb_ref[b, h], acc)
            acc = acc * LOG2E
            cur = jnp.where(prev_key, -jnp.inf, acc)
            bias_ref[0, g, :L, cols] = jnp.full((L, L), -jnp.inf, F32)
            bias_ref[1, g, :L, cols] = jnp.where(prev_key, acc, -jnp.inf)
            bias_ref[0, g, L:, cols] = cur
            bias_ref[1, g, L:, cols] = cur
            sinkrow_ref[g, :, cols] = jnp.full((1, L), sink_ref[h] * LOG2E, F32)


def _bias_table(rel_bias, sinks):
    L = ATTN_BLOCK
    c = np.arange(L)[:, None]
    i = np.arange(L)[None, :]
    bucket = _t5_causal_bucket((i - c) % L)
    return pl.pallas_call(
        _bias_kernel,
        in_specs=[pl.BlockSpec(memory_space=pltpu.VMEM),
                  pl.BlockSpec(memory_space=pltpu.SMEM),
                  pl.BlockSpec(memory_space=pltpu.SMEM)],
        out_specs=[pl.BlockSpec(memory_space=pltpu.VMEM), pl.BlockSpec(memory_space=pltpu.VMEM)],
        out_shape=[jax.ShapeDtypeStruct((2, ATTN_KV_HEADS, 2 * L, ATTN_GROUP * L), F32),
                   jax.ShapeDtypeStruct((ATTN_KV_HEADS, 1, ATTN_GROUP * L), F32)],
        name="bias_table",
    )(jnp.asarray(bucket), rel_bias.astype(F32), sinks.astype(F32))


def _inproj_kernel(x_ref, mod_ref, w_ref, wgu_ref, bg_ref, gn_ref,
                   qa_ref, kva_ref, g_ref, la_ref, *, nsub):
    rs = x_ref.shape[1] // nsub
    p_kv = ATTN_Q
    p_g = p_kv + 2 * ATTN_KV
    p_go = p_g + 2 * GLA_K + GLA_V
    p_z = p_g + GLA_PACK

    def norm(s):
        x = x_ref[0, s * rs:(s + 1) * rs, :]
        return (_rms(x) * (1.0 + mod_ref[0, MOD_SCALE1]) + mod_ref[0, MOD_SHIFT1]).astype(BF16)

    def finish(s, y):
        rows = slice(s * rs, (s + 1) * rs)
        gp = _dot(y[:, p_z:].astype(BF16), wgu_ref[...]) + bg_ref[...]
        log_sig = jnp.minimum(gp, 0.0) - jnp.log1p(jnp.exp(-jnp.abs(gp)))
        la = log_sig * (LOG2E / GLA_GATE_NORM)
        la_hi = la.astype(BF16)
        la_ref[0, rows, :GLA_K] = la_hi
        la_ref[0, rows, GLA_K:] = (la - la_hi.astype(F32)).astype(BF16)
        go = y[:, p_go:p_z]
        g_ref[0, rows, 2 * GLA_K + GLA_V:] = (go * jax.nn.sigmoid(go) * gn_ref[...]).astype(BF16)
        g_ref[0, rows, :GLA_K] = (y[:, p_g:p_g + GLA_K] * (GLA_DK ** -0.5)).astype(BF16)
        g_ref[0, rows, GLA_K:2 * GLA_K + GLA_V] = y[:, p_g + GLA_K:p_go].astype(BF16)
        qa_ref[0, rows, :] = (y[:, :p_kv] * (ATTN_HEAD_DIM ** -0.5 * LOG2E)).astype(BF16)
        kva_ref[0, rows, :] = y[:, p_kv:p_g].astype(BF16)

    hb = norm(0)
    y_prev = None
    for s in range(nsub):
        y = _dot(hb, w_ref[...])
        if s + 1 < nsub:
            hb = norm(s + 1)
        if y_prev is not None:
            finish(s - 1, y_prev)
        y_prev = y
    finish(nsub - 1, y_prev)


def _inproj(x, mod, w, wgu, bg, gn, tm, nsub):
    bsz, t, d = x.shape
    const = lambda shape: pl.BlockSpec(shape, lambda b, i: (0,) * len(shape),
                                       pipeline_mode=pl.Buffered(1))
    tile = lambda n: pl.BlockSpec((1, tm, n), lambda b, i: (b, i, 0))
    return pl.pallas_call(
        functools.partial(_inproj_kernel, nsub=nsub),
        grid=(bsz, t // tm),
        in_specs=[tile(d), _mod_spec(d), const(w.shape), const(wgu.shape), const(bg.shape), const(gn.shape)],
        out_specs=[tile(ATTN_Q), tile(2 * ATTN_KV), tile(GLA_PACK), tile(2 * GLA_K)],
        out_shape=[jax.ShapeDtypeStruct((bsz, t, ATTN_Q), BF16),
                   jax.ShapeDtypeStruct((bsz, t, 2 * ATTN_KV), BF16),
                   jax.ShapeDtypeStruct((bsz, t, GLA_PACK), BF16),
                   jax.ShapeDtypeStruct((bsz, t, 2 * GLA_K), BF16)],
        compiler_params=pltpu.CompilerParams(
            dimension_semantics=("arbitrary", "arbitrary"), vmem_limit_bytes=VMEM_LIMIT),
        name="inproj",
    )(x, mod, w, wgu, bg, gn)


def _swa_scores(n, g, q_ref, kv_ref, kvp_ref, bias_ref):
    L = ATTN_BLOCK
    G = ATTN_GROUP
    low = lax.broadcasted_iota(jnp.int32, (L, LANES), 1) < HALF
    zero = jnp.zeros((), BF16)
    rows = slice(n * L, (n + 1) * L)
    k_prev = kvp_ref[0, :, :ATTN_KV] if n == 0 else kv_ref[0, (n - 1) * L:n * L, :ATTN_KV]
    k2 = jnp.concatenate([k_prev, kv_ref[0, rows, :ATTN_KV]], axis=0)
    sel = jnp.where(pl.program_id(1) == 0, 0, 1) if n == 0 else 1
    keep = low if g == 0 else jnp.logical_not(low)
    qs = jnp.concatenate(
        [jnp.where(keep, q_ref[0, rows, p * LANES:(p + 1) * LANES], zero) for p in range(G)],
        axis=0)
    s = lax.dot_general(k2, qs, NT_DIMS, preferred_element_type=F32) + bias_ref[sel, g]
    return jnp.maximum(s[:L], s[L:])


def _swa_softmax(g, sc, sink_ref):
    L = ATTN_BLOCK
    G = ATTN_GROUP
    key = lax.broadcasted_iota(jnp.int32, (L, G * L), 0)
    qry = lax.broadcasted_iota(jnp.int32, (L, G * L), 1) & (L - 1)
    prev_key = key > qry
    sink = sink_ref[g]
    m = jnp.maximum(jnp.max(sc, axis=0, keepdims=True), sink)
    p_ = jnp.exp2(sc - m)
    denom = jnp.sum(p_, axis=0, keepdims=True) + jnp.exp2(sink - m)
    pb = p_.astype(BF16)
    zero = jnp.zeros((), BF16)
    p2 = jnp.concatenate([jnp.where(prev_key, pb, zero), jnp.where(prev_key, zero, pb)], axis=0)
    return p2, denom


def _swa_pv(n, g, p2, denom, kv_ref, kvp_ref, o_ref):
    L = ATTN_BLOCK
    G = ATTN_GROUP
    dh = ATTN_HEAD_DIM
    rows = slice(n * L, (n + 1) * L)
    v_prev = kvp_ref[0, :, ATTN_KV:] if n == 0 else kv_ref[0, (n - 1) * L:n * L, ATTN_KV:]
    v2 = jnp.concatenate([v_prev, kv_ref[0, rows, ATTN_KV:]], axis=0)
    o = lax.dot_general(v2, p2, TN_DIMS, preferred_element_type=F32)
    o = (o[g * dh:(g + 1) * dh] * (1.0 / denom)).astype(BF16)
    for r in range(G):
        h = g * G + r
        o_ref[0, h * dh:(h + 1) * dh, rows] = o[:, r * L:(r + 1) * L]


def _gla_prep(c, g_ref, la_ref):
    C = GLA_CHUNK
    ri = lax.broadcasted_iota(jnp.int32, (C, C), 0)
    ci = lax.broadcasted_iota(jnp.int32, (C, C), 1)
    tril = (ri >= ci).astype(BF16)
    rows = slice(c * C, (c + 1) * C)
    b = _dot(tril, la_ref[0, rows, :GLA_K]) + _dot(tril, la_ref[0, rows, GLA_K:])
    b_mid = b[C // 2 - 1:C // 2, :]
    b_last = b[C - 1:C, :]
    q = g_ref[0, rows, 0:GLA_K].astype(F32)
    k = g_ref[0, rows, GLA_K:2 * GLA_K].astype(F32)
    q_t = q * jnp.exp2(b - b_mid)
    k_t = k * jnp.exp2(b_mid - b)
    q_in = (q_t * jnp.exp2(b_mid)).astype(BF16)
    k_s = (k_t * jnp.exp2(b_last - b_mid)).astype(BF16)
    return q_in, q_t.astype(BF16), k_t.astype(BF16), k_s, jnp.exp2(b_last)


def _gla_intra(prep):
    C = GLA_CHUNK
    dk = GLA_DK
    _, q_t, k_t, _, _ = prep
    ri2 = lax.broadcasted_iota(jnp.int32, (C, 2 * C), 0)
    ci2 = lax.broadcasted_iota(jnp.int32, (C, 2 * C), 1) & (C - 1)
    causal2 = ri2 >= ci2
    low_k = lax.broadcasted_iota(jnp.int32, (C, 2 * dk), 1) < dk
    zero = jnp.zeros((), BF16)
    out = []
    for p in range(GLA_PAIRS):
        ks = slice(p * 2 * dk, (p + 1) * 2 * dk)
        kt_p = k_t[:, ks]
        kbd = jnp.concatenate([jnp.where(low_k, kt_p, zero), jnp.where(low_k, zero, kt_p)], axis=0)
        a = lax.dot_general(q_t[:, ks], kbd, NT_DIMS, preferred_element_type=F32)
        out.append(jnp.where(causal2, a.astype(BF16), zero))
    return out


def _gla_out(c, prep, a, st, g_ref, o_ref):
    C = GLA_CHUNK
    dk, dv = GLA_DK, GLA_DV
    q_in, _, _, k_s, decay = prep
    low_k = lax.broadcasted_iota(jnp.int32, (C, 2 * dk), 1) < dk
    zero = jnp.zeros((), BF16)
    zeros_v = jnp.zeros((C, dv), BF16)
    rows = slice(c * C, (c + 1) * C)
    st_new = []
    for p in range(GLA_PAIRS):
        ks = slice(p * 2 * dk, (p + 1) * 2 * dk)
        vs = slice(2 * GLA_K + p * 2 * dv, 2 * GLA_K + (p + 1) * 2 * dv)
        gs = slice(2 * GLA_K + GLA_V + p * 2 * dv, 2 * GLA_K + GLA_V + (p + 1) * 2 * dv)
        v_0 = g_ref[0, rows, vs.start:vs.start + dv]
        v_1 = g_ref[0, rows, vs.start + dv:vs.stop]
        vbd = jnp.concatenate([jnp.concatenate([v_0, zeros_v], axis=1),
                               jnp.concatenate([zeros_v, v_1], axis=1)], axis=0)
        o_p = _dot(a[p], vbd) + lax.dot_general(q_in[:, ks], st[p].astype(BF16), NT_DIMS,
                                                preferred_element_type=F32)
        ks_p = k_s[:, ks]
        dst = jnp.concatenate(
            [lax.dot_general(v_0, jnp.where(low_k, ks_p, zero), TN_DIMS, preferred_element_type=F32),
             lax.dot_general(v_1, jnp.where(low_k, zero, ks_p), TN_DIMS, preferred_element_type=F32)],
            axis=0)
        st_new.append(st[p] * decay[:, ks] + dst)
        o_n = jnp.concatenate([_rms(o_p[:, :dv]), _rms(o_p[:, dv:])], axis=1)
        o_ref[0, rows, p * 2 * dv:(p + 1) * 2 * dv] = (o_n * g_ref[0, rows, gs].astype(F32)).astype(BF16)
    return st_new


def _mixer_kernel(q_ref, kv_ref, kvp_ref, bias_ref, sink_ref, g_ref, la_ref,
                  attn_ref, gla_ref, st_ref, *, nblk):
    @pl.when(pl.program_id(1) == 0)
    def _():
        st_ref[...] = jnp.zeros_like(st_ref)

    st = [st_ref[p] for p in range(GLA_PAIRS)]
    units = [(n, g) for n in range(nblk) for g in range(ATTN_KV_HEADS)]
    nu = len(units)
    sc, pd, prep, intra = {}, {}, {}, {}

    def emit(kind, i):
        nonlocal st
        if kind == "prep" and 0 <= i < nblk:
            prep[i] = _gla_prep(i, g_ref, la_ref)
        elif kind == "intra" and 0 <= i < nblk:
            intra[i] = _gla_intra(prep[i])
        elif kind == "out" and 0 <= i < nblk:
            st = _gla_out(i, prep.pop(i), intra.pop(i), st, g_ref, gla_ref)
        elif kind == "scores" and 0 <= i < nu:
            sc[i] = _swa_scores(*units[i], q_ref, kv_ref, kvp_ref, bias_ref)
        elif kind == "softmax" and 0 <= i < nu:
            pd[i] = _swa_softmax(units[i][1], sc.pop(i), sink_ref)
        elif kind == "pv" and 0 <= i < nu:
            _swa_pv(*units[i], *pd.pop(i), kv_ref, kvp_ref, attn_ref)

    for slot in range(nu + 2 * ATTN_KV_HEADS):
        if slot % ATTN_KV_HEADS == 0:
            c = slot // ATTN_KV_HEADS
            emit("out", c - 2)
            emit("intra", c - 1)
            emit("prep", c)
        emit("pv", slot - 2)
        emit("softmax", slot - 1)
        emit("scores", slot)
    for p in range(GLA_PAIRS):
        st_ref[p] = st[p]


def _mixer(qa, kva, bias, sinkrow, gpack, la, tq):
    assert ATTN_BLOCK == GLA_CHUNK
    bsz, t, _ = qa.shape
    nblk = tq // ATTN_BLOCK
    whole = lambda a: pl.BlockSpec(a.shape, lambda b, i: (0,) * a.ndim, pipeline_mode=pl.Buffered(1))
    tile = lambda n: pl.BlockSpec((1, tq, n), lambda b, i: (b, i, 0))
    return pl.pallas_call(
        functools.partial(_mixer_kernel, nblk=nblk),
        grid=(bsz, t // tq),
        in_specs=[tile(ATTN_Q), tile(2 * ATTN_KV),
                  pl.BlockSpec((1, ATTN_BLOCK, 2 * ATTN_KV),
                               lambda b, i: (b, jnp.maximum(i * nblk - 1, 0), 0)),
                  whole(bias), whole(sinkrow), tile(GLA_PACK), tile(2 * GLA_K)],
        out_specs=[pl.BlockSpec((1, ATTN_Q, tq), lambda b, i: (b, 0, i)), tile(GLA_V)],
        out_shape=[jax.ShapeDtypeStruct((bsz, ATTN_Q, t), BF16),
                   jax.ShapeDtypeStruct((bsz, t, GLA_V), BF16)],
        scratch_shapes=[pltpu.VMEM((GLA_PAIRS, 2 * GLA_DV, 2 * GLA_DK), F32)],
        compiler_params=pltpu.CompilerParams(
            dimension_semantics=("arbitrary", "arbitrary"), vmem_limit_bytes=VMEM_LIMIT),
        name="mixer",
    )(qa, kva, kva, bias, sinkrow, gpack, la)


def _mlp_kernel(x_ref, attn_ref, gla_ref, mod_ref, wo_ref, w1_ref, w2_ref,
                fg_ref, o_ref, u_ref, *, fchunk, final, nsub):
    rs = x_ref.shape[1] // nsub
    nf = D_FF // fchunk

    def pre(s):
        rows = slice(s * rs, (s + 1) * rs)
        mix = (lax.dot_general(attn_ref[0, :, rows], wo_ref[:ATTN_Q, :], TN_DIMS, preferred_element_type=F32)
               + _dot(gla_ref[0, rows, :], wo_ref[ATTN_Q:, :]))
        x1 = x_ref[0, rows, :] + mod_ref[0, MOD_GATE1] * mix
        return x1, (_rms(x1) * (1.0 + mod_ref[0, MOD_SCALE2]) + mod_ref[0, MOD_SHIFT2]).astype(BF16)

    def up(s, hb, c):
        cols = slice(c * fchunk, (c + 1) * fchunk)
        u = jnp.maximum(_dot(hb, w1_ref[:, cols]), 0.0)
        u_ref[s, :, cols] = (u * u).astype(BF16)

    def post(s, x1, y):
        x2 = x1 + mod_ref[0, MOD_GATE2] * y
        o_ref[0, s * rs:(s + 1) * rs, :] = _rms(x2) * fg_ref[...] if final else x2

    x1, hb = pre(0)
    pending = None
    for s in range(nsub):
        for c in range(nf // 2):
            up(s, hb, c)
        if s + 1 < nsub:
            nxt = pre(s + 1)
        for c in range(nf // 2, nf):
            up(s, hb, c)
        if pending is not None:
            post(*pending)
        pending = (s, x1, _dot(u_ref[s], w2_ref[...]))
        if s + 1 < nsub:
            x1, hb = nxt
    post(*pending)


def _mlp(x, attn, gla, mod, wo, w1, w2, fg, tm, final, nsub):
    bsz, t, d = x.shape
    const = lambda shape: pl.BlockSpec(shape, lambda b, i: (0,) * len(shape),
                                       pipeline_mode=pl.Buffered(1))
    tile = lambda n: pl.BlockSpec((1, tm, n), lambda b, i: (b, i, 0))
    return pl.pallas_call(
        functools.partial(_mlp_kernel, fchunk=1024, final=final, nsub=nsub),
        grid=(bsz, t // tm),
        in_specs=[tile(d), pl.BlockSpec((1, ATTN_Q, tm), lambda b, i: (b, 0, i)), tile(GLA_V),
                  _mod_spec(d),
                  const(wo.shape), const(w1.shape), const(w2.shape), const(fg.shape)],
        out_specs=tile(d),
        out_shape=jax.ShapeDtypeStruct((bsz, t, d), F32),
        scratch_shapes=[pltpu.VMEM((nsub, tm // nsub, D_FF), BF16)],
        compiler_params=pltpu.CompilerParams(
            dimension_semantics=("arbitrary", "arbitrary"), vmem_limit_bytes=VMEM_LIMIT),
        name="mlp",
    )(x, attn, gla, mod, wo, w1, w2, fg)


def kernel(x, c, w_ada, b_ada, w_in, w_gate_up, b_gate, gla_norm_g, attn_sinks, rel_bias,
           w_out, w_mlp_in, w_mlp_out, final_norm_g):
    bsz, t, d = x.shape
    depth = w_ada.shape[0]
    tile_in, tile_mix, tile_mlp = (min(v, t) for v in (INPROJ_TILE, MIXER_TILE, MLP_TILE))
    p_kv = ATTN_Q
    p_g = ATTN_Q + 2 * ATTN_KV
    p_z = p_g + GLA_PACK
    for l in range(depth):
        mod = _ada(c, w_ada[l], b_ada[l]).reshape(bsz, N_MOD, 1, d)
        bias, sinkrow = _bias_table(rel_bias, attn_sinks[l])
        w = w_in[l]
        win = jnp.concatenate(
            [_pair_heads(w[:, :p_kv]), w[:, p_kv:p_z],
             jnp.pad(w[:, p_z:], ((0, 0), (0, Z_PAD - GLA_GATE_RANK)))], axis=1).astype(BF16)
        wgu = jnp.pad(w_gate_up[l], ((0, Z_PAD - GLA_GATE_RANK), (0, 0))).astype(BF16)
        qa, kva, gpack, la = _inproj(x, mod, win, wgu, b_gate[l].reshape(1, GLA_K),
                                     jnp.tile(gla_norm_g[l], GLA_HEADS).reshape(1, GLA_V),
                                     tile_in, max(1, tile_in // INPROJ_SUB_TILE))
        attn, gla = _mixer(qa, kva, bias, sinkrow, gpack, la, tile_mix)
        x = _mlp(x, attn, gla, mod,
                 w_out[l].astype(BF16), w_mlp_in[l].astype(BF16), w_mlp_out[l].astype(BF16),
                 final_norm_g.reshape(1, d), tile_mlp, final=(l == depth - 1),
                 nsub=max(1, tile_mlp // MLP_SUB_TILE))
    return x
```

```python
import functools

import numpy as np
import jax
import jax.numpy as jnp
from jax import lax
from jax.experimental import pallas as pl
from jax.experimental.pallas import tpu as pltpu

F32 = jnp.float32
BF16 = jnp.bfloat16

D_MODEL = 1024
ATTN_HEADS = 8
ATTN_KV_HEADS = 2
ATTN_GROUP = ATTN_HEADS // ATTN_KV_HEADS
ATTN_HEAD_DIM = 64
WINDOW = 128
ATTN_BLOCK = 128
NUM_BUCKETS = 32
MAX_DISTANCE = 128
GLA_HEADS = 4
GLA_PAIRS = GLA_HEADS // 2
GLA_DK = 64
GLA_DV = 128
GLA_GATE_RANK = 16
GLA_GATE_NORM = 16.0
GLA_CHUNK = 128
D_FF = 4 * D_MODEL
EPS = 1e-6
N_MOD = 6

ATTN_Q = ATTN_HEADS * ATTN_HEAD_DIM
ATTN_KV = ATTN_KV_HEADS * ATTN_HEAD_DIM
GLA_K = GLA_HEADS * GLA_DK
GLA_V = GLA_HEADS * GLA_DV
GLA_PACK = 2 * GLA_K + 2 * GLA_V
LANES = 128
HALF = LANES // 2
Z_PAD = LANES
LOG2E = 1.4426950408889634

VMEM_LIMIT = 56 * 1024 * 1024
INPROJ_TILE, INPROJ_SUB_TILE = 1024, 512
MIXER_TILE = 2048
MLP_TILE, MLP_SUB_TILE = 1024, 512
MLP_FF_CHUNK = 1024
ADA_COLS_PER_STEP = 1536

NT_DIMS = (((1,), (1,)), ((), ()))
TN_DIMS = (((0,), (0,)), ((), ()))

MOD_SHIFT1, MOD_SCALE1, MOD_GATE1, MOD_SHIFT2, MOD_SCALE2, MOD_GATE2 = range(N_MOD)


def _dot(a, b):
    return jnp.dot(a, b, preferred_element_type=F32)


def _pair_heads(wq):
    d = wq.shape[0]
    return wq.reshape(d, ATTN_KV_HEADS, ATTN_GROUP, ATTN_HEAD_DIM).transpose(0, 2, 1, 3).reshape(d, ATTN_Q)


def _mod_spec(d):
    return pl.BlockSpec((1, N_MOD, 1, d), lambda b, i: (b, 0, 0, 0))


def _rms(x):
    return x * lax.rsqrt(jnp.mean(x * x, axis=-1, keepdims=True) + EPS)


def _ada_kernel(c_ref, w_ref, b_ref, o_ref):
    c = c_ref[...]
    cond = c * jax.nn.sigmoid(c)
    o_ref[...] = jnp.dot(cond, w_ref[...], preferred_element_type=F32,
                         precision=lax.Precision.HIGHEST) + b_ref[...]


def _ada(c, w_ada, b_ada):
    bsz, d = c.shape
    n = w_ada.shape[1]
    tn = ADA_COLS_PER_STEP
    assert n % tn == 0
    return pl.pallas_call(
        _ada_kernel,
        grid=(n // tn,),
        in_specs=[pl.BlockSpec((bsz, d), lambda j: (0, 0)),
                  pl.BlockSpec((d, tn), lambda j: (0, j)),
                  pl.BlockSpec((1, tn), lambda j: (0, j))],
        out_specs=pl.BlockSpec((bsz, tn), lambda j: (0, j)),
        out_shape=jax.ShapeDtypeStruct((bsz, n), F32),
        name="ada",
    )(c, w_ada, b_ada.reshape(1, n))


def _t5_causal_bucket(dist):
    max_exact = NUM_BUCKETS // 2
    d = np.maximum(dist, 0)
    large = max_exact + (np.log(np.maximum(d, max_exact) / max_exact)
                         / np.log(MAX_DISTANCE / max_exact) * (NUM_BUCKETS - max_exact)).astype(np.int32)
    large = np.minimum(large, NUM_BUCKETS - 1)
    return np.where(d < max_exact, d, large).astype(np.int32)


def _bias_kernel(bucket_ref, rb_ref, sink_ref, bias_ref, sinkrow_ref):
    L = ATTN_BLOCK
    bucket = bucket_ref[...]
    key = lax.broadcasted_iota(jnp.int32, (L, L), 0)
    qry = lax.broadcasted_iota(jnp.int32, (L, L), 1)
    prev_key = key > qry
    for g in range(ATTN_KV_HEADS):
        for r in range(ATTN_GROUP):
            h = g * ATTN_GROUP + r
            cols = slice(r * L, (r + 1) * L)
            acc = jnp.zeros((L, L), F32)
            for b in range(NUM_BUCKETS):
                acc = jnp.where(bucket == b, rb_ref[b, h], acc)
            acc = acc * LOG2E
            cur = jnp.where(prev_key, -jnp.inf, acc)
            bias_ref[0, g, :L, cols] = jnp.full((L, L), -jnp.inf, F32)
            bias_ref[1, g, :L, cols] = jnp.where(prev_key, acc, -jnp.inf)
            bias_ref[0, g, L:, cols] = cur
            bias_ref[1, g, L:, cols] = cur
            sinkrow_ref[g, :, cols] = jnp.full((1, L), sink_ref[h] * LOG2E, F32)


def _bias_table(rel_bias, sinks):
    L = ATTN_BLOCK
    c = np.arange(L)[:, None]
    i = np.arange(L)[None, :]
    bucket = _t5_causal_bucket((i - c) % L)
    return pl.pallas_call(
        _bias_kernel,
        in_specs=[pl.BlockSpec(memory_space=pltpu.VMEM),
                  pl.BlockSpec(memory_space=pltpu.SMEM),
                  pl.BlockSpec(memory_space=pltpu.SMEM)],
        out_specs=[pl.BlockSpec(memory_space=pltpu.VMEM), pl.BlockSpec(memory_space=pltpu.VMEM)],
        out_shape=[jax.ShapeDtypeStruct((2, ATTN_KV_HEADS, 2 * L, ATTN_GROUP * L), F32),
                   jax.ShapeDtypeStruct((ATTN_KV_HEADS, 1, ATTN_GROUP * L), F32)],
        name="bias_table",
    )(jnp.asarray(bucket), rel_bias.astype(F32), sinks.astype(F32))


def _inproj_kernel(x_ref, mod_ref, w_ref, wgu_ref, bg_ref, gn_ref,
                   qa_ref, kva_ref, g_ref, la_ref, *, nsub):
    rs = x_ref.shape[1] // nsub
    p_kv = ATTN_Q
    p_g = p_kv + 2 * ATTN_KV
    p_go = p_g + 2 * GLA_K + GLA_V
    p_z = p_g + GLA_PACK

    def norm(s):
        x = x_ref[0, s * rs:(s + 1) * rs, :]
        return (_rms(x) * (1.0 + mod_ref[0, MOD_SCALE1]) + mod_ref[0, MOD_SHIFT1]).astype(BF16)

    def finish(s, y):
        rows = slice(s * rs, (s + 1) * rs)
        gp = _dot(y[:, p_z:].astype(BF16), wgu_ref[...]) + bg_ref[...]
        log_sig = jnp.minimum(gp, 0.0) - jnp.log1p(jnp.exp(-jnp.abs(gp)))
        la = log_sig * (LOG2E / GLA_GATE_NORM)
        la_hi = la.astype(BF16)
        la_ref[0, rows, :GLA_K] = la_hi
        la_ref[0, rows, GLA_K:] = (la - la_hi.astype(F32)).astype(BF16)
        go = y[:, p_go:p_z]
        g_ref[0, rows, 2 * GLA_K + GLA_V:] = (go * jax.nn.sigmoid(go) * gn_ref[...]).astype(BF16)
        g_ref[0, rows, :GLA_K] = (y[:, p_g:p_g + GLA_K] * (GLA_DK ** -0.5)).astype(BF16)
        g_ref[0, rows, GLA_K:2 * GLA_K + GLA_V] = y[:, p_g + GLA_K:p_go].astype(BF16)
        qa_ref[0, rows, :] = (y[:, :p_kv] * (ATTN_HEAD_DIM ** -0.5 * LOG2E)).astype(BF16)
        kva_ref[0, rows, :] = y[:, p_kv:p_g].astype(BF16)

    hb = norm(0)
    y_prev = None
    for s in range(nsub):
        y = _dot(hb, w_ref[...])
        if s + 1 < nsub:
            hb = norm(s + 1)
        if y_prev is not None:
            finish(s - 1, y_prev)
        y_prev = y
    finish(nsub - 1, y_prev)


def _inproj(x, mod, w, wgu, bg, gn, tm, nsub):
    bsz, t, d = x.shape
    const = lambda shape: pl.BlockSpec(shape, lambda b, i: (0,) * len(shape),
                                       pipeline_mode=pl.Buffered(1))
    tile = lambda n: pl.BlockSpec((1, tm, n), lambda b, i: (b, i, 0))
    return pl.pallas_call(
        functools.partial(_inproj_kernel, nsub=nsub),
        grid=(bsz, t // tm),
        in_specs=[tile(d), _mod_spec(d), const(w.shape), const(wgu.shape), const(bg.shape), const(gn.shape)],
        out_specs=[tile(ATTN_Q), tile(2 * ATTN_KV), tile(GLA_PACK), tile(2 * GLA_K)],
        out_shape=[jax.ShapeDtypeStruct((bsz, t, ATTN_Q), BF16),
                   jax.ShapeDtypeStruct((bsz, t, 2 * ATTN_KV), BF16),
                   jax.ShapeDtypeStruct((bsz, t, GLA_PACK), BF16),
                   jax.ShapeDtypeStruct((bsz, t, 2 * GLA_K), BF16)],
        compiler_params=pltpu.CompilerParams(
            dimension_semantics=("arbitrary", "arbitrary"), vmem_limit_bytes=VMEM_LIMIT),
        name="inproj",
    )(x, mod, w, wgu, bg, gn)


def _swa_scores(n, g, q_ref, kv_ref, kvp_ref, bias_ref):
    L = ATTN_BLOCK
    G = ATTN_GROUP
    low = lax.broadcasted_iota(jnp.int32, (L, LANES), 1) < HALF
    zero = jnp.zeros((), BF16)
    rows = slice(n * L, (n + 1) * L)
    k_prev = kvp_ref[0, :, :ATTN_KV] if n == 0 else kv_ref[0, (n - 1) * L:n * L, :ATTN_KV]
    k2 = jnp.concatenate([k_prev, kv_ref[0, rows, :ATTN_KV]], axis=0)
    sel = jnp.where(pl.program_id(1) == 0, 0, 1) if n == 0 else 1
    keep = low if g == 0 else jnp.logical_not(low)
    qs = jnp.concatenate(
        [jnp.where(keep, q_ref[0, rows, p * LANES:(p + 1) * LANES], zero) for p in range(G)],
        axis=0)
    s = lax.dot_general(k2, qs, NT_DIMS, preferred_element_type=F32) + bias_ref[sel, g]
    return jnp.maximum(s[:L], s[L:])


def _swa_softmax(g, sc, sink_ref):
    L = ATTN_BLOCK
    G = ATTN_GROUP
    key = lax.broadcasted_iota(jnp.int32, (L, G * L), 0)
    qry = lax.broadcasted_iota(jnp.int32, (L, G * L), 1) & (L - 1)
    prev_key = key > qry
    sink = sink_ref[g]
    m = jnp.maximum(jnp.max(sc, axis=0, keepdims=True), sink)
    p_ = jnp.exp2(sc - m)
    denom = jnp.sum(p_, axis=0, keepdims=True) + jnp.exp2(sink - m)
    pb = p_.astype(BF16)
    zero = jnp.zeros((), BF16)
    p2 = jnp.concatenate([jnp.where(prev_key, pb, zero), jnp.where(prev_key, zero, pb)], axis=0)
    return p2, denom


def _swa_pv(n, g, p2, denom, kv_ref, kvp_ref, o_ref):
    L = ATTN_BLOCK
    G = ATTN_GROUP
    dh = ATTN_HEAD_DIM
    rows = slice(n * L, (n + 1) * L)
    v_prev = kvp_ref[0, :, ATTN_KV:] if n == 0 else kv_ref[0, (n - 1) * L:n * L, ATTN_KV:]
    v2 = jnp.concatenate([v_prev, kv_ref[0, rows, ATTN_KV:]], axis=0)
    o = lax.dot_general(v2, p2, TN_DIMS, preferred_element_type=F32)
    o = (o[g * dh:(g + 1) * dh] * (1.0 / denom)).astype(BF16)
    for r in range(G):
        h = g * G + r
        o_ref[0, h * dh:(h + 1) * dh, rows] = o[:, r * L:(r + 1) * L]


def _gla_prep(c, g_ref, la_ref):
    C = GLA_CHUNK
    ri = lax.broadcasted_iota(jnp.int32, (C, C), 0)
    ci = lax.broadcasted_iota(jnp.int32, (C, C), 1)
    tril = (ri >= ci).astype(BF16)
    rows = slice(c * C, (c + 1) * C)
    b = _dot(tril, la_ref[0, rows, :GLA_K]) + _dot(tril, la_ref[0, rows, GLA_K:])
    b_mid = b[C // 2 - 1:C // 2, :]
    b_last = b[C - 1:C, :]
    q = g_ref[0, rows, 0:GLA_K].astype(F32)
    k = g_ref[0, rows, GLA_K:2 * GLA_K].astype(F32)
    q_t = q * jnp.exp2(b - b_mid)
    k_t = k * jnp.exp2(b_mid - b)
    q_in = (q_t * jnp.exp2(b_mid)).astype(BF16)
    k_s = (k_t * jnp.exp2(b_last - b_mid)).astype(BF16)
    return q_in, q_t.astype(BF16), k_t.astype(BF16), k_s, jnp.exp2(b_last)


def _gla_intra(prep):
    C = GLA_CHUNK
    dk = GLA_DK
    _, q_t, k_t, _, _ = prep
    ri2 = lax.broadcasted_iota(jnp.int32, (C, 2 * C), 0)
    ci2 = lax.broadcasted_iota(jnp.int32, (C, 2 * C), 1) & (C - 1)
    causal2 = ri2 >= ci2
    low_k = lax.broadcasted_iota(jnp.int32, (C, 2 * dk), 1) < dk
    zero = jnp.zeros((), BF16)
    out = []
    for p in range(GLA_PAIRS):
        ks = slice(p * 2 * dk, (p + 1) * 2 * dk)
        kt_p = k_t[:, ks]
        kbd = jnp.concatenate([jnp.where(low_k, kt_p, zero), jnp.where(low_k, zero, kt_p)], axis=0)
        a = lax.dot_general(q_t[:, ks], kbd, NT_DIMS, preferred_element_type=F32)
        out.append(jnp.where(causal2, a.astype(BF16), zero))
    return out


def _gla_out(c, prep, a, st, g_ref, o_ref):
    C = GLA_CHUNK
    dk, dv = GLA_DK, GLA_DV
    q_in, _, _, k_s, decay = prep
    low_k = lax.broadcasted_iota(jnp.int32, (C, 2 * dk), 1) < dk
    zero = jnp.zeros((), BF16)
    zeros_v = jnp.zeros((C, dv), BF16)
    rows = slice(c * C, (c + 1) * C)
    st_new = []
    for p in range(GLA_PAIRS):
        ks = slice(p * 2 * dk, (p + 1) * 2 * dk)
        vs = slice(2 * GLA_K + p * 2 * dv, 2 * GLA_K + (p + 1) * 2 * dv)
        gs = slice(2 * GLA_K + GLA_V + p * 2 * dv, 2 * GLA_K + GLA_V + (p + 1) * 2 * dv)
        v_0 = g_ref[0, rows, vs.start:vs.start + dv]
        v_1 = g_ref[0, rows, vs.start + dv:vs.stop]
        vbd = jnp.concatenate([jnp.concatenate([v_0, zeros_v], axis=1),
                               jnp.concatenate([zeros_v, v_1], axis=1)], axis=0)
        o_p = _dot(a[p], vbd) + lax.dot_general(q_in[:, ks], st[p].astype(BF16), NT_DIMS,
                                                preferred_element_type=F32)
        ks_p = k_s[:, ks]
        dst = jnp.concatenate(
            [lax.dot_general(v_0, jnp.where(low_k, ks_p, zero), TN_DIMS, preferred_element_type=F32),
             lax.dot_general(v_1, jnp.where(low_k, zero, ks_p), TN_DIMS, preferred_element_type=F32)],
            axis=0)
        st_new.append(st[p] * decay[:, ks] + dst)
        o_n = jnp.concatenate([_rms(o_p[:, :dv]), _rms(o_p[:, dv:])], axis=1)
        o_ref[0, rows, p * 2 * dv:(p + 1) * 2 * dv] = (o_n * g_ref[0, rows, gs].astype(F32)).astype(BF16)
    return st_new


def _mixer_kernel(q_ref, kv_ref, kvp_ref, bias_ref, sink_ref, g_ref, la_ref,
                  attn_ref, gla_ref, st_ref, *, nblk):
    @pl.when(pl.program_id(1) == 0)
    def _():
        st_ref[...] = jnp.zeros_like(st_ref)

    st = [st_ref[p] for p in range(GLA_PAIRS)]
    units = [(n, g) for n in range(nblk) for g in range(ATTN_KV_HEADS)]
    nu = len(units)
    sc, pd, prep, intra = {}, {}, {}, {}

    def emit(kind, i):
        nonlocal st
        if kind == "prep" and 0 <= i < nblk:
            prep[i] = _gla_prep(i, g_ref, la_ref)
        elif kind == "intra" and 0 <= i < nblk:
            intra[i] = _gla_intra(prep[i])
        elif kind == "out" and 0 <= i < nblk:
            st = _gla_out(i, prep.pop(i), intra.pop(i), st, g_ref, gla_ref)
        elif kind == "scores" and 0 <= i < nu:
            sc[i] = _swa_scores(*units[i], q_ref, kv_ref, kvp_ref, bias_ref)
        elif kind == "softmax" and 0 <= i < nu:
            pd[i] = _swa_softmax(units[i][1], sc.pop(i), sink_ref)
        elif kind == "pv" and 0 <= i < nu:
            _swa_pv(*units[i], *pd.pop(i), kv_ref, kvp_ref, attn_ref)

    for slot in range(nu + 2 * ATTN_KV_HEADS):
        if slot % ATTN_KV_HEADS == 0:
            c = slot // ATTN_KV_HEADS
            emit("out", c - 2)
            emit("intra", c - 1)
            emit("prep", c)
        emit("pv", slot - 2)
        emit("softmax", slot - 1)
        emit("scores", slot)
    for p in range(GLA_PAIRS):
        st_ref[p] = st[p]


def _mixer(qa, kva, bias, sinkrow, gpack, la, tq):
    assert ATTN_BLOCK == GLA_CHUNK == WINDOW
    bsz, t, _ = qa.shape
    assert t % tq == 0 and tq % ATTN_BLOCK == 0
    nblk = tq // ATTN_BLOCK
    whole = lambda a: pl.BlockSpec(a.shape, lambda b, i: (0,) * a.ndim, pipeline_mode=pl.Buffered(1))
    tile = lambda n: pl.BlockSpec((1, tq, n), lambda b, i: (b, i, 0))
    return pl.pallas_call(
        functools.partial(_mixer_kernel, nblk=nblk),
        grid=(bsz, t // tq),
        in_specs=[tile(ATTN_Q), tile(2 * ATTN_KV),
                  pl.BlockSpec((1, ATTN_BLOCK, 2 * ATTN_KV),
                               lambda b, i: (b, jnp.maximum(i * nblk - 1, 0), 0)),
                  whole(bias), whole(sinkrow), tile(GLA_PACK), tile(2 * GLA_K)],
        out_specs=[pl.BlockSpec((1, ATTN_Q, tq), lambda b, i: (b, 0, i)), tile(GLA_V)],
        out_shape=[jax.ShapeDtypeStruct((bsz, ATTN_Q, t), BF16),
                   jax.ShapeDtypeStruct((bsz, t, GLA_V), BF16)],
        scratch_shapes=[pltpu.VMEM((GLA_PAIRS, 2 * GLA_DV, 2 * GLA_DK), F32)],
        compiler_params=pltpu.CompilerParams(
            dimension_semantics=("arbitrary", "arbitrary"), vmem_limit_bytes=VMEM_LIMIT),
        name="mixer",
    )(qa, kva, kva, bias, sinkrow, gpack, la)


def _mlp_kernel(x_ref, attn_ref, gla_ref, mod_ref, wo_ref, w1_ref, w2_ref,
                fg_ref, o_ref, u_ref, *, fchunk, final, nsub):
    rs = x_ref.shape[1] // nsub
    nf = D_FF // fchunk

    def pre(s):
        rows = slice(s * rs, (s + 1) * rs)
        mix = (lax.dot_general(attn_ref[0, :, rows], wo_ref[:ATTN_Q, :], TN_DIMS, preferred_element_type=F32)
               + _dot(gla_ref[0, rows, :], wo_ref[ATTN_Q:, :]))
        x1 = x_ref[0, rows, :] + mod_ref[0, MOD_GATE1] * mix
        return x1, (_rms(x1) * (1.0 + mod_ref[0, MOD_SCALE2]) + mod_ref[0, MOD_SHIFT2]).astype(BF16)

    def up(s, hb, c):
        cols = slice(c * fchunk, (c + 1) * fchunk)
        u = jnp.maximum(_dot(hb, w1_ref[:, cols]), 0.0)
        u_ref[s, :, cols] = (u * u).astype(BF16)

    def post(s, x1, y):
        x2 = x1 + mod_ref[0, MOD_GATE2] * y
        o_ref[0, s * rs:(s + 1) * rs, :] = _rms(x2) * fg_ref[...] if final else x2

    x1, hb = pre(0)
    pending = None
    for s in range(nsub):
        for c in range(nf // 2):
            up(s, hb, c)
        if s + 1 < nsub:
            nxt = pre(s + 1)
        for c in range(nf // 2, nf):
            up(s, hb, c)
        if pending is not None:
            post(*pending)
        pending = (s, x1, _dot(u_ref[s], w2_ref[...]))
        if s + 1 < nsub:
            x1, hb = nxt
    post(*pending)


def _mlp(x, attn, gla, mod, wo, w1, w2, fg, tm, final, nsub):
    bsz, t, d = x.shape
    const = lambda shape: pl.BlockSpec(shape, lambda b, i: (0,) * len(shape),
                                       pipeline_mode=pl.Buffered(1))
    tile = lambda n: pl.BlockSpec((1, tm, n), lambda b, i: (b, i, 0))
    return pl.pallas_call(
        functools.partial(_mlp_kernel, fchunk=MLP_FF_CHUNK, final=final, nsub=nsub),
        grid=(bsz, t // tm),
        in_specs=[tile(d), pl.BlockSpec((1, ATTN_Q, tm), lambda b, i: (b, 0, i)), tile(GLA_V),
                  _mod_spec(d),
                  const(wo.shape), const(w1.shape), const(w2.shape), const(fg.shape)],
        out_specs=tile(d),
        out_shape=jax.ShapeDtypeStruct((bsz, t, d), F32),
        scratch_shapes=[pltpu.VMEM((nsub, tm // nsub, D_FF), BF16)],
        compiler_params=pltpu.CompilerParams(
            dimension_semantics=("arbitrary", "arbitrary"), vmem_limit_bytes=VMEM_LIMIT),
        name="mlp",
    )(x, attn, gla, mod, wo, w1, w2, fg)


def kernel(x, c, w_ada, b_ada, w_in, w_gate_up, b_gate, gla_norm_g, attn_sinks, rel_bias,
           w_out, w_mlp_in, w_mlp_out, final_norm_g):
    bsz, t, d = x.shape
    depth = w_ada.shape[0]
    tile_in, tile_mix, tile_mlp = (min(v, t) for v in (INPROJ_TILE, MIXER_TILE, MLP_TILE))
    p_kv = ATTN_Q
    p_g = ATTN_Q + 2 * ATTN_KV
    p_z = p_g + GLA_PACK
    for l in range(depth):
        mod = _ada(c, w_ada[l], b_ada[l]).reshape(bsz, N_MOD, 1, d)
        bias, sinkrow = _bias_table(rel_bias, attn_sinks[l])
        w = w_in[l]
        win = jnp.concatenate(
            [_pair_heads(w[:, :p_kv]), w[:, p_kv:p_z],
             jnp.pad(w[:, p_z:], ((0, 0), (0, Z_PAD - GLA_GATE_RANK)))], axis=1).astype(BF16)
        wgu = jnp.pad(w_gate_up[l], ((0, Z_PAD - GLA_GATE_RANK), (0, 0))).astype(BF16)
        qa, kva, gpack, la = _inproj(x, mod, win, wgu, b_gate[l].reshape(1, GLA_K),
                                     jnp.tile(gla_norm_g[l], GLA_HEADS).reshape(1, GLA_V),
                                     tile_in, max(1, tile_in // INPROJ_SUB_TILE))
        attn, gla = _mixer(qa, kva, bias, sinkrow, gpack, la, tile_mix)
        x = _mlp(x, attn, gla, mod,
                 w_out[l].astype(BF16), w_mlp_in[l].astype(BF16), w_mlp_out[l].astype(BF16),
                 final_norm_g.reshape(1, d), tile_mlp, final=(l == depth - 1),
                 nsub=max(1, tile_mlp // MLP_SUB_TILE))
    return x
```

```python
import functools

import numpy as np
import jax
import jax.numpy as jnp
from jax import lax
from jax.experimental import pallas as pl
from jax.experimental.pallas import tpu as pltpu

F32 = jnp.float32
BF16 = jnp.bfloat16

D_MODEL = 1024
ATTN_HEADS = 8
ATTN_KV_HEADS = 2
ATTN_GROUP = ATTN_HEADS // ATTN_KV_HEADS
ATTN_HEAD_DIM = 64
WINDOW = 128
ATTN_BLOCK = 128
NUM_BUCKETS = 32
MAX_DISTANCE = 128
GLA_HEADS = 4
GLA_PAIRS = GLA_HEADS // 2
GLA_DK = 64
GLA_DV = 128
GLA_GATE_RANK = 16
GLA_GATE_NORM = 16.0
GLA_CHUNK = 128
D_FF = 4 * D_MODEL
EPS = 1e-6
N_MOD = 6

ATTN_Q = ATTN_HEADS * ATTN_HEAD_DIM
ATTN_KV = ATTN_KV_HEADS * ATTN_HEAD_DIM
GLA_K = GLA_HEADS * GLA_DK
GLA_V = GLA_HEADS * GLA_DV
GLA_PACK = 2 * GLA_K + 2 * GLA_V
LANES = 128
HALF = LANES // 2
Z_PAD = LANES
LOG2E = 1.4426950408889634

VMEM_LIMIT = 56 * 1024 * 1024
INPROJ_TILE, INPROJ_SUB_TILE = 1024, 256
MIXER_TILE = 2048
MLP_TILE, MLP_SUB_TILE = 1024, 512
MLP_FF_CHUNK = 1024
ADA_COLS_PER_STEP = 1536

NT_DIMS = (((1,), (1,)), ((), ()))
TN_DIMS = (((0,), (0,)), ((), ()))

MOD_SHIFT1, MOD_SCALE1, MOD_GATE1, MOD_SHIFT2, MOD_SCALE2, MOD_GATE2 = range(N_MOD)


def _dot(a, b):
    return jnp.dot(a, b, preferred_element_type=F32)


def _pair_heads(wq):
    d = wq.shape[0]
    return wq.reshape(d, ATTN_KV_HEADS, ATTN_GROUP, ATTN_HEAD_DIM).transpose(0, 2, 1, 3).reshape(d, ATTN_Q)


def _mod_spec(d):
    return pl.BlockSpec((1, N_MOD, 1, d), lambda b, i: (b, 0, 0, 0))


def _rms(x):
    return x * lax.rsqrt(jnp.mean(x * x, axis=-1, keepdims=True) + EPS)


def _ada_kernel(c_ref, w_ref, b_ref, o_ref):
    c = c_ref[...]
    cond = c * jax.nn.sigmoid(c)
    o_ref[...] = jnp.dot(cond, w_ref[...], preferred_element_type=F32,
                         precision=lax.Precision.HIGHEST) + b_ref[...]


def _ada(c, w_ada, b_ada):
    bsz, d = c.shape
    n = w_ada.shape[1]
    tn = ADA_COLS_PER_STEP
    assert n % tn == 0
    return pl.pallas_call(
        _ada_kernel,
        grid=(n // tn,),
        in_specs=[pl.BlockSpec((bsz, d), lambda j: (0, 0)),
                  pl.BlockSpec((d, tn), lambda j: (0, j)),
                  pl.BlockSpec((1, tn), lambda j: (0, j))],
        out_specs=pl.BlockSpec((bsz, tn), lambda j: (0, j)),
        out_shape=jax.ShapeDtypeStruct((bsz, n), F32),
        name="ada",
    )(c, w_ada, b_ada.reshape(1, n))


def _t5_causal_bucket(dist):
    max_exact = NUM_BUCKETS // 2
    d = np.maximum(dist, 0)
    large = max_exact + (np.log(np.maximum(d, max_exact) / max_exact)
                         / np.log(MAX_DISTANCE / max_exact) * (NUM_BUCKETS - max_exact)).astype(np.int32)
    large = np.minimum(large, NUM_BUCKETS - 1)
    return np.where(d < max_exact, d, large).astype(np.int32)


def _bias_kernel(bucket_ref, rb_ref, sink_ref, bias_ref, sinkrow_ref):
    L = ATTN_BLOCK
    bucket = bucket_ref[...]
    key = lax.broadcasted_iota(jnp.int32, (L, L), 0)
    qry = lax.broadcasted_iota(jnp.int32, (L, L), 1)
    prev_key = key > qry
    for g in range(ATTN_KV_HEADS):
        for r in range(ATTN_GROUP):
            h = g * ATTN_GROUP + r
            cols = slice(r * L, (r + 1) * L)
            acc = jnp.zeros((L, L), F32)
            for b in range(NUM_BUCKETS):
                acc = jnp.where(bucket == b, rb_ref[b, h], acc)
            acc = acc * LOG2E
            cur = jnp.where(prev_key, -jnp.inf, acc)
            bias_ref[0, g, :L, cols] = jnp.full((L, L), -jnp.inf, F32)
            bias_ref[1, g, :L, cols] = jnp.where(prev_key, acc, -jnp.inf)
            bias_ref[0, g, L:, cols] = cur
            bias_ref[1, g, L:, cols] = cur
            sinkrow_ref[g, :, cols] = jnp.full((1, L), sink_ref[h] * LOG2E, F32)


def _bias_table(rel_bias, sinks):
    L = ATTN_BLOCK
    c = np.arange(L)[:, None]
    i = np.arange(L)[None, :]
    bucket = _t5_causal_bucket((i - c) % L)
    return pl.pallas_call(
        _bias_kernel,
        in_specs=[pl.BlockSpec(memory_space=pltpu.VMEM),
                  pl.BlockSpec(memory_space=pltpu.SMEM),
                  pl.BlockSpec(memory_space=pltpu.SMEM)],
        out_specs=[pl.BlockSpec(memory_space=pltpu.VMEM), pl.BlockSpec(memory_space=pltpu.VMEM)],
        out_shape=[jax.ShapeDtypeStruct((2, ATTN_KV_HEADS, 2 * L, ATTN_GROUP * L), F32),
                   jax.ShapeDtypeStruct((ATTN_KV_HEADS, 1, ATTN_GROUP * L), F32)],
        name="bias_table",
    )(jnp.asarray(bucket), rel_bias.astype(F32), sinks.astype(F32))


def _inproj_kernel(x_ref, mod_ref, w_ref, wgu_ref, bg_ref, gn_ref,
                   qa_ref, kva_ref, g_ref, la_ref, *, nsub):
    rs = x_ref.shape[1] // nsub
    p_kv = ATTN_Q
    p_g = p_kv + 2 * ATTN_KV
    p_go = p_g + 2 * GLA_K + GLA_V
    p_z = p_g + GLA_PACK

    def norm(s):
        x = x_ref[0, s * rs:(s + 1) * rs, :]
        return (_rms(x) * (1.0 + mod_ref[0, MOD_SCALE1]) + mod_ref[0, MOD_SHIFT1]).astype(BF16)

    def finish(s, y):
        rows = slice(s * rs, (s + 1) * rs)
        gp = _dot(y[:, p_z:].astype(BF16), wgu_ref[...]) + bg_ref[...]
        gp2 = gp * LOG2E
        la = (jnp.minimum(gp2, 0.0) - jnp.log2(1.0 + jnp.exp2(-jnp.abs(gp2)))) * (1.0 / GLA_GATE_NORM)
        la_hi = la.astype(BF16)
        la_ref[0, rows, :GLA_K] = la_hi
        la_ref[0, rows, GLA_K:] = (la - la_hi.astype(F32)).astype(BF16)
        go = y[:, p_go:p_z]
        g_ref[0, rows, 2 * GLA_K + GLA_V:] = (go * (1.0 + jnp.tanh(0.5 * go)) * (0.5 * gn_ref[...])).astype(BF16)
        g_ref[0, rows, :GLA_K] = (y[:, p_g:p_g + GLA_K] * (GLA_DK ** -0.5)).astype(BF16)
        g_ref[0, rows, GLA_K:2 * GLA_K + GLA_V] = y[:, p_g + GLA_K:p_go].astype(BF16)
        qa_ref[0, rows, :] = (y[:, :p_kv] * (ATTN_HEAD_DIM ** -0.5 * LOG2E)).astype(BF16)
        kva_ref[0, rows, :] = y[:, p_kv:p_g].astype(BF16)

    hb = norm(0)
    y_prev = None
    for s in range(nsub):
        y = _dot(hb, w_ref[...])
        if s + 1 < nsub:
            hb = norm(s + 1)
        if y_prev is not None:
            finish(s - 1, y_prev)
        y_prev = y
    finish(nsub - 1, y_prev)


def _inproj(x, mod, w, wgu, bg, gn, tm, nsub):
    bsz, t, d = x.shape
    const = lambda shape: pl.BlockSpec(shape, lambda b, i: (0,) * len(shape),
                                       pipeline_mode=pl.Buffered(1))
    tile = lambda n: pl.BlockSpec((1, tm, n), lambda b, i: (b, i, 0))
    return pl.pallas_call(
        functools.partial(_inproj_kernel, nsub=nsub),
        grid=(bsz, t // tm),
        in_specs=[tile(d), _mod_spec(d), const(w.shape), const(wgu.shape), const(bg.shape), const(gn.shape)],
        out_specs=[tile(ATTN_Q), tile(2 * ATTN_KV), tile(GLA_PACK), tile(2 * GLA_K)],
        out_shape=[jax.ShapeDtypeStruct((bsz, t, ATTN_Q), BF16),
                   jax.ShapeDtypeStruct((bsz, t, 2 * ATTN_KV), BF16),
                   jax.ShapeDtypeStruct((bsz, t, GLA_PACK), BF16),
                   jax.ShapeDtypeStruct((bsz, t, 2 * GLA_K), BF16)],
        compiler_params=pltpu.CompilerParams(
            dimension_semantics=("arbitrary", "arbitrary"), vmem_limit_bytes=VMEM_LIMIT),
        name="inproj",
    )(x, mod, w, wgu, bg, gn)


def _swa_scores(n, g, q_ref, kv_ref, kvp_ref, bias_ref):
    L = ATTN_BLOCK
    G = ATTN_GROUP
    low = lax.broadcasted_iota(jnp.int32, (L, LANES), 1) < HALF
    zero = jnp.zeros((), BF16)
    rows = slice(n * L, (n + 1) * L)
    k_prev = kvp_ref[0, :, :ATTN_KV] if n == 0 else kv_ref[0, (n - 1) * L:n * L, :ATTN_KV]
    k2 = jnp.concatenate([k_prev, kv_ref[0, rows, :ATTN_KV]], axis=0)
    sel = jnp.where(pl.program_id(1) == 0, 0, 1) if n == 0 else 1
    keep = low if g == 0 else jnp.logical_not(low)
    qs = jnp.concatenate(
        [jnp.where(keep, q_ref[0, rows, p * LANES:(p + 1) * LANES], zero) for p in range(G)],
        axis=0)
    s = lax.dot_general(k2, qs, NT_DIMS, preferred_element_type=F32) + bias_ref[sel, g]
    return jnp.maximum(s[:L], s[L:])


def _swa_softmax(g, sc, sink_ref):
    L = ATTN_BLOCK
    G = ATTN_GROUP
    key = lax.broadcasted_iota(jnp.int32, (L, G * L), 0)
    qry = lax.broadcasted_iota(jnp.int32, (L, G * L), 1) & (L - 1)
    prev_key = key > qry
    sink = sink_ref[g]
    m = jnp.maximum(jnp.max(sc, axis=0, keepdims=True), sink)
    p_ = jnp.exp2(sc - m)
    denom = jnp.sum(p_, axis=0, keepdims=True) + jnp.exp2(sink - m)
    pb = p_.astype(BF16)
    zero = jnp.zeros((), BF16)
    p2 = jnp.concatenate([jnp.where(prev_key, pb, zero), jnp.where(prev_key, zero, pb)], axis=0)
    return p2, denom


def _swa_pv(n, g, p2, denom, kv_ref, kvp_ref, o_ref):
    L = ATTN_BLOCK
    G = ATTN_GROUP
    dh = ATTN_HEAD_DIM
    rows = slice(n * L, (n + 1) * L)
    v_prev = kvp_ref[0, :, ATTN_KV:] if n == 0 else kv_ref[0, (n - 1) * L:n * L, ATTN_KV:]
    v2 = jnp.concatenate([v_prev, kv_ref[0, rows, ATTN_KV:]], axis=0)
    o = lax.dot_general(v2, p2, TN_DIMS, preferred_element_type=F32)
    o = (o[g * dh:(g + 1) * dh] * (1.0 / denom)).astype(BF16)
    for r in range(G):
        h = g * G + r
        o_ref[0, h * dh:(h + 1) * dh, rows] = o[:, r * L:(r + 1) * L]


def _gla_prep(c, g_ref, la_ref):
    C = GLA_CHUNK
    ri = lax.broadcasted_iota(jnp.int32, (C, C), 0)
    ci = lax.broadcasted_iota(jnp.int32, (C, C), 1)
    tril = (ri >= ci).astype(BF16)
    rows = slice(c * C, (c + 1) * C)
    b = _dot(tril, la_ref[0, rows, :GLA_K]) + _dot(tril, la_ref[0, rows, GLA_K:])
    b_mid = b[C // 2 - 1:C // 2, :]
    b_last = b[C - 1:C, :]
    q = g_ref[0, rows, 0:GLA_K].astype(F32)
    k = g_ref[0, rows, GLA_K:2 * GLA_K].astype(F32)
    q_t = q * jnp.exp2(b - b_mid)
    k_t = k * jnp.exp2(b_mid - b)
    q_in = (q_t * jnp.exp2(b_mid)).astype(BF16)
    k_s = (k_t * jnp.exp2(b_last - b_mid)).astype(BF16)
    return q_in, q_t.astype(BF16), k_t.astype(BF16), k_s, jnp.exp2(b_last)


def _gla_intra(prep):
    C = GLA_CHUNK
    dk = GLA_DK
    _, q_t, k_t, _, _ = prep
    ri2 = lax.broadcasted_iota(jnp.int32, (C, 2 * C), 0)
    ci2 = lax.broadcasted_iota(jnp.int32, (C, 2 * C), 1) & (C - 1)
    causal2 = ri2 >= ci2
    low_k = lax.broadcasted_iota(jnp.int32, (C, 2 * dk), 1) < dk
    zero = jnp.zeros((), BF16)
    out = []
    for p in range(GLA_PAIRS):
        ks = slice(p * 2 * dk, (p + 1) * 2 * dk)
        kt_p = k_t[:, ks]
        kbd = jnp.concatenate([jnp.where(low_k, kt_p, zero), jnp.where(low_k, zero, kt_p)], axis=0)
        a = lax.dot_general(q_t[:, ks], kbd, NT_DIMS, preferred_element_type=F32)
        out.append(jnp.where(causal2, a.astype(BF16), zero))
    return out


def _gla_out(c, prep, a, st, g_ref, o_ref):
    C = GLA_CHUNK
    dk, dv = GLA_DK, GLA_DV
    q_in, _, _, k_s, decay = prep
    low_k = lax.broadcasted_iota(jnp.int32, (C, 2 * dk), 1) < dk
    zero = jnp.zeros((), BF16)
    zeros_v = jnp.zeros((C, dv), BF16)
    rows = slice(c * C, (c + 1) * C)
    st_new = []
    for p in range(GLA_PAIRS):
        ks = slice(p * 2 * dk, (p + 1) * 2 * dk)
        vs = slice(2 * GLA_K + p * 2 * dv, 2 * GLA_K + (p + 1) * 2 * dv)
        gs = slice(2 * GLA_K + GLA_V + p * 2 * dv, 2 * GLA_K + GLA_V + (p + 1) * 2 * dv)
        v_0 = g_ref[0, rows, vs.start:vs.start + dv]
        v_1 = g_ref[0, rows, vs.start + dv:vs.stop]
        vbd = jnp.concatenate([jnp.concatenate([v_0, zeros_v], axis=1),
                               jnp.concatenate([zeros_v, v_1], axis=1)], axis=0)
        o_p = _dot(a[p], vbd) + lax.dot_general(q_in[:, ks], st[p].astype(BF16), NT_DIMS,
                                                preferred_element_type=F32)
        ks_p = k_s[:, ks]
        dst = jnp.concatenate(
            [lax.dot_general(v_0, jnp.where(low_k, ks_p, zero), TN_DIMS, preferred_element_type=F32),
             lax.dot_general(v_1, jnp.where(low_k, zero, ks_p), TN_DIMS, preferred_element_type=F32)],
            axis=0)
        st_new.append(st[p] * decay[:, ks] + dst)
        o_n = jnp.concatenate([_rms(o_p[:, :dv]), _rms(o_p[:, dv:])], axis=1)
        o_ref[0, rows, p * 2 * dv:(p + 1) * 2 * dv] = (o_n * g_ref[0, rows, gs].astype(F32)).astype(BF16)
    return st_new


def _mixer_kernel(q_ref, kv_ref, kvp_ref, bias_ref, sink_ref, g_ref, la_ref,
                  attn_ref, gla_ref, st_ref, *, nblk):
    @pl.when(pl.program_id(1) == 0)
    def _():
        st_ref[...] = jnp.zeros_like(st_ref)

    st = [st_ref[p] for p in range(GLA_PAIRS)]
    units = [(n, g) for n in range(nblk) for g in range(ATTN_KV_HEADS)]
    nu = len(units)
    sc, pd, prep, intra = {}, {}, {}, {}

    def emit(kind, i):
        nonlocal st
        if kind == "prep" and 0 <= i < nblk:
            prep[i] = _gla_prep(i, g_ref, la_ref)
        elif kind == "intra" and 0 <= i < nblk:
            intra[i] = _gla_intra(prep[i])
        elif kind == "out" and 0 <= i < nblk:
            st = _gla_out(i, prep.pop(i), intra.pop(i), st, g_ref, gla_ref)
        elif kind == "scores" and 0 <= i < nu:
            sc[i] = _swa_scores(*units[i], q_ref, kv_ref, kvp_ref, bias_ref)
        elif kind == "softmax" and 0 <= i < nu:
            pd[i] = _swa_softmax(units[i][1], sc.pop(i), sink_ref)
        elif kind == "pv" and 0 <= i < nu:
            _swa_pv(*units[i], *pd.pop(i), kv_ref, kvp_ref, attn_ref)

    for slot in range(nu + 2 * ATTN_KV_HEADS):
        if slot % ATTN_KV_HEADS == 0:
            c = slot // ATTN_KV_HEADS
            emit("out", c - 2)
            emit("intra", c - 1)
            emit("prep", c)
        emit("pv", slot - 2)
        emit("softmax", slot - 1)
        emit("scores", slot)
    for p in range(GLA_PAIRS):
        st_ref[p] = st[p]


def _mixer(qa, kva, bias, sinkrow, gpack, la, tq):
    assert ATTN_BLOCK == GLA_CHUNK == WINDOW
    bsz, t, _ = qa.shape
    assert t % tq == 0 and tq % ATTN_BLOCK == 0
    nblk = tq // ATTN_BLOCK
    whole = lambda a: pl.BlockSpec(a.shape, lambda b, i: (0,) * a.ndim, pipeline_mode=pl.Buffered(1))
    tile = lambda n: pl.BlockSpec((1, tq, n), lambda b, i: (b, i, 0))
    return pl.pallas_call(
        functools.partial(_mixer_kernel, nblk=nblk),
        grid=(bsz, t // tq),
        in_specs=[tile(ATTN_Q), tile(2 * ATTN_KV),
                  pl.BlockSpec((1, ATTN_BLOCK, 2 * ATTN_KV),
                               lambda b, i: (b, jnp.maximum(i * nblk - 1, 0), 0)),
                  whole(bias), whole(sinkrow), tile(GLA_PACK), tile(2 * GLA_K)],
        out_specs=[pl.BlockSpec((1, ATTN_Q, tq), lambda b, i: (b, 0, i)), tile(GLA_V)],
        out_shape=[jax.ShapeDtypeStruct((bsz, ATTN_Q, t), BF16),
                   jax.ShapeDtypeStruct((bsz, t, GLA_V), BF16)],
        scratch_shapes=[pltpu.VMEM((GLA_PAIRS, 2 * GLA_DV, 2 * GLA_DK), F32)],
        compiler_params=pltpu.CompilerParams(
            dimension_semantics=("arbitrary", "arbitrary"), vmem_limit_bytes=VMEM_LIMIT),
        name="mixer",
    )(qa, kva, kva, bias, sinkrow, gpack, la)


def _mlp_kernel(x_ref, attn_ref, gla_ref, mod_ref, wo_ref, w1_ref, w2_ref,
                fg_ref, o_ref, u_ref, *, fchunk, final, nsub):
    rs = x_ref.shape[1] // nsub
    nf = D_FF // fchunk

    def pre(s):
        rows = slice(s * rs, (s + 1) * rs)
        mix = (lax.dot_general(attn_ref[0, :, rows], wo_ref[:ATTN_Q, :], TN_DIMS, preferred_element_type=F32)
               + _dot(gla_ref[0, rows, :], wo_ref[ATTN_Q:, :]))
        x1 = x_ref[0, rows, :] + mod_ref[0, MOD_GATE1] * mix
        return x1, (_rms(x1) * (1.0 + mod_ref[0, MOD_SCALE2]) + mod_ref[0, MOD_SHIFT2]).astype(BF16)

    def up(s, hb, c):
        cols = slice(c * fchunk, (c + 1) * fchunk)
        u = jnp.maximum(_dot(hb, w1_ref[:, cols]), 0.0)
        u_ref[s, :, cols] = (u * u).astype(BF16)

    def post(s, x1, y):
        x2 = x1 + mod_ref[0, MOD_GATE2] * y
        o_ref[0, s * rs:(s + 1) * rs, :] = _rms(x2) * fg_ref[...] if final else x2

    x1, hb = pre(0)
    pending = None
    for s in range(nsub):
        for c in range(nf // 2):
            up(s, hb, c)
        if s + 1 < nsub:
            nxt = pre(s + 1)
        for c in range(nf // 2, nf):
            up(s, hb, c)
        if pending is not None:
            post(*pending)
        pending = (s, x1, _dot(u_ref[s], w2_ref[...]))
        if s + 1 < nsub:
            x1, hb = nxt
    post(*pending)


def _mlp(x, attn, gla, mod, wo, w1, w2, fg, tm, final, nsub):
    bsz, t, d = x.shape
    const = lambda shape: pl.BlockSpec(shape, lambda b, i: (0,) * len(shape),
                                       pipeline_mode=pl.Buffered(1))
    tile = lambda n: pl.BlockSpec((1, tm, n), lambda b, i: (b, i, 0))
    return pl.pallas_call(
        functools.partial(_mlp_kernel, fchunk=MLP_FF_CHUNK, final=final, nsub=nsub),
        grid=(bsz, t // tm),
        in_specs=[tile(d), pl.BlockSpec((1, ATTN_Q, tm), lambda b, i: (b, 0, i)), tile(GLA_V),
                  _mod_spec(d),
                  const(wo.shape), const(w1.shape), const(w2.shape), const(fg.shape)],
        out_specs=tile(d),
        out_shape=jax.ShapeDtypeStruct((bsz, t, d), F32),
        scratch_shapes=[pltpu.VMEM((nsub, tm // nsub, D_FF), BF16)],
        compiler_params=pltpu.CompilerParams(
            dimension_semantics=("arbitrary", "arbitrary"), vmem_limit_bytes=VMEM_LIMIT),
        name="mlp",
    )(x, attn, gla, mod, wo, w1, w2, fg)


def kernel(x, c, w_ada, b_ada, w_in, w_gate_up, b_gate, gla_norm_g, attn_sinks, rel_bias,
           w_out, w_mlp_in, w_mlp_out, final_norm_g):
    bsz, t, d = x.shape
    depth = w_ada.shape[0]
    tile_in, tile_mix, tile_mlp = (min(v, t) for v in (INPROJ_TILE, MIXER_TILE, MLP_TILE))
    p_kv = ATTN_Q
    p_g = ATTN_Q + 2 * ATTN_KV
    p_z = p_g + GLA_PACK
    for l in range(depth):
        mod = _ada(c, w_ada[l], b_ada[l]).reshape(bsz, N_MOD, 1, d)
        bias, sinkrow = _bias_table(rel_bias, attn_sinks[l])
        w = w_in[l]
        win = jnp.concatenate(
            [_pair_heads(w[:, :p_kv]), w[:, p_kv:p_z],
             jnp.pad(w[:, p_z:], ((0, 0), (0, Z_PAD - GLA_GATE_RANK)))], axis=1).astype(BF16)
        wgu = jnp.pad(w_gate_up[l], ((0, Z_PAD - GLA_GATE_RANK), (0, 0))).astype(BF16)
        qa, kva, gpack, la = _inproj(x, mod, win, wgu, b_gate[l].reshape(1, GLA_K),
                                     jnp.tile(gla_norm_g[l], GLA_HEADS).reshape(1, GLA_V),
                                     tile_in, max(1, tile_in // INPROJ_SUB_TILE))
        attn, gla = _mixer(qa, kva, bias, sinkrow, gpack, la, tile_mix)
        x = _mlp(x, attn, gla, mod,
                 w_out[l].astype(BF16), w_mlp_in[l].astype(BF16), w_mlp_out[l].astype(BF16),
                 final_norm_g.reshape(1, d), tile_mlp, final=(l == depth - 1),
                 nsub=max(1, tile_mlp // MLP_SUB_TILE))
    return x
```

```python
import functools

import numpy as np
import jax
import jax.numpy as jnp
from jax import lax
from jax.experimental import pallas as pl
from jax.experimental.pallas import tpu as pltpu

F32 = jnp.float32
BF16 = jnp.bfloat16

D_MODEL = 1024
ATTN_HEADS = 8
ATTN_KV_HEADS = 2
ATTN_GROUP = ATTN_HEADS // ATTN_KV_HEADS
ATTN_HEAD_DIM = 64
WINDOW = 128
ATTN_BLOCK = 128
NUM_BUCKETS = 32
MAX_DISTANCE = 128
GLA_HEADS = 4
GLA_PAIRS = GLA_HEADS // 2
GLA_DK = 64
GLA_DV = 128
GLA_GATE_RANK = 16
GLA_GATE_NORM = 16.0
GLA_CHUNK = 128
D_FF = 4 * D_MODEL
EPS = 1e-6
N_MOD = 6

ATTN_Q = ATTN_HEADS * ATTN_HEAD_DIM
ATTN_KV = ATTN_KV_HEADS * ATTN_HEAD_DIM
GLA_K = GLA_HEADS * GLA_DK
GLA_V = GLA_HEADS * GLA_DV
GLA_PACK = 2 * GLA_K + 2 * GLA_V
LANES = 128
Z_PAD = LANES
LOG2E = 1.4426950408889634

VMEM_LIMIT = 56 * 1024 * 1024
INPROJ_TILE, INPROJ_SUB_TILE = 1024, 256
MIXER_TILE = 2048
MLP_TILE, MLP_SUB_TILE = 1024, 512
MLP_FF_CHUNK = 1024
ADA_COLS_PER_STEP = 1536

NT_DIMS = (((1,), (1,)), ((), ()))
TN_DIMS = (((0,), (0,)), ((), ()))

MOD_SHIFT1, MOD_SCALE1, MOD_GATE1, MOD_SHIFT2, MOD_SCALE2, MOD_GATE2 = range(N_MOD)


def _dot(a, b):
    return jnp.dot(a, b, preferred_element_type=F32)


def _pair_heads(wq):
    d = wq.shape[0]
    return wq.reshape(d, ATTN_KV_HEADS, ATTN_GROUP, ATTN_HEAD_DIM).transpose(0, 2, 1, 3).reshape(d, ATTN_Q)


def _mod_spec(d):
    return pl.BlockSpec((1, N_MOD, 1, d), lambda b, i: (b, 0, 0, 0))


def _rms(x):
    return x * lax.rsqrt(jnp.mean(x * x, axis=-1, keepdims=True) + EPS)


def _ada_kernel(c_ref, w_ref, b_ref, o_ref):
    c = c_ref[...]
    cond = c * jax.nn.sigmoid(c)
    c_hi = cond.astype(BF16)
    c_lo = (cond - c_hi.astype(F32)).astype(BF16)
    w = w_ref[...]
    w_hi = w.astype(BF16)
    w_lo = (w - w_hi.astype(F32)).astype(BF16)
    o_ref[...] = _dot(c_hi, w_hi) + (_dot(c_hi, w_lo) + _dot(c_lo, w_hi)) + b_ref[...]


def _ada(c, w_ada, b_ada):
    bsz, d = c.shape
    n = w_ada.shape[1]
    tn = ADA_COLS_PER_STEP
    assert n % tn == 0
    return pl.pallas_call(
        _ada_kernel,
        grid=(n // tn,),
        in_specs=[pl.BlockSpec((bsz, d), lambda j: (0, 0)),
                  pl.BlockSpec((d, tn), lambda j: (0, j)),
                  pl.BlockSpec((1, tn), lambda j: (0, j))],
        out_specs=pl.BlockSpec((bsz, tn), lambda j: (0, j)),
        out_shape=jax.ShapeDtypeStruct((bsz, n), F32),
        name="ada",
    )(c, w_ada, b_ada.reshape(1, n))


def _t5_causal_bucket(dist):
    max_exact = NUM_BUCKETS // 2
    d = np.maximum(dist, 0)
    large = max_exact + (np.log(np.maximum(d, max_exact) / max_exact)
                         / np.log(MAX_DISTANCE / max_exact) * (NUM_BUCKETS - max_exact)).astype(np.int32)
    large = np.minimum(large, NUM_BUCKETS - 1)
    return np.where(d < max_exact, d, large).astype(np.int32)


def _bias_kernel(bucket_ref, rb_ref, sink_ref, bias_ref, sinkrow_ref):
    L = ATTN_BLOCK
    bucket = bucket_ref[...]
    key = lax.broadcasted_iota(jnp.int32, (L, L), 0)
    qry = lax.broadcasted_iota(jnp.int32, (L, L), 1)
    prev_key = key > qry
    for g in range(ATTN_KV_HEADS):
        for r in range(ATTN_GROUP):
            h = g * ATTN_GROUP + r
            cols = slice(r * L, (r + 1) * L)
            acc = jnp.zeros((L, L), F32)
            for b in range(NUM_BUCKETS):
                acc = jnp.where(bucket == b, rb_ref[b, h], acc)
            acc = acc * LOG2E
            cur = jnp.where(prev_key, -jnp.inf, acc)
            bias_ref[0, g, :L, cols] = jnp.full((L, L), -jnp.inf, F32)
            bias_ref[1, g, :L, cols] = jnp.where(prev_key, acc, -jnp.inf)
            bias_ref[0, g, L:, cols] = cur
            bias_ref[1, g, L:, cols] = cur
            sinkrow_ref[g, :, cols] = jnp.full((1, L), sink_ref[h] * LOG2E, F32)


def _bias_table(rel_bias, sinks):
    L = ATTN_BLOCK
    c = np.arange(L)[:, None]
    i = np.arange(L)[None, :]
    bucket = _t5_causal_bucket((i - c) % L)
    return pl.pallas_call(
        _bias_kernel,
        in_specs=[pl.BlockSpec(memory_space=pltpu.VMEM),
                  pl.BlockSpec(memory_space=pltpu.SMEM),
                  pl.BlockSpec(memory_space=pltpu.SMEM)],
        out_specs=[pl.BlockSpec(memory_space=pltpu.VMEM), pl.BlockSpec(memory_space=pltpu.VMEM)],
        out_shape=[jax.ShapeDtypeStruct((2, ATTN_KV_HEADS, 2 * L, ATTN_GROUP * L), F32),
                   jax.ShapeDtypeStruct((ATTN_KV_HEADS, 1, ATTN_GROUP * L), F32)],
        name="bias_table",
    )(jnp.asarray(bucket), rel_bias.astype(F32), sinks.astype(F32))


def _inproj_kernel(x_ref, mod_ref, w_ref, wgu_ref, bg_ref, gn_ref,
                   qa_ref, kva_ref, g_ref, la_ref, *, nsub):
    rs = x_ref.shape[1] // nsub
    p_kv = ATTN_Q
    p_g = p_kv + 2 * ATTN_KV
    p_go = p_g + 2 * GLA_K + GLA_V
    p_z = p_g + GLA_PACK

    def norm(s):
        x = x_ref[0, s * rs:(s + 1) * rs, :]
        return (_rms(x) * (1.0 + mod_ref[0, MOD_SCALE1]) + mod_ref[0, MOD_SHIFT1]).astype(BF16)

    def finish(s, y):
        rows = slice(s * rs, (s + 1) * rs)
        gp = _dot(y[:, p_z:].astype(BF16), wgu_ref[...]) + bg_ref[...]
        gp2 = gp * LOG2E
        la = (jnp.minimum(gp2, 0.0) - jnp.log2(1.0 + jnp.exp2(-jnp.abs(gp2)))) * (1.0 / GLA_GATE_NORM)
        la_hi = la.astype(BF16)
        la_ref[0, rows, :GLA_K] = la_hi
        la_ref[0, rows, GLA_K:] = (la - la_hi.astype(F32)).astype(BF16)
        go = y[:, p_go:p_z]
        g_ref[0, rows, 2 * GLA_K + GLA_V:] = (go * (1.0 + jnp.tanh(0.5 * go)) * (0.5 * gn_ref[...])).astype(BF16)
        g_ref[0, rows, :GLA_K] = (y[:, p_g:p_g + GLA_K] * (GLA_DK ** -0.5)).astype(BF16)
        g_ref[0, rows, GLA_K:2 * GLA_K + GLA_V] = y[:, p_g + GLA_K:p_go].astype(BF16)
        qa_ref[0, rows, :] = (y[:, :p_kv] * (ATTN_HEAD_DIM ** -0.5 * LOG2E)).astype(BF16)
        kva_ref[0, rows, :] = y[:, p_kv:p_g].astype(BF16)

    hb = norm(0)
    y_prev = None
    for s in range(nsub):
        y = _dot(hb, w_ref[...])
        if s + 1 < nsub:
            hb = norm(s + 1)
        if y_prev is not None:
            finish(s - 1, y_prev)
        y_prev = y
    finish(nsub - 1, y_prev)


def _inproj(x, mod, w, wgu, bg, gn, tm, nsub):
    bsz, t, d = x.shape
    const = lambda shape: pl.BlockSpec(shape, lambda b, i: (0,) * len(shape),
                                       pipeline_mode=pl.Buffered(1))
    tile = lambda n: pl.BlockSpec((1, tm, n), lambda b, i: (b, i, 0))
    return pl.pallas_call(
        functools.partial(_inproj_kernel, nsub=nsub),
        grid=(bsz, t // tm),
        in_specs=[tile(d), _mod_spec(d), const(w.shape), const(wgu.shape), const(bg.shape), const(gn.shape)],
        out_specs=[tile(ATTN_Q), tile(2 * ATTN_KV), tile(GLA_PACK), tile(2 * GLA_K)],
        out_shape=[jax.ShapeDtypeStruct((bsz, t, ATTN_Q), BF16),
                   jax.ShapeDtypeStruct((bsz, t, 2 * ATTN_KV), BF16),
                   jax.ShapeDtypeStruct((bsz, t, GLA_PACK), BF16),
                   jax.ShapeDtypeStruct((bsz, t, 2 * GLA_K), BF16)],
        compiler_params=pltpu.CompilerParams(
            dimension_semantics=("arbitrary", "arbitrary"), vmem_limit_bytes=VMEM_LIMIT),
        name="inproj",
    )(x, mod, w, wgu, bg, gn)


def _swa_scores(n, g, q_ref, kv_ref, kvp_ref, bias_ref, qt_cache):
    L = ATTN_BLOCK
    G = ATTN_GROUP
    dh = ATTN_HEAD_DIM
    rows = slice(n * L, (n + 1) * L)
    k_prev = kvp_ref[0, :, :ATTN_KV] if n == 0 else kv_ref[0, (n - 1) * L:n * L, :ATTN_KV]
    k2 = jnp.concatenate([k_prev, kv_ref[0, rows, :ATTN_KV]], axis=0)
    sel = jnp.where(pl.program_id(1) == 0, 0, 1) if n == 0 else 1
    if n not in qt_cache:
        qt_cache[n] = [q_ref[0, rows, p * LANES:(p + 1) * LANES].T for p in range(G)]
    qs_t = jnp.concatenate([qt[g * dh:(g + 1) * dh, :] for qt in qt_cache[n]], axis=1)
    s = _dot(k2[:, g * dh:(g + 1) * dh], qs_t) + bias_ref[sel, g]
    return jnp.maximum(s[:L], s[L:])


def _swa_softmax(g, sc, sink_ref):
    L = ATTN_BLOCK
    G = ATTN_GROUP
    key = lax.broadcasted_iota(jnp.int32, (L, G * L), 0)
    qry = lax.broadcasted_iota(jnp.int32, (L, G * L), 1) & (L - 1)
    prev_key = key > qry
    sink = sink_ref[g]
    m = jnp.maximum(jnp.max(sc, axis=0, keepdims=True), sink)
    p_ = jnp.exp2(sc - m)
    denom = jnp.sum(p_, axis=0, keepdims=True) + jnp.exp2(sink - m)
    pb = p_.astype(BF16)
    zero = jnp.zeros((), BF16)
    p2 = jnp.concatenate([jnp.where(prev_key, pb, zero), jnp.where(prev_key, zero, pb)], axis=0)
    return p2, denom


def _swa_pv(n, g, p2, denom, kv_ref, kvp_ref, o_ref):
    L = ATTN_BLOCK
    G = ATTN_GROUP
    dh = ATTN_HEAD_DIM
    rows = slice(n * L, (n + 1) * L)
    v_prev = kvp_ref[0, :, ATTN_KV:] if n == 0 else kv_ref[0, (n - 1) * L:n * L, ATTN_KV:]
    v2 = jnp.concatenate([v_prev, kv_ref[0, rows, ATTN_KV:]], axis=0)
    o = lax.dot_general(v2, p2, TN_DIMS, preferred_element_type=F32)
    o = (o[g * dh:(g + 1) * dh] * (1.0 / denom)).astype(BF16)
    for r in range(G):
        h = g * G + r
        o_ref[0, h * dh:(h + 1) * dh, rows] = o[:, r * L:(r + 1) * L]


def _gla_prep(c, g_ref, la_ref):
    C = GLA_CHUNK
    ri = lax.broadcasted_iota(jnp.int32, (C, C), 0)
    ci = lax.broadcasted_iota(jnp.int32, (C, C), 1)
    tril = (ri >= ci).astype(BF16)
    rows = slice(c * C, (c + 1) * C)
    b = _dot(tril, la_ref[0, rows, :GLA_K]) + _dot(tril, la_ref[0, rows, GLA_K:])
    b_mid = b[C // 2 - 1:C // 2, :]
    b_last = b[C - 1:C, :]
    q = g_ref[0, rows, 0:GLA_K].astype(F32)
    k = g_ref[0, rows, GLA_K:2 * GLA_K].astype(F32)
    q_t = q * jnp.exp2(b - b_mid)
    k_t = k * jnp.exp2(b_mid - b)
    q_in = (q_t * jnp.exp2(b_mid)).astype(BF16)
    k_s = (k_t * jnp.exp2(b_last - b_mid)).astype(BF16)
    return q_in, q_t.astype(BF16), k_t.astype(BF16), k_s, jnp.exp2(b_last)


def _gla_intra(prep):
    C = GLA_CHUNK
    dk = GLA_DK
    _, q_t, k_t, _, _ = prep
    ri2 = lax.broadcasted_iota(jnp.int32, (C, 2 * C), 0)
    ci2 = lax.broadcasted_iota(jnp.int32, (C, 2 * C), 1) & (C - 1)
    causal2 = ri2 >= ci2
    low_k = lax.broadcasted_iota(jnp.int32, (C, 2 * dk), 1) < dk
    zero = jnp.zeros((), BF16)
    out = []
    for p in range(GLA_PAIRS):
        ks = slice(p * 2 * dk, (p + 1) * 2 * dk)
        kt_p = k_t[:, ks]
        kbd = jnp.concatenate([jnp.where(low_k, kt_p, zero), jnp.where(low_k, zero, kt_p)], axis=0)
        a = lax.dot_general(q_t[:, ks], kbd, NT_DIMS, preferred_element_type=F32)
        out.append(jnp.where(causal2, a.astype(BF16), zero))
    return out


def _gla_out(c, prep, a, st, g_ref, o_ref):
    C = GLA_CHUNK
    dk, dv = GLA_DK, GLA_DV
    q_in, _, _, k_s, decay = prep
    low_k = lax.broadcasted_iota(jnp.int32, (C, 2 * dk), 1) < dk
    zero = jnp.zeros((), BF16)
    zeros_v = jnp.zeros((C, dv), BF16)
    rows = slice(c * C, (c + 1) * C)
    st_new = []
    for p in range(GLA_PAIRS):
        ks = slice(p * 2 * dk, (p + 1) * 2 * dk)
        vs = slice(2 * GLA_K + p * 2 * dv, 2 * GLA_K + (p + 1) * 2 * dv)
        gs = slice(2 * GLA_K + GLA_V + p * 2 * dv, 2 * GLA_K + GLA_V + (p + 1) * 2 * dv)
        v_0 = g_ref[0, rows, vs.start:vs.start + dv]
        v_1 = g_ref[0, rows, vs.start + dv:vs.stop]
        vbd = jnp.concatenate([jnp.concatenate([v_0, zeros_v], axis=1),
                               jnp.concatenate([zeros_v, v_1], axis=1)], axis=0)
        o_p = _dot(a[p], vbd) + lax.dot_general(q_in[:, ks], st[p].astype(BF16), NT_DIMS,
                                                preferred_element_type=F32)
        ks_p = k_s[:, ks]
        dst = jnp.concatenate(
            [lax.dot_general(v_0, jnp.where(low_k, ks_p, zero), TN_DIMS, preferred_element_type=F32),
             lax.dot_general(v_1, jnp.where(low_k, zero, ks_p), TN_DIMS, preferred_element_type=F32)],
            axis=0)
        st_new.append(st[p] * decay[:, ks] + dst)
        o_n = jnp.concatenate([_rms(o_p[:, :dv]), _rms(o_p[:, dv:])], axis=1)
        o_ref[0, rows, p * 2 * dv:(p + 1) * 2 * dv] = (o_n * g_ref[0, rows, gs].astype(F32)).astype(BF16)
    return st_new


def _mixer_kernel(q_ref, kv_ref, kvp_ref, bias_ref, sink_ref, g_ref, la_ref,
                  attn_ref, gla_ref, st_ref, *, nblk):
    @pl.when(pl.program_id(1) == 0)
    def _():
        st_ref[...] = jnp.zeros_like(st_ref)

    st = [st_ref[p] for p in range(GLA_PAIRS)]
    units = [(n, g) for n in range(nblk) for g in range(ATTN_KV_HEADS)]
    nu = len(units)
    sc, pd, prep, intra, qt_cache = {}, {}, {}, {}, {}

    def emit(kind, i):
        nonlocal st
        if kind == "prep" and 0 <= i < nblk:
            prep[i] = _gla_prep(i, g_ref, la_ref)
        elif kind == "intra" and 0 <= i < nblk:
            intra[i] = _gla_intra(prep[i])
        elif kind == "out" and 0 <= i < nblk:
            st = _gla_out(i, prep.pop(i), intra.pop(i), st, g_ref, gla_ref)
        elif kind == "scores" and 0 <= i < nu:
            sc[i] = _swa_scores(*units[i], q_ref, kv_ref, kvp_ref, bias_ref, qt_cache)
        elif kind == "softmax" and 0 <= i < nu:
            pd[i] = _swa_softmax(units[i][1], sc.pop(i), sink_ref)
        elif kind == "pv" and 0 <= i < nu:
            _swa_pv(*units[i], *pd.pop(i), kv_ref, kvp_ref, attn_ref)

    for slot in range(nu + 2 * ATTN_KV_HEADS):
        if slot % ATTN_KV_HEADS == 0:
            c = slot // ATTN_KV_HEADS
            emit("out", c - 2)
            emit("intra", c - 1)
            emit("prep", c)
        emit("pv", slot - 2)
        emit("softmax", slot - 1)
        emit("scores", slot)
    for p in range(GLA_PAIRS):
        st_ref[p] = st[p]


def _mixer(qa, kva, bias, sinkrow, gpack, la, tq):
    assert ATTN_BLOCK == GLA_CHUNK == WINDOW
    bsz, t, _ = qa.shape
    assert t % tq == 0 and tq % ATTN_BLOCK == 0
    nblk = tq // ATTN_BLOCK
    whole = lambda a: pl.BlockSpec(a.shape, lambda b, i: (0,) * a.ndim, pipeline_mode=pl.Buffered(1))
    tile = lambda n: pl.BlockSpec((1, tq, n), lambda b, i: (b, i, 0))
    return pl.pallas_call(
        functools.partial(_mixer_kernel, nblk=nblk),
        grid=(bsz, t // tq),
        in_specs=[tile(ATTN_Q), tile(2 * ATTN_KV),
                  pl.BlockSpec((1, ATTN_BLOCK, 2 * ATTN_KV),
                               lambda b, i: (b, jnp.maximum(i * nblk - 1, 0), 0)),
                  whole(bias), whole(sinkrow), tile(GLA_PACK), tile(2 * GLA_K)],
        out_specs=[pl.BlockSpec((1, ATTN_Q, tq), lambda b, i: (b, 0, i)), tile(GLA_V)],
        out_shape=[jax.ShapeDtypeStruct((bsz, ATTN_Q, t), BF16),
                   jax.ShapeDtypeStruct((bsz, t, GLA_V), BF16)],
        scratch_shapes=[pltpu.VMEM((GLA_PAIRS, 2 * GLA_DV, 2 * GLA_DK), F32)],
        compiler_params=pltpu.CompilerParams(
            dimension_semantics=("arbitrary", "arbitrary"), vmem_limit_bytes=VMEM_LIMIT),
        name="mixer",
    )(qa, kva, kva, bias, sinkrow, gpack, la)


def _mlp_kernel(x_ref, attn_ref, gla_ref, mod_ref, wo_ref, w1_ref, w2_ref,
                fg_ref, o_ref, u_ref, *, fchunk, final, nsub):
    rs = x_ref.shape[1] // nsub
    nf = D_FF // fchunk

    def pre(s):
        rows = slice(s * rs, (s + 1) * rs)
        mix = (lax.dot_general(attn_ref[0, :, rows], wo_ref[:ATTN_Q, :], TN_DIMS, preferred_element_type=F32)
               + _dot(gla_ref[0, rows, :], wo_ref[ATTN_Q:, :]))
        x1 = x_ref[0, rows, :] + mod_ref[0, MOD_GATE1] * mix
        return x1, (_rms(x1) * (1.0 + mod_ref[0, MOD_SCALE2]) + mod_ref[0, MOD_SHIFT2]).astype(BF16)

    def up(s, hb, c):
        cols = slice(c * fchunk, (c + 1) * fchunk)
        u = jnp.maximum(_dot(hb, w1_ref[:, cols]), 0.0)
        u_ref[s, :, cols] = (u * u).astype(BF16)

    def post(s, x1, y):
        x2 = x1 + mod_ref[0, MOD_GATE2] * y
        o_ref[0, s * rs:(s + 1) * rs, :] = _rms(x2) * fg_ref[...] if final else x2

    x1, hb = pre(0)
    pending = None
    for s in range(nsub):
        for c in range(nf // 2):
            up(s, hb, c)
        if s + 1 < nsub:
            nxt = pre(s + 1)
        for c in range(nf // 2, nf):
            up(s, hb, c)
        if pending is not None:
            post(*pending)
        pending = (s, x1, _dot(u_ref[s], w2_ref[...]))
        if s + 1 < nsub:
            x1, hb = nxt
    post(*pending)


def _mlp(x, attn, gla, mod, wo, w1, w2, fg, tm, final, nsub):
    bsz, t, d = x.shape
    const = lambda shape: pl.BlockSpec(shape, lambda b, i: (0,) * len(shape),
                                       pipeline_mode=pl.Buffered(1))
    tile = lambda n: pl.BlockSpec((1, tm, n), lambda b, i: (b, i, 0))
    return pl.pallas_call(
        functools.partial(_mlp_kernel, fchunk=MLP_FF_CHUNK, final=final, nsub=nsub),
        grid=(bsz, t // tm),
        in_specs=[tile(d), pl.BlockSpec((1, ATTN_Q, tm), lambda b, i: (b, 0, i)), tile(GLA_V),
                  _mod_spec(d),
                  const(wo.shape), const(w1.shape), const(w2.shape), const(fg.shape)],
        out_specs=tile(d),
        out_shape=jax.ShapeDtypeStruct((bsz, t, d), F32),
        scratch_shapes=[pltpu.VMEM((nsub, tm // nsub, D_FF), BF16)],
        compiler_params=pltpu.CompilerParams(
            dimension_semantics=("arbitrary", "arbitrary"), vmem_limit_bytes=VMEM_LIMIT),
        name="mlp",
    )(x, attn, gla, mod, wo, w1, w2, fg)


def kernel(x, c, w_ada, b_ada, w_in, w_gate_up, b_gate, gla_norm_g, attn_sinks, rel_bias,
           w_out, w_mlp_in, w_mlp_out, final_norm_g):
    bsz, t, d = x.shape
    depth = w_ada.shape[0]
    tile_in, tile_mix, tile_mlp = (min(v, t) for v in (INPROJ_TILE, MIXER_TILE, MLP_TILE))
    p_kv = ATTN_Q
    p_g = ATTN_Q + 2 * ATTN_KV
    p_z = p_g + GLA_PACK
    for l in range(depth):
        mod = _ada(c, w_ada[l], b_ada[l]).reshape(bsz, N_MOD, 1, d)
        bias, sinkrow = _bias_table(rel_bias, attn_sinks[l])
        w = w_in[l]
        win = jnp.concatenate(
            [_pair_heads(w[:, :p_kv]), w[:, p_kv:p_z],
             jnp.pad(w[:, p_z:], ((0, 0), (0, Z_PAD - GLA_GATE_RANK)))], axis=1).astype(BF16)
        wgu = jnp.pad(w_gate_up[l], ((0, Z_PAD - GLA_GATE_RANK), (0, 0))).astype(BF16)
        qa, kva, gpack, la = _inproj(x, mod, win, wgu, b_gate[l].reshape(1, GLA_K),
                                     jnp.tile(gla_norm_g[l], GLA_HEADS).reshape(1, GLA_V),
                                     tile_in, max(1, tile_in // INPROJ_SUB_TILE))
        attn, gla = _mixer(qa, kva, bias, sinkrow, gpack, la, tile_mix)
        x = _mlp(x, attn, gla, mod,
                 w_out[l].astype(BF16), w_mlp_in[l].astype(BF16), w_mlp_out[l].astype(BF16),
                 final_norm_g.reshape(1, d), tile_mlp, final=(l == depth - 1),
                 nsub=max(1, tile_mlp // MLP_SUB_TILE))
    return x
```

```python
import functools

import numpy as np
import jax
import jax.numpy as jnp
from jax import lax
from jax.experimental import pallas as pl
from jax.experimental.pallas import tpu as pltpu

F32 = jnp.float32
BF16 = jnp.bfloat16

D_MODEL = 1024
ATTN_HEADS = 8
ATTN_KV_HEADS = 2
ATTN_GROUP = ATTN_HEADS // ATTN_KV_HEADS
ATTN_HEAD_DIM = 64
WINDOW = 128
ATTN_BLOCK = 128
NUM_BUCKETS = 32
MAX_DISTANCE = 128
GLA_HEADS = 4
GLA_PAIRS = GLA_HEADS // 2
GLA_DK = 64
GLA_DV = 128
GLA_GATE_RANK = 16
GLA_GATE_NORM = 16.0
GLA_CHUNK = 128
D_FF = 4 * D_MODEL
EPS = 1e-6
N_MOD = 6

ATTN_Q = ATTN_HEADS * ATTN_HEAD_DIM
ATTN_KV = ATTN_KV_HEADS * ATTN_HEAD_DIM
GLA_K = GLA_HEADS * GLA_DK
GLA_V = GLA_HEADS * GLA_DV
GLA_PACK = 2 * GLA_K + 2 * GLA_V
LANES = 128
Z_PAD = LANES
LOG2E = 1.4426950408889634

VMEM_LIMIT = 56 * 1024 * 1024
INPROJ_TILE, INPROJ_SUB_TILE = 1024, 128
MIXER_TILE = 2048
MLP_TILE, MLP_SUB_TILE = 1024, 512
MLP_FF_CHUNK = 1024
ADA_COLS_PER_STEP = 1536

NT_DIMS = (((1,), (1,)), ((), ()))
TN_DIMS = (((0,), (0,)), ((), ()))

MOD_SHIFT1, MOD_SCALE1, MOD_GATE1, MOD_SHIFT2, MOD_SCALE2, MOD_GATE2 = range(N_MOD)


def _dot(a, b):
    return jnp.dot(a, b, preferred_element_type=F32)


def _pair_heads(wq):
    d = wq.shape[0]
    return wq.reshape(d, ATTN_KV_HEADS, ATTN_GROUP, ATTN_HEAD_DIM).transpose(0, 2, 1, 3).reshape(d, ATTN_Q)


def _mod_spec(d):
    return pl.BlockSpec((1, N_MOD, 1, d), lambda b, i: (b, 0, 0, 0))


def _rms(x):
    return x * lax.rsqrt(jnp.mean(x * x, axis=-1, keepdims=True) + EPS)


def _ada_kernel(c_ref, w_ref, b_ref, o_ref):
    c = c_ref[...]
    cond = c * jax.nn.sigmoid(c)
    c_hi = cond.astype(BF16)
    c_lo = (cond - c_hi.astype(F32)).astype(BF16)
    w = w_ref[...]
    w_hi = w.astype(BF16)
    w_lo = (w - w_hi.astype(F32)).astype(BF16)
    o_ref[...] = _dot(c_hi, w_hi) + (_dot(c_hi, w_lo) + _dot(c_lo, w_hi)) + b_ref[...]


def _ada(c, w_ada, b_ada):
    bsz, d = c.shape
    n = w_ada.shape[1]
    tn = ADA_COLS_PER_STEP
    assert n % tn == 0
    return pl.pallas_call(
        _ada_kernel,
        grid=(n // tn,),
        in_specs=[pl.BlockSpec((bsz, d), lambda j: (0, 0)),
                  pl.BlockSpec((d, tn), lambda j: (0, j)),
                  pl.BlockSpec((1, tn), lambda j: (0, j))],
        out_specs=pl.BlockSpec((bsz, tn), lambda j: (0, j)),
        out_shape=jax.ShapeDtypeStruct((bsz, n), F32),
        name="ada",
    )(c, w_ada, b_ada.reshape(1, n))


def _t5_causal_bucket(dist):
    max_exact = NUM_BUCKETS // 2
    d = np.maximum(dist, 0)
    large = max_exact + (np.log(np.maximum(d, max_exact) / max_exact)
                         / np.log(MAX_DISTANCE / max_exact) * (NUM_BUCKETS - max_exact)).astype(np.int32)
    large = np.minimum(large, NUM_BUCKETS - 1)
    return np.where(d < max_exact, d, large).astype(np.int32)


def _bias_kernel(bucket_ref, rb_ref, sink_ref, bias_ref, sinkrow_ref):
    L = ATTN_BLOCK
    bucket = bucket_ref[...]
    key = lax.broadcasted_iota(jnp.int32, (L, L), 0)
    qry = lax.broadcasted_iota(jnp.int32, (L, L), 1)
    prev_key = key > qry
    for g in range(ATTN_KV_HEADS):
        for r in range(ATTN_GROUP):
            h = g * ATTN_GROUP + r
            cols = slice(r * L, (r + 1) * L)
            acc = jnp.zeros((L, L), F32)
            for b in range(NUM_BUCKETS):
                acc = jnp.where(bucket == b, rb_ref[b, h], acc)
            acc = acc * LOG2E
            cur = jnp.where(prev_key, -jnp.inf, acc)
            bias_ref[0, g, :L, cols] = jnp.full((L, L), -jnp.inf, F32)
            bias_ref[1, g, :L, cols] = jnp.where(prev_key, acc, -jnp.inf)
            bias_ref[0, g, L:, cols] = cur
            bias_ref[1, g, L:, cols] = cur
            sinkrow_ref[g, :, cols] = jnp.full((1, L), sink_ref[h] * LOG2E, F32)


def _bias_table(rel_bias, sinks):
    L = ATTN_BLOCK
    c = np.arange(L)[:, None]
    i = np.arange(L)[None, :]
    bucket = _t5_causal_bucket((i - c) % L)
    return pl.pallas_call(
        _bias_kernel,
        in_specs=[pl.BlockSpec(memory_space=pltpu.VMEM),
                  pl.BlockSpec(memory_space=pltpu.SMEM),
                  pl.BlockSpec(memory_space=pltpu.SMEM)],
        out_specs=[pl.BlockSpec(memory_space=pltpu.VMEM), pl.BlockSpec(memory_space=pltpu.VMEM)],
        out_shape=[jax.ShapeDtypeStruct((2, ATTN_KV_HEADS, 2 * L, ATTN_GROUP * L), F32),
                   jax.ShapeDtypeStruct((ATTN_KV_HEADS, 1, ATTN_GROUP * L), F32)],
        name="bias_table",
    )(jnp.asarray(bucket), rel_bias.astype(F32), sinks.astype(F32))


def _inproj_kernel(x_ref, mod_ref, w_ref, wgu_ref, bg_ref, gn_ref,
                   qa_ref, kva_ref, g_ref, la_ref, *, nsub):
    rs = x_ref.shape[1] // nsub
    p_kv = ATTN_Q
    p_g = p_kv + 2 * ATTN_KV
    p_go = p_g + 2 * GLA_K + GLA_V
    p_z = p_g + GLA_PACK

    def norm(s):
        x = x_ref[0, s * rs:(s + 1) * rs, :]
        return (_rms(x) * (1.0 + mod_ref[0, MOD_SCALE1]) + mod_ref[0, MOD_SHIFT1]).astype(BF16)

    def finish(s, y):
        rows = slice(s * rs, (s + 1) * rs)
        gp = _dot(y[:, p_z:].astype(BF16), wgu_ref[...]) + bg_ref[...]
        gp2 = gp * LOG2E
        la = (jnp.minimum(gp2, 0.0) - jnp.log2(1.0 + jnp.exp2(-jnp.abs(gp2)))) * (1.0 / GLA_GATE_NORM)
        la_hi = la.astype(BF16)
        la_ref[0, rows, :GLA_K] = la_hi
        la_ref[0, rows, GLA_K:] = (la - la_hi.astype(F32)).astype(BF16)
        go = y[:, p_go:p_z]
        g_ref[0, rows, 2 * GLA_K + GLA_V:] = (go * (1.0 + jnp.tanh(0.5 * go)) * (0.5 * gn_ref[...])).astype(BF16)
        g_ref[0, rows, :GLA_K] = (y[:, p_g:p_g + GLA_K] * (GLA_DK ** -0.5)).astype(BF16)
        g_ref[0, rows, GLA_K:2 * GLA_K + GLA_V] = y[:, p_g + GLA_K:p_go].astype(BF16)
        qa_ref[0, rows, :] = (y[:, :p_kv] * (ATTN_HEAD_DIM ** -0.5 * LOG2E)).astype(BF16)
        kva_ref[0, rows, :] = y[:, p_kv:p_g].astype(BF16)

    hb = norm(0)
    y_prev = None
    for s in range(nsub):
        y = _dot(hb, w_ref[...])
        if s + 1 < nsub:
            hb = norm(s + 1)
        if y_prev is not None:
            finish(s - 1, y_prev)
        y_prev = y
    finish(nsub - 1, y_prev)


def _inproj(x, mod, w, wgu, bg, gn, tm, nsub):
    bsz, t, d = x.shape
    const = lambda shape: pl.BlockSpec(shape, lambda b, i: (0,) * len(shape),
                                       pipeline_mode=pl.Buffered(1))
    tile = lambda n: pl.BlockSpec((1, tm, n), lambda b, i: (b, i, 0))
    return pl.pallas_call(
        functools.partial(_inproj_kernel, nsub=nsub),
        grid=(bsz, t // tm),
        in_specs=[tile(d), _mod_spec(d), const(w.shape), const(wgu.shape), const(bg.shape), const(gn.shape)],
        out_specs=[tile(ATTN_Q), tile(2 * ATTN_KV), tile(GLA_PACK), tile(2 * GLA_K)],
        out_shape=[jax.ShapeDtypeStruct((bsz, t, ATTN_Q), BF16),
                   jax.ShapeDtypeStruct((bsz, t, 2 * ATTN_KV), BF16),
                   jax.ShapeDtypeStruct((bsz, t, GLA_PACK), BF16),
                   jax.ShapeDtypeStruct((bsz, t, 2 * GLA_K), BF16)],
        compiler_params=pltpu.CompilerParams(
            dimension_semantics=("arbitrary", "arbitrary"), vmem_limit_bytes=VMEM_LIMIT),
        name="inproj",
    )(x, mod, w, wgu, bg, gn)


def _swa_scores(n, g, q_ref, kv_ref, kvp_ref, bias_ref, qt_cache):
    L = ATTN_BLOCK
    G = ATTN_GROUP
    dh = ATTN_HEAD_DIM
    rows = slice(n * L, (n + 1) * L)
    k_prev = kvp_ref[0, :, :ATTN_KV] if n == 0 else kv_ref[0, (n - 1) * L:n * L, :ATTN_KV]
    k2 = jnp.concatenate([k_prev, kv_ref[0, rows, :ATTN_KV]], axis=0)
    sel = jnp.where(pl.program_id(1) == 0, 0, 1) if n == 0 else 1
    if n not in qt_cache:
        qt_cache[n] = [q_ref[0, rows, p * LANES:(p + 1) * LANES].T for p in range(G)]
    qs_t = jnp.concatenate([qt[g * dh:(g + 1) * dh, :] for qt in qt_cache[n]], axis=1)
    s = _dot(k2[:, g * dh:(g + 1) * dh], qs_t) + bias_ref[sel, g]
    return jnp.maximum(s[:L], s[L:])


def _swa_softmax(g, sc, sink_ref):
    L = ATTN_BLOCK
    G = ATTN_GROUP
    key = lax.broadcasted_iota(jnp.int32, (L, G * L), 0)
    qry = lax.broadcasted_iota(jnp.int32, (L, G * L), 1) & (L - 1)
    prev_key = key > qry
    sink = sink_ref[g]
    m = jnp.maximum(jnp.max(sc, axis=0, keepdims=True), sink)
    p_ = jnp.exp2(sc - m)
    denom = jnp.sum(p_, axis=0, keepdims=True) + jnp.exp2(sink - m)
    pb = p_.astype(BF16)
    zero = jnp.zeros((), BF16)
    p2 = jnp.concatenate([jnp.where(prev_key, pb, zero), jnp.where(prev_key, zero, pb)], axis=0)
    return p2, denom


def _swa_pv(n, g, p2, denom, kv_ref, kvp_ref, o_ref):
    L = ATTN_BLOCK
    G = ATTN_GROUP
    dh = ATTN_HEAD_DIM
    rows = slice(n * L, (n + 1) * L)
    v_prev = kvp_ref[0, :, ATTN_KV:] if n == 0 else kv_ref[0, (n - 1) * L:n * L, ATTN_KV:]
    v2 = jnp.concatenate([v_prev, kv_ref[0, rows, ATTN_KV:]], axis=0)
    o = lax.dot_general(v2, p2, TN_DIMS, preferred_element_type=F32)
    o = (o[g * dh:(g + 1) * dh] * (1.0 / denom)).astype(BF16)
    for r in range(G):
        h = g * G + r
        o_ref[0, h * dh:(h + 1) * dh, rows] = o[:, r * L:(r + 1) * L]


def _gla_prep(c, g_ref, la_ref):
    C = GLA_CHUNK
    ri = lax.broadcasted_iota(jnp.int32, (C, C), 0)
    ci = lax.broadcasted_iota(jnp.int32, (C, C), 1)
    tril = (ri >= ci).astype(BF16)
    rows = slice(c * C, (c + 1) * C)
    b = _dot(tril, la_ref[0, rows, :GLA_K]) + _dot(tril, la_ref[0, rows, GLA_K:])
    b_mid = b[C // 2 - 1:C // 2, :]
    b_last = b[C - 1:C, :]
    q = g_ref[0, rows, 0:GLA_K].astype(F32)
    k = g_ref[0, rows, GLA_K:2 * GLA_K].astype(F32)
    q_t = q * jnp.exp2(b - b_mid)
    k_t = k * jnp.exp2(b_mid - b)
    q_in = (q_t * jnp.exp2(b_mid)).astype(BF16)
    k_s = (k_t * jnp.exp2(b_last - b_mid)).astype(BF16)
    return q_in, q_t.astype(BF16), k_t.astype(BF16), k_s, jnp.exp2(b_last)


def _gla_intra(prep):
    C = GLA_CHUNK
    dk = GLA_DK
    _, q_t, k_t, _, _ = prep
    ri2 = lax.broadcasted_iota(jnp.int32, (C, 2 * C), 0)
    ci2 = lax.broadcasted_iota(jnp.int32, (C, 2 * C), 1) & (C - 1)
    causal2 = ri2 >= ci2
    low_k = lax.broadcasted_iota(jnp.int32, (C, 2 * dk), 1) < dk
    zero = jnp.zeros((), BF16)
    out = []
    for p in range(GLA_PAIRS):
        ks = slice(p * 2 * dk, (p + 1) * 2 * dk)
        kt_p = k_t[:, ks]
        kbd = jnp.concatenate([jnp.where(low_k, kt_p, zero), jnp.where(low_k, zero, kt_p)], axis=0)
        a = lax.dot_general(q_t[:, ks], kbd, NT_DIMS, preferred_element_type=F32)
        out.append(jnp.where(causal2, a.astype(BF16), zero))
    return out


def _gla_out(c, prep, a, st, g_ref, o_ref):
    C = GLA_CHUNK
    dk, dv = GLA_DK, GLA_DV
    q_in, _, _, k_s, decay = prep
    low_k = lax.broadcasted_iota(jnp.int32, (C, 2 * dk), 1) < dk
    zero = jnp.zeros((), BF16)
    zeros_v = jnp.zeros((C, dv), BF16)
    rows = slice(c * C, (c + 1) * C)
    st_new = []
    for p in range(GLA_PAIRS):
        ks = slice(p * 2 * dk, (p + 1) * 2 * dk)
        vs = slice(2 * GLA_K + p * 2 * dv, 2 * GLA_K + (p + 1) * 2 * dv)
        gs = slice(2 * GLA_K + GLA_V + p * 2 * dv, 2 * GLA_K + GLA_V + (p + 1) * 2 * dv)
        v_0 = g_ref[0, rows, vs.start:vs.start + dv]
        v_1 = g_ref[0, rows, vs.start + dv:vs.stop]
        vbd = jnp.concatenate([jnp.concatenate([v_0, zeros_v], axis=1),
                               jnp.concatenate([zeros_v, v_1], axis=1)], axis=0)
        o_p = _dot(a[p], vbd) + lax.dot_general(q_in[:, ks], st[p].astype(BF16), NT_DIMS,
                                                preferred_element_type=F32)
        ks_p = k_s[:, ks]
        dst = jnp.concatenate(
            [lax.dot_general(v_0, jnp.where(low_k, ks_p, zero), TN_DIMS, preferred_element_type=F32),
             lax.dot_general(v_1, jnp.where(low_k, zero, ks_p), TN_DIMS, preferred_element_type=F32)],
            axis=0)
        st_new.append(st[p] * decay[:, ks] + dst)
        o_n = jnp.concatenate([_rms(o_p[:, :dv]), _rms(o_p[:, dv:])], axis=1)
        o_ref[0, rows, p * 2 * dv:(p + 1) * 2 * dv] = (o_n * g_ref[0, rows, gs].astype(F32)).astype(BF16)
    return st_new


def _mixer_kernel(q_ref, kv_ref, kvp_ref, bias_ref, sink_ref, g_ref, la_ref,
                  attn_ref, gla_ref, st_ref, *, nblk):
    @pl.when(pl.program_id(1) == 0)
    def _():
        st_ref[...] = jnp.zeros_like(st_ref)

    st = [st_ref[p] for p in range(GLA_PAIRS)]
    units = [(n, g) for n in range(nblk) for g in range(ATTN_KV_HEADS)]
    nu = len(units)
    sc, pd, prep, intra, qt_cache = {}, {}, {}, {}, {}

    def emit(kind, i):
        nonlocal st
        if kind == "prep" and 0 <= i < nblk:
            prep[i] = _gla_prep(i, g_ref, la_ref)
        elif kind == "intra" and 0 <= i < nblk:
            intra[i] = _gla_intra(prep[i])
        elif kind == "out" and 0 <= i < nblk:
            st = _gla_out(i, prep.pop(i), intra.pop(i), st, g_ref, gla_ref)
        elif kind == "scores" and 0 <= i < nu:
            sc[i] = _swa_scores(*units[i], q_ref, kv_ref, kvp_ref, bias_ref, qt_cache)
        elif kind == "softmax" and 0 <= i < nu:
            pd[i] = _swa_softmax(units[i][1], sc.pop(i), sink_ref)
        elif kind == "pv" and 0 <= i < nu:
            _swa_pv(*units[i], *pd.pop(i), kv_ref, kvp_ref, attn_ref)

    for slot in range(nu + 2 * ATTN_KV_HEADS):
        if slot % ATTN_KV_HEADS == 0:
            c = slot // ATTN_KV_HEADS
            emit("out", c - 2)
            emit("intra", c - 1)
            emit("prep", c)
        emit("pv", slot - 2 * ATTN_KV_HEADS)
        emit("softmax", slot - ATTN_KV_HEADS)
        emit("scores", slot)
    for p in range(GLA_PAIRS):
        st_ref[p] = st[p]


def _mixer(qa, kva, bias, sinkrow, gpack, la, tq):
    assert ATTN_BLOCK == GLA_CHUNK == WINDOW
    bsz, t, _ = qa.shape
    assert t % tq == 0 and tq % ATTN_BLOCK == 0
    nblk = tq // ATTN_BLOCK
    whole = lambda a: pl.BlockSpec(a.shape, lambda b, i: (0,) * a.ndim, pipeline_mode=pl.Buffered(1))
    tile = lambda n: pl.BlockSpec((1, tq, n), lambda b, i: (b, i, 0))
    return pl.pallas_call(
        functools.partial(_mixer_kernel, nblk=nblk),
        grid=(bsz, t // tq),
        in_specs=[tile(ATTN_Q), tile(2 * ATTN_KV),
                  pl.BlockSpec((1, ATTN_BLOCK, 2 * ATTN_KV),
                               lambda b, i: (b, jnp.maximum(i * nblk - 1, 0), 0)),
                  whole(bias), whole(sinkrow), tile(GLA_PACK), tile(2 * GLA_K)],
        out_specs=[pl.BlockSpec((1, ATTN_Q, tq), lambda b, i: (b, 0, i)), tile(GLA_V)],
        out_shape=[jax.ShapeDtypeStruct((bsz, ATTN_Q, t), BF16),
                   jax.ShapeDtypeStruct((bsz, t, GLA_V), BF16)],
        scratch_shapes=[pltpu.VMEM((GLA_PAIRS, 2 * GLA_DV, 2 * GLA_DK), F32)],
        compiler_params=pltpu.CompilerParams(
            dimension_semantics=("arbitrary", "arbitrary"), vmem_limit_bytes=VMEM_LIMIT),
        name="mixer",
    )(qa, kva, kva, bias, sinkrow, gpack, la)


def _mlp_kernel(x_ref, attn_ref, gla_ref, mod_ref, wo_ref, w1_ref, w2_ref,
                fg_ref, o_ref, u_ref, *, fchunk, final, nsub):
    rs = x_ref.shape[1] // nsub
    nf = D_FF // fchunk

    def pre(s):
        rows = slice(s * rs, (s + 1) * rs)
        mix = (lax.dot_general(attn_ref[0, :, rows], wo_ref[:ATTN_Q, :], TN_DIMS, preferred_element_type=F32)
               + _dot(gla_ref[0, rows, :], wo_ref[ATTN_Q:, :]))
        x1 = x_ref[0, rows, :] + mod_ref[0, MOD_GATE1] * mix
        return x1, (_rms(x1) * (1.0 + mod_ref[0, MOD_SCALE2]) + mod_ref[0, MOD_SHIFT2]).astype(BF16)

    def up(s, hb, c):
        cols = slice(c * fchunk, (c + 1) * fchunk)
        u = jnp.maximum(_dot(hb, w1_ref[:, cols]), 0.0)
        u_ref[s, :, cols] = (u * u).astype(BF16)

    def post(s, x1, y):
        x2 = x1 + mod_ref[0, MOD_GATE2] * y
        o_ref[0, s * rs:(s + 1) * rs, :] = _rms(x2) * fg_ref[...] if final else x2

    x1, hb = pre(0)
    pending = None
    for s in range(nsub):
        for c in range(nf // 2):
            up(s, hb, c)
        if s + 1 < nsub:
            nxt = pre(s + 1)
        for c in range(nf // 2, nf):
            up(s, hb, c)
        if pending is not None:
            post(*pending)
        pending = (s, x1, _dot(u_ref[s], w2_ref[...]))
        if s + 1 < nsub:
            x1, hb = nxt
    post(*pending)


def _mlp(x, attn, gla, mod, wo, w1, w2, fg, tm, final, nsub):
    bsz, t, d = x.shape
    const = lambda shape: pl.BlockSpec(shape, lambda b, i: (0,) * len(shape),
                                       pipeline_mode=pl.Buffered(1))
    tile = lambda n: pl.BlockSpec((1, tm, n), lambda b, i: (b, i, 0))
    return pl.pallas_call(
        functools.partial(_mlp_kernel, fchunk=MLP_FF_CHUNK, final=final, nsub=nsub),
        grid=(bsz, t // tm),
        in_specs=[tile(d), pl.BlockSpec((1, ATTN_Q, tm), lambda b, i: (b, 0, i)), tile(GLA_V),
                  _mod_spec(d),
                  const(wo.shape), const(w1.shape), const(w2.shape), const(fg.shape)],
        out_specs=tile(d),
        out_shape=jax.ShapeDtypeStruct((bsz, t, d), F32),
        scratch_shapes=[pltpu.VMEM((nsub, tm // nsub, D_FF), BF16)],
        compiler_params=pltpu.CompilerParams(
            dimension_semantics=("arbitrary", "arbitrary"), vmem_limit_bytes=VMEM_LIMIT),
        name="mlp",
    )(x, attn, gla, mod, wo, w1, w2, fg)


def kernel(x, c, w_ada, b_ada, w_in, w_gate_up, b_gate, gla_norm_g, attn_sinks, rel_bias,
           w_out, w_mlp_in, w_mlp_out, final_norm_g):
    bsz, t, d = x.shape
    depth = w_ada.shape[0]
    tile_in, tile_mix, tile_mlp = (min(v, t) for v in (INPROJ_TILE, MIXER_TILE, MLP_TILE))
    p_kv = ATTN_Q
    p_g = ATTN_Q + 2 * ATTN_KV
    p_z = p_g + GLA_PACK
    for l in range(depth):
        mod = _ada(c, w_ada[l], b_ada[l]).reshape(bsz, N_MOD, 1, d)
        bias, sinkrow = _bias_table(rel_bias, attn_sinks[l])
        w = w_in[l]
        win = jnp.concatenate(
            [_pair_heads(w[:, :p_kv]), w[:, p_kv:p_z],
             jnp.pad(w[:, p_z:], ((0, 0), (0, Z_PAD - GLA_GATE_RANK)))], axis=1).astype(BF16)
        wgu = jnp.pad(w_gate_up[l], ((0, Z_PAD - GLA_GATE_RANK), (0, 0))).astype(BF16)
        qa, kva, gpack, la = _inproj(x, mod, win, wgu, b_gate[l].reshape(1, GLA_K),
                                     jnp.tile(gla_norm_g[l], GLA_HEADS).reshape(1, GLA_V),
                                     tile_in, max(1, tile_in // INPROJ_SUB_TILE))
        attn, gla = _mixer(qa, kva, bias, sinkrow, gpack, la, tile_mix)
        x = _mlp(x, attn, gla, mod,
                 w_out[l].astype(BF16), w_mlp_in[l].astype(BF16), w_mlp_out[l].astype(BF16),
                 final_norm_g.reshape(1, d), tile_mlp, final=(l == depth - 1),
                 nsub=max(1, tile_mlp // MLP_SUB_TILE))
    return x
```

```python
import functools

import numpy as np
import jax
import jax.numpy as jnp
from jax import lax
from jax.experimental import pallas as pl
from jax.experimental.pallas import tpu as pltpu

F32 = jnp.float32
BF16 = jnp.bfloat16

D_MODEL = 1024
ATTN_HEADS = 8
ATTN_KV_HEADS = 2
ATTN_GROUP = ATTN_HEADS // ATTN_KV_HEADS
ATTN_HEAD_DIM = 64
WINDOW = 128
ATTN_BLOCK = 128
NUM_BUCKETS = 32
MAX_DISTANCE = 128
GLA_HEADS = 4
GLA_PAIRS = GLA_HEADS // 2
GLA_DK = 64
GLA_DV = 128
GLA_GATE_RANK = 16
GLA_GATE_NORM = 16.0
GLA_CHUNK = 128
D_FF = 4 * D_MODEL
EPS = 1e-6
N_MOD = 6

ATTN_Q = ATTN_HEADS * ATTN_HEAD_DIM
ATTN_KV = ATTN_KV_HEADS * ATTN_HEAD_DIM
GLA_K = GLA_HEADS * GLA_DK
GLA_V = GLA_HEADS * GLA_DV
GLA_PACK = 2 * GLA_K + 2 * GLA_V
LANES = 128
Z_PAD = LANES
LOG2E = 1.4426950408889634

VMEM_LIMIT = 56 * 1024 * 1024
INPROJ_TILE, INPROJ_SUB_TILE = 1024, 128
MIXER_TILE = 2048
MLP_TILE, MLP_SUB_TILE = 1024, 512
MLP_FF_CHUNK = 1024
ADA_COLS_PER_STEP = 1536

NT_DIMS = (((1,), (1,)), ((), ()))
TN_DIMS = (((0,), (0,)), ((), ()))

MOD_SHIFT1, MOD_SCALE1, MOD_GATE1, MOD_SHIFT2, MOD_SCALE2, MOD_GATE2 = range(N_MOD)


def _dot(a, b):
    return jnp.dot(a, b, preferred_element_type=F32)


def _mod_spec(d):
    return pl.BlockSpec((1, N_MOD, 1, d), lambda b, i: (b, 0, 0, 0))


def _rms(x):
    return x * lax.rsqrt(jnp.mean(x * x, axis=-1, keepdims=True) + EPS)


def _ada_kernel(c_ref, w_ref, b_ref, o_ref):
    c = c_ref[...]
    cond = c * jax.nn.sigmoid(c)
    c_hi = cond.astype(BF16)
    c_lo = (cond - c_hi.astype(F32)).astype(BF16)
    w = w_ref[...]
    w_hi = w.astype(BF16)
    w_lo = (w - w_hi.astype(F32)).astype(BF16)
    o_ref[...] = _dot(c_hi, w_hi) + (_dot(c_hi, w_lo) + _dot(c_lo, w_hi)) + b_ref[...]


def _ada(c, w_ada, b_ada):
    bsz, d = c.shape
    n = w_ada.shape[1]
    tn = ADA_COLS_PER_STEP
    assert n % tn == 0
    return pl.pallas_call(
        _ada_kernel,
        grid=(n // tn,),
        in_specs=[pl.BlockSpec((bsz, d), lambda j: (0, 0)),
                  pl.BlockSpec((d, tn), lambda j: (0, j)),
                  pl.BlockSpec((1, tn), lambda j: (0, j))],
        out_specs=pl.BlockSpec((bsz, tn), lambda j: (0, j)),
        out_shape=jax.ShapeDtypeStruct((bsz, n), F32),
        name="ada",
    )(c, w_ada, b_ada.reshape(1, n))


def _t5_causal_bucket(dist):
    max_exact = NUM_BUCKETS // 2
    d = np.maximum(dist, 0)
    large = max_exact + (np.log(np.maximum(d, max_exact) / max_exact)
                         / np.log(MAX_DISTANCE / max_exact) * (NUM_BUCKETS - max_exact)).astype(np.int32)
    large = np.minimum(large, NUM_BUCKETS - 1)
    return np.where(d < max_exact, d, large).astype(np.int32)


def _bias_kernel(bucket_ref, rb_ref, sink_ref, bias_ref, sinkrow_ref):
    L = ATTN_BLOCK
    bucket = bucket_ref[...]
    key = lax.broadcasted_iota(jnp.int32, (L, L), 0)
    qry = lax.broadcasted_iota(jnp.int32, (L, L), 1)
    prev_key = key > qry
    for g in range(ATTN_KV_HEADS):
        for r in range(ATTN_GROUP):
            h = g * ATTN_GROUP + r
            cols = slice(r * L, (r + 1) * L)
            acc = jnp.zeros((L, L), F32)
            for b in range(NUM_BUCKETS):
                acc = jnp.where(bucket == b, rb_ref[b, h], acc)
            acc = acc * LOG2E
            cur = jnp.where(prev_key, -jnp.inf, acc)
            bias_ref[0, g, :L, cols] = jnp.full((L, L), -jnp.inf, F32)
            bias_ref[1, g, :L, cols] = jnp.where(prev_key, acc, -jnp.inf)
            bias_ref[0, g, L:, cols] = cur
            bias_ref[1, g, L:, cols] = cur
            sinkrow_ref[g, :, cols] = jnp.full((1, L), sink_ref[h] * LOG2E, F32)


def _bias_table(rel_bias, sinks):
    L = ATTN_BLOCK
    c = np.arange(L)[:, None]
    i = np.arange(L)[None, :]
    bucket = _t5_causal_bucket((i - c) % L)
    return pl.pallas_call(
        _bias_kernel,
        in_specs=[pl.BlockSpec(memory_space=pltpu.VMEM),
                  pl.BlockSpec(memory_space=pltpu.SMEM),
                  pl.BlockSpec(memory_space=pltpu.SMEM)],
        out_specs=[pl.BlockSpec(memory_space=pltpu.VMEM), pl.BlockSpec(memory_space=pltpu.VMEM)],
        out_shape=[jax.ShapeDtypeStruct((2, ATTN_KV_HEADS, 2 * L, ATTN_GROUP * L), F32),
                   jax.ShapeDtypeStruct((ATTN_KV_HEADS, 1, ATTN_GROUP * L), F32)],
        name="bias_table",
    )(jnp.asarray(bucket), rel_bias.astype(F32), sinks.astype(F32))


def _inproj_kernel(x_ref, mod_ref, w_ref, wgu_ref, bg_ref, gn_ref,
                   qa_ref, kva_ref, g_ref, la_ref, *, nsub):
    rs = x_ref.shape[1] // nsub
    p_kv = ATTN_Q
    p_g = p_kv + 2 * ATTN_KV
    p_go = p_g + 2 * GLA_K + GLA_V
    p_z = p_g + GLA_PACK

    def norm(s):
        x = x_ref[0, s * rs:(s + 1) * rs, :]
        return (_rms(x) * (1.0 + mod_ref[0, MOD_SCALE1]) + mod_ref[0, MOD_SHIFT1]).astype(BF16)

    def finish(s, y):
        rows = slice(s * rs, (s + 1) * rs)
        gp = _dot(y[:, p_z:].astype(BF16), wgu_ref[...]) + bg_ref[...]
        gp2 = gp * LOG2E
        la = (jnp.minimum(gp2, 0.0) - jnp.log2(1.0 + jnp.exp2(-jnp.abs(gp2)))) * (1.0 / GLA_GATE_NORM)
        la_hi = la.astype(BF16)
        la_ref[0, rows, :GLA_K] = la_hi
        la_ref[0, rows, GLA_K:] = (la - la_hi.astype(F32)).astype(BF16)
        go = y[:, p_go:p_z]
        g_ref[0, rows, 2 * GLA_K + GLA_V:] = (go * (1.0 + jnp.tanh(0.5 * go)) * (0.5 * gn_ref[...])).astype(BF16)
        g_ref[0, rows, :GLA_K] = (y[:, p_g:p_g + GLA_K] * (GLA_DK ** -0.5)).astype(BF16)
        g_ref[0, rows, GLA_K:2 * GLA_K + GLA_V] = y[:, p_g + GLA_K:p_go].astype(BF16)
        qa_ref[0, rows, :] = (y[:, :p_kv] * (ATTN_HEAD_DIM ** -0.5 * LOG2E)).astype(BF16)
        kva_ref[0, rows, :] = y[:, p_kv:p_g].astype(BF16)

    hb = norm(0)
    y_prev = None
    for s in range(nsub):
        y = _dot(hb, w_ref[...])
        if s + 1 < nsub:
            hb = norm(s + 1)
        if y_prev is not None:
            finish(s - 1, y_prev)
        y_prev = y
    finish(nsub - 1, y_prev)


def _inproj(x, mod, w, wgu, bg, gn, tm, nsub):
    bsz, t, d = x.shape
    const = lambda shape: pl.BlockSpec(shape, lambda b, i: (0,) * len(shape),
                                       pipeline_mode=pl.Buffered(1))
    tile = lambda n: pl.BlockSpec((1, tm, n), lambda b, i: (b, i, 0))
    return pl.pallas_call(
        functools.partial(_inproj_kernel, nsub=nsub),
        grid=(bsz, t // tm),
        in_specs=[tile(d), _mod_spec(d), const(w.shape), const(wgu.shape), const(bg.shape), const(gn.shape)],
        out_specs=[tile(ATTN_Q), tile(2 * ATTN_KV), tile(GLA_PACK), tile(2 * GLA_K)],
        out_shape=[jax.ShapeDtypeStruct((bsz, t, ATTN_Q), BF16),
                   jax.ShapeDtypeStruct((bsz, t, 2 * ATTN_KV), BF16),
                   jax.ShapeDtypeStruct((bsz, t, GLA_PACK), BF16),
                   jax.ShapeDtypeStruct((bsz, t, 2 * GLA_K), BF16)],
        compiler_params=pltpu.CompilerParams(
            dimension_semantics=("arbitrary", "arbitrary"), vmem_limit_bytes=VMEM_LIMIT),
        name="inproj",
    )(x, mod, w, wgu, bg, gn)


def _swa_scores(n, g, q_ref, kv_ref, kvp_ref, bias_ref):
    L = ATTN_BLOCK
    G = ATTN_GROUP
    dh = ATTN_HEAD_DIM
    rows = slice(n * L, (n + 1) * L)
    k_prev = kvp_ref[0, :, :ATTN_KV] if n == 0 else kv_ref[0, (n - 1) * L:n * L, :ATTN_KV]
    k2 = jnp.concatenate([k_prev, kv_ref[0, rows, :ATTN_KV]], axis=0)
    sel = jnp.where(pl.program_id(1) == 0, 0, 1) if n == 0 else 1
    q_t = q_ref[0, rows, g * G * dh:(g + 1) * G * dh].T
    qs_t = jnp.concatenate([q_t[r * dh:(r + 1) * dh, :] for r in range(G)], axis=1)
    s = _dot(k2[:, g * dh:(g + 1) * dh], qs_t) + bias_ref[sel, g]
    return jnp.maximum(s[:L], s[L:])


def _swa_softmax(g, sc, sink_ref):
    L = ATTN_BLOCK
    G = ATTN_GROUP
    key = lax.broadcasted_iota(jnp.int32, (L, G * L), 0)
    qry = lax.broadcasted_iota(jnp.int32, (L, G * L), 1) & (L - 1)
    prev_key = key > qry
    sink = sink_ref[g]
    m = jnp.maximum(jnp.max(sc, axis=0, keepdims=True), sink)
    p_ = jnp.exp2(sc - m)
    denom = jnp.sum(p_, axis=0, keepdims=True) + jnp.exp2(sink - m)
    pb = p_.astype(BF16)
    zero = jnp.zeros((), BF16)
    p2 = jnp.concatenate([jnp.where(prev_key, pb, zero), jnp.where(prev_key, zero, pb)], axis=0)
    return p2, denom


def _swa_pv(n, g, p2, denom, kv_ref, kvp_ref, o_ref):
    L = ATTN_BLOCK
    G = ATTN_GROUP
    dh = ATTN_HEAD_DIM
    rows = slice(n * L, (n + 1) * L)
    v_prev = kvp_ref[0, :, ATTN_KV:] if n == 0 else kv_ref[0, (n - 1) * L:n * L, ATTN_KV:]
    v2 = jnp.concatenate([v_prev, kv_ref[0, rows, ATTN_KV:]], axis=0)
    o = lax.dot_general(v2, p2, TN_DIMS, preferred_element_type=F32)
    o = (o[g * dh:(g + 1) * dh] * (1.0 / denom)).astype(BF16)
    for r in range(G):
        h = g * G + r
        o_ref[0, h * dh:(h + 1) * dh, rows] = o[:, r * L:(r + 1) * L]


def _gla_prep(c, g_ref, la_ref):
    C = GLA_CHUNK
    ri = lax.broadcasted_iota(jnp.int32, (C, C), 0)
    ci = lax.broadcasted_iota(jnp.int32, (C, C), 1)
    tril = (ri >= ci).astype(BF16)
    rows = slice(c * C, (c + 1) * C)
    b = _dot(tril, la_ref[0, rows, :GLA_K]) + _dot(tril, la_ref[0, rows, GLA_K:])
    b_mid = b[C // 2 - 1:C // 2, :]
    b_last = b[C - 1:C, :]
    q = g_ref[0, rows, 0:GLA_K].astype(F32)
    k = g_ref[0, rows, GLA_K:2 * GLA_K].astype(F32)
    q_t = q * jnp.exp2(b - b_mid)
    k_t = k * jnp.exp2(b_mid - b)
    q_in = (q_t * jnp.exp2(b_mid)).astype(BF16)
    k_s = (k_t * jnp.exp2(b_last - b_mid)).astype(BF16)
    return q_in, q_t.astype(BF16), k_t.astype(BF16), k_s, jnp.exp2(b_last)


def _gla_intra(prep):
    C = GLA_CHUNK
    dk = GLA_DK
    _, q_t, k_t, _, _ = prep
    ri2 = lax.broadcasted_iota(jnp.int32, (C, 2 * C), 0)
    ci2 = lax.broadcasted_iota(jnp.int32, (C, 2 * C), 1) & (C - 1)
    causal2 = ri2 >= ci2
    low_k = lax.broadcasted_iota(jnp.int32, (C, 2 * dk), 1) < dk
    zero = jnp.zeros((), BF16)
    out = []
    for p in range(GLA_PAIRS):
        ks = slice(p * 2 * dk, (p + 1) * 2 * dk)
        kt_p = k_t[:, ks]
        kbd = jnp.concatenate([jnp.where(low_k, kt_p, zero), jnp.where(low_k, zero, kt_p)], axis=0)
        a = lax.dot_general(q_t[:, ks], kbd, NT_DIMS, preferred_element_type=F32)
        out.append(jnp.where(causal2, a.astype(BF16), zero))
    return out


def _gla_out(c, prep, a, st, g_ref, o_ref):
    C = GLA_CHUNK
    dk, dv = GLA_DK, GLA_DV
    q_in, _, _, k_s, decay = prep
    low_k = lax.broadcasted_iota(jnp.int32, (C, 2 * dk), 1) < dk
    zero = jnp.zeros((), BF16)
    zeros_v = jnp.zeros((C, dv), BF16)
    rows = slice(c * C, (c + 1) * C)
    st_new = []
    for p in range(GLA_PAIRS):
        ks = slice(p * 2 * dk, (p + 1) * 2 * dk)
        vs = slice(2 * GLA_K + p * 2 * dv, 2 * GLA_K + (p + 1) * 2 * dv)
        gs = slice(2 * GLA_K + GLA_V + p * 2 * dv, 2 * GLA_K + GLA_V + (p + 1) * 2 * dv)
        v_0 = g_ref[0, rows, vs.start:vs.start + dv]
        v_1 = g_ref[0, rows, vs.start + dv:vs.stop]
        vbd = jnp.concatenate([jnp.concatenate([v_0, zeros_v], axis=1),
                               jnp.concatenate([zeros_v, v_1], axis=1)], axis=0)
        o_p = _dot(a[p], vbd) + lax.dot_general(q_in[:, ks], st[p].astype(BF16), NT_DIMS,
                                                preferred_element_type=F32)
        ks_p = k_s[:, ks]
        dst = jnp.concatenate(
            [lax.dot_general(v_0, jnp.where(low_k, ks_p, zero), TN_DIMS, preferred_element_type=F32),
             lax.dot_general(v_1, jnp.where(low_k, zero, ks_p), TN_DIMS, preferred_element_type=F32)],
            axis=0)
        st_new.append(st[p] * decay[:, ks] + dst)
        o_n = jnp.concatenate([_rms(o_p[:, :dv]), _rms(o_p[:, dv:])], axis=1)
        o_ref[0, rows, p * 2 * dv:(p + 1) * 2 * dv] = (o_n * g_ref[0, rows, gs].astype(F32)).astype(BF16)
    return st_new


def _mixer_kernel(q_ref, kv_ref, kvp_ref, bias_ref, sink_ref, g_ref, la_ref,
                  attn_ref, gla_ref, st_ref, *, nblk):
    @pl.when(pl.program_id(1) == 0)
    def _():
        st_ref[...] = jnp.zeros_like(st_ref)

    st = [st_ref[p] for p in range(GLA_PAIRS)]
    units = [(n, g) for n in range(nblk) for g in range(ATTN_KV_HEADS)]
    nu = len(units)
    sc, pd, prep, intra = {}, {}, {}, {}

    def emit(kind, i):
        nonlocal st
        if kind == "prep" and 0 <= i < nblk:
            prep[i] = _gla_prep(i, g_ref, la_ref)
        elif kind == "intra" and 0 <= i < nblk:
            intra[i] = _gla_intra(prep[i])
        elif kind == "out" and 0 <= i < nblk:
            st = _gla_out(i, prep.pop(i), intra.pop(i), st, g_ref, gla_ref)
        elif kind == "scores" and 0 <= i < nu:
            sc[i] = _swa_scores(*units[i], q_ref, kv_ref, kvp_ref, bias_ref)
        elif kind == "softmax" and 0 <= i < nu:
            pd[i] = _swa_softmax(units[i][1], sc.pop(i), sink_ref)
        elif kind == "pv" and 0 <= i < nu:
            _swa_pv(*units[i], *pd.pop(i), kv_ref, kvp_ref, attn_ref)

    for slot in range(nu + 2 * ATTN_KV_HEADS):
        if slot % ATTN_KV_HEADS == 0:
            c = slot // ATTN_KV_HEADS
            emit("out", c - 2)
            emit("intra", c - 1)
            emit("prep", c)
        emit("pv", slot - 2 * ATTN_KV_HEADS)
        emit("softmax", slot - ATTN_KV_HEADS)
        emit("scores", slot)
    for p in range(GLA_PAIRS):
        st_ref[p] = st[p]


def _mixer(qa, kva, bias, sinkrow, gpack, la, tq):
    assert ATTN_BLOCK == GLA_CHUNK == WINDOW
    bsz, t, _ = qa.shape
    assert t % tq == 0 and tq % ATTN_BLOCK == 0
    nblk = tq // ATTN_BLOCK
    whole = lambda a: pl.BlockSpec(a.shape, lambda b, i: (0,) * a.ndim, pipeline_mode=pl.Buffered(1))
    tile = lambda n: pl.BlockSpec((1, tq, n), lambda b, i: (b, i, 0))
    return pl.pallas_call(
        functools.partial(_mixer_kernel, nblk=nblk),
        grid=(bsz, t // tq),
        in_specs=[tile(ATTN_Q), tile(2 * ATTN_KV),
                  pl.BlockSpec((1, ATTN_BLOCK, 2 * ATTN_KV),
                               lambda b, i: (b, jnp.maximum(i * nblk - 1, 0), 0)),
                  whole(bias), whole(sinkrow), tile(GLA_PACK), tile(2 * GLA_K)],
        out_specs=[pl.BlockSpec((1, ATTN_Q, tq), lambda b, i: (b, 0, i)), tile(GLA_V)],
        out_shape=[jax.ShapeDtypeStruct((bsz, ATTN_Q, t), BF16),
                   jax.ShapeDtypeStruct((bsz, t, GLA_V), BF16)],
        scratch_shapes=[pltpu.VMEM((GLA_PAIRS, 2 * GLA_DV, 2 * GLA_DK), F32)],
        compiler_params=pltpu.CompilerParams(
            dimension_semantics=("arbitrary", "arbitrary"), vmem_limit_bytes=VMEM_LIMIT),
        name="mixer",
    )(qa, kva, kva, bias, sinkrow, gpack, la)


def _mlp_kernel(x_ref, attn_ref, gla_ref, mod_ref, wo_ref, w1_ref, w2_ref,
                fg_ref, o_ref, u_ref, *, fchunk, final, nsub):
    rs = x_ref.shape[1] // nsub
    nf = D_FF // fchunk

    def pre(s):
        rows = slice(s * rs, (s + 1) * rs)
        mix = (lax.dot_general(attn_ref[0, :, rows], wo_ref[:ATTN_Q, :], TN_DIMS, preferred_element_type=F32)
               + _dot(gla_ref[0, rows, :], wo_ref[ATTN_Q:, :]))
        x1 = x_ref[0, rows, :] + mod_ref[0, MOD_GATE1] * mix
        return x1, (_rms(x1) * (1.0 + mod_ref[0, MOD_SCALE2]) + mod_ref[0, MOD_SHIFT2]).astype(BF16)

    def up(s, hb, c):
        cols = slice(c * fchunk, (c + 1) * fchunk)
        u = jnp.maximum(_dot(hb, w1_ref[:, cols]), 0.0)
        u_ref[s, :, cols] = (u * u).astype(BF16)

    def post(s, x1, y):
        x2 = x1 + mod_ref[0, MOD_GATE2] * y
        o_ref[0, s * rs:(s + 1) * rs, :] = _rms(x2) * fg_ref[...] if final else x2

    x1, hb = pre(0)
    pending = None
    for s in range(nsub):
        for c in range(nf // 2):
            up(s, hb, c)
        if s + 1 < nsub:
            nxt = pre(s + 1)
        for c in range(nf // 2, nf):
            up(s, hb, c)
        if pending is not None:
            post(*pending)
        pending = (s, x1, _dot(u_ref[s], w2_ref[...]))
        if s + 1 < nsub:
            x1, hb = nxt
    post(*pending)


def _mlp(x, attn, gla, mod, wo, w1, w2, fg, tm, final, nsub):
    bsz, t, d = x.shape
    const = lambda shape: pl.BlockSpec(shape, lambda b, i: (0,) * len(shape),
                                       pipeline_mode=pl.Buffered(1))
    tile = lambda n: pl.BlockSpec((1, tm, n), lambda b, i: (b, i, 0))
    return pl.pallas_call(
        functools.partial(_mlp_kernel, fchunk=MLP_FF_CHUNK, final=final, nsub=nsub),
        grid=(bsz, t // tm),
        in_specs=[tile(d), pl.BlockSpec((1, ATTN_Q, tm), lambda b, i: (b, 0, i)), tile(GLA_V),
                  _mod_spec(d),
                  const(wo.shape), const(w1.shape), const(w2.shape), const(fg.shape)],
        out_specs=tile(d),
        out_shape=jax.ShapeDtypeStruct((bsz, t, d), F32),
        scratch_shapes=[pltpu.VMEM((nsub, tm // nsub, D_FF), BF16)],
        compiler_params=pltpu.CompilerParams(
            dimension_semantics=("arbitrary", "arbitrary"), vmem_limit_bytes=VMEM_LIMIT),
        name="mlp",
    )(x, attn, gla, mod, wo, w1, w2, fg)


def kernel(x, c, w_ada, b_ada, w_in, w_gate_up, b_gate, gla_norm_g, attn_sinks, rel_bias,
           w_out, w_mlp_in, w_mlp_out, final_norm_g):
    bsz, t, d = x.shape
    depth = w_ada.shape[0]
    tile_in, tile_mix, tile_mlp = (min(v, t) for v in (INPROJ_TILE, MIXER_TILE, MLP_TILE))
    for l in range(depth):
        mod = _ada(c, w_ada[l], b_ada[l]).reshape(bsz, N_MOD, 1, d)
        bias, sinkrow = _bias_table(rel_bias, attn_sinks[l])
        win = jnp.pad(w_in[l], ((0, 0), (0, Z_PAD - GLA_GATE_RANK))).astype(BF16)
        wgu = jnp.pad(w_gate_up[l], ((0, Z_PAD - GLA_GATE_RANK), (0, 0))).astype(BF16)
        qa, kva, gpack, la = _inproj(x, mod, win, wgu, b_gate[l].reshape(1, GLA_K),
                                     jnp.tile(gla_norm_g[l], GLA_HEADS).reshape(1, GLA_V),
                                     tile_in, max(1, tile_in // INPROJ_SUB_TILE))
        attn, gla = _mixer(qa, kva, bias, sinkrow, gpack, la, tile_mix)
        x = _mlp(x, attn, gla, mod,
                 w_out[l].astype(BF16), w_mlp_in[l].astype(BF16), w_mlp_out[l].astype(BF16),
                 final_norm_g.reshape(1, d), tile_mlp, final=(l == depth - 1),
                 nsub=max(1, tile_mlp // MLP_SUB_TILE))
    return x
```

```python
import functools

import numpy as np
import jax
import jax.numpy as jnp
from jax import lax
from jax.experimental import pallas as pl
from jax.experimental.pallas import tpu as pltpu

F32 = jnp.float32
BF16 = jnp.bfloat16

D_MODEL = 1024
ATTN_HEADS = 8
ATTN_KV_HEADS = 2
ATTN_GROUP = ATTN_HEADS // ATTN_KV_HEADS
ATTN_HEAD_DIM = 64
WINDOW = 128
ATTN_BLOCK = 128
NUM_BUCKETS = 32
MAX_DISTANCE = 128
GLA_HEADS = 4
GLA_PAIRS = GLA_HEADS // 2
GLA_DK = 64
GLA_DV = 128
GLA_GATE_RANK = 16
GLA_GATE_NORM = 16.0
GLA_CHUNK = 128
D_FF = 4 * D_MODEL
EPS = 1e-6
N_MOD = 6

ATTN_Q = ATTN_HEADS * ATTN_HEAD_DIM
ATTN_KV = ATTN_KV_HEADS * ATTN_HEAD_DIM
GLA_K = GLA_HEADS * GLA_DK
GLA_V = GLA_HEADS * GLA_DV
GLA_PACK = 2 * GLA_K + 2 * GLA_V
LANES = 128
LOG2E = 1.4426950408889634

VMEM_LIMIT = 56 * 1024 * 1024
INPROJ_TILE, INPROJ_SUB_TILE = 1024, 128
MIXER_TILE = 2048
MLP_TILE, MLP_SUB_TILE = 1024, 512
MLP_FF_CHUNK = 1024
ADA_COLS_PER_STEP = 1536

NT_DIMS = (((1,), (1,)), ((), ()))
TN_DIMS = (((0,), (0,)), ((), ()))

MOD_SHIFT1, MOD_SCALE1, MOD_GATE1, MOD_SHIFT2, MOD_SCALE2, MOD_GATE2 = range(N_MOD)


def _dot(a, b):
    return jnp.dot(a, b, preferred_element_type=F32)


def _mod_spec(d):
    return pl.BlockSpec((1, N_MOD, 1, d), lambda b, i: (b, 0, 0, 0))


def _rms(x):
    return x * lax.rsqrt(jnp.mean(x * x, axis=-1, keepdims=True) + EPS)


def _ada_kernel(c_ref, w_ref, b_ref, o_ref):
    c = c_ref[...]
    cond = c * jax.nn.sigmoid(c)
    c_hi = cond.astype(BF16)
    c_lo = (cond - c_hi.astype(F32)).astype(BF16)
    w = w_ref[...]
    w_hi = w.astype(BF16)
    w_lo = (w - w_hi.astype(F32)).astype(BF16)
    o_ref[...] = _dot(c_hi, w_hi) + (_dot(c_hi, w_lo) + _dot(c_lo, w_hi)) + b_ref[...]


def _ada(c, w_ada, b_ada):
    bsz, d = c.shape
    n = w_ada.shape[1]
    tn = ADA_COLS_PER_STEP
    assert n % tn == 0
    return pl.pallas_call(
        _ada_kernel,
        grid=(n // tn,),
        in_specs=[pl.BlockSpec((bsz, d), lambda j: (0, 0)),
                  pl.BlockSpec((d, tn), lambda j: (0, j)),
                  pl.BlockSpec((1, tn), lambda j: (0, j))],
        out_specs=pl.BlockSpec((bsz, tn), lambda j: (0, j)),
        out_shape=jax.ShapeDtypeStruct((bsz, n), F32),
        name="ada",
    )(c, w_ada, b_ada.reshape(1, n))


def _t5_causal_bucket(dist):
    max_exact = NUM_BUCKETS // 2
    d = np.maximum(dist, 0)
    large = max_exact + (np.log(np.maximum(d, max_exact) / max_exact)
                         / np.log(MAX_DISTANCE / max_exact) * (NUM_BUCKETS - max_exact)).astype(np.int32)
    large = np.minimum(large, NUM_BUCKETS - 1)
    return np.where(d < max_exact, d, large).astype(np.int32)


def _bias_kernel(bucket_ref, rb_ref, sink_ref, bias_ref, sinkrow_ref):
    L = ATTN_BLOCK
    bucket = bucket_ref[...]
    key = lax.broadcasted_iota(jnp.int32, (L, L), 0)
    qry = lax.broadcasted_iota(jnp.int32, (L, L), 1)
    prev_key = key > qry
    for g in range(ATTN_KV_HEADS):
        for r in range(ATTN_GROUP):
            h = g * ATTN_GROUP + r
            cols = slice(r * L, (r + 1) * L)
            acc = jnp.zeros((L, L), F32)
            for b in range(NUM_BUCKETS):
                acc = jnp.where(bucket == b, rb_ref[b, h], acc)
            acc = acc * LOG2E
            cur = jnp.where(prev_key, -jnp.inf, acc)
            bias_ref[0, g, :L, cols] = jnp.full((L, L), -jnp.inf, F32)
            bias_ref[1, g, :L, cols] = jnp.where(prev_key, acc, -jnp.inf)
            bias_ref[0, g, L:, cols] = cur
            bias_ref[1, g, L:, cols] = cur
            sinkrow_ref[g, :, cols] = jnp.full((1, L), sink_ref[h] * LOG2E, F32)


def _bias_table(rel_bias, sinks):
    L = ATTN_BLOCK
    c = np.arange(L)[:, None]
    i = np.arange(L)[None, :]
    bucket = _t5_causal_bucket((i - c) % L)
    return pl.pallas_call(
        _bias_kernel,
        in_specs=[pl.BlockSpec(memory_space=pltpu.VMEM),
                  pl.BlockSpec(memory_space=pltpu.SMEM),
                  pl.BlockSpec(memory_space=pltpu.SMEM)],
        out_specs=[pl.BlockSpec(memory_space=pltpu.VMEM), pl.BlockSpec(memory_space=pltpu.VMEM)],
        out_shape=[jax.ShapeDtypeStruct((2, ATTN_KV_HEADS, 2 * L, ATTN_GROUP * L), F32),
                   jax.ShapeDtypeStruct((ATTN_KV_HEADS, 1, ATTN_GROUP * L), F32)],
        name="bias_table",
    )(jnp.asarray(bucket), rel_bias.astype(F32), sinks.astype(F32))


def _gate_weight_kernel(wz_ref, wgu_ref, o_ref):
    a, b = wz_ref[...], wgu_ref[...]
    a_hi, b_hi = a.astype(BF16), b.astype(BF16)
    a_lo = (a - a_hi.astype(F32)).astype(BF16)
    b_lo = (b - b_hi.astype(F32)).astype(BF16)
    o_ref[...] = _dot(a_hi, b_hi) + (_dot(a_hi, b_lo) + _dot(a_lo, b_hi))


def _gate_weight(wz, wgu):
    vm = pl.BlockSpec(memory_space=pltpu.VMEM)
    return pl.pallas_call(
        _gate_weight_kernel, in_specs=[vm, vm], out_specs=vm,
        out_shape=jax.ShapeDtypeStruct((wz.shape[0], wgu.shape[1]), F32), name="gate_weight",
    )(wz, wgu)


def _inproj_kernel(x_ref, mod_ref, w_ref, bg_ref, gn_ref,
                   qa_ref, kva_ref, g_ref, la_ref, *, nsub):
    rs = x_ref.shape[1] // nsub
    p_kv = ATTN_Q
    p_g = p_kv + 2 * ATTN_KV
    p_go = p_g + 2 * GLA_K + GLA_V
    p_z = p_g + GLA_PACK

    def norm(s):
        x = x_ref[0, s * rs:(s + 1) * rs, :]
        return (_rms(x) * (1.0 + mod_ref[0, MOD_SCALE1]) + mod_ref[0, MOD_SHIFT1]).astype(BF16)

    def finish(s, y):
        rows = slice(s * rs, (s + 1) * rs)
        gp = y[:, p_z:] + bg_ref[...]
        gp2 = gp * LOG2E
        la = (jnp.minimum(gp2, 0.0) - jnp.log2(1.0 + jnp.exp2(-jnp.abs(gp2)))) * (1.0 / GLA_GATE_NORM)
        la_hi = la.astype(BF16)
        la_ref[0, rows, :GLA_K] = la_hi
        la_ref[0, rows, GLA_K:] = (la - la_hi.astype(F32)).astype(BF16)
        go = y[:, p_go:p_z]
        g_ref[0, rows, 2 * GLA_K + GLA_V:] = (go * (1.0 + jnp.tanh(0.5 * go)) * (0.5 * gn_ref[...])).astype(BF16)
        g_ref[0, rows, :GLA_K] = (y[:, p_g:p_g + GLA_K] * (GLA_DK ** -0.5)).astype(BF16)
        g_ref[0, rows, GLA_K:2 * GLA_K + GLA_V] = y[:, p_g + GLA_K:p_go].astype(BF16)
        qa_ref[0, rows, :] = (y[:, :p_kv] * (ATTN_HEAD_DIM ** -0.5 * LOG2E)).astype(BF16)
        kva_ref[0, rows, :] = y[:, p_kv:p_g].astype(BF16)

    hb = norm(0)
    y_prev = None
    for s in range(nsub):
        y = _dot(hb, w_ref[...])
        if s + 1 < nsub:
            hb = norm(s + 1)
        if y_prev is not None:
            finish(s - 1, y_prev)
        y_prev = y
    finish(nsub - 1, y_prev)


def _inproj(x, mod, w, bg, gn, tm, nsub):
    bsz, t, d = x.shape
    const = lambda shape: pl.BlockSpec(shape, lambda b, i: (0,) * len(shape),
                                       pipeline_mode=pl.Buffered(1))
    tile = lambda n: pl.BlockSpec((1, tm, n), lambda b, i: (b, i, 0))
    return pl.pallas_call(
        functools.partial(_inproj_kernel, nsub=nsub),
        grid=(bsz, t // tm),
        in_specs=[tile(d), _mod_spec(d), const(w.shape), const(bg.shape), const(gn.shape)],
        out_specs=[tile(ATTN_Q), tile(2 * ATTN_KV), tile(GLA_PACK), tile(2 * GLA_K)],
        out_shape=[jax.ShapeDtypeStruct((bsz, t, ATTN_Q), BF16),
                   jax.ShapeDtypeStruct((bsz, t, 2 * ATTN_KV), BF16),
                   jax.ShapeDtypeStruct((bsz, t, GLA_PACK), BF16),
                   jax.ShapeDtypeStruct((bsz, t, 2 * GLA_K), BF16)],
        compiler_params=pltpu.CompilerParams(
            dimension_semantics=("arbitrary", "arbitrary"), vmem_limit_bytes=VMEM_LIMIT),
        name="inproj",
    )(x, mod, w, bg, gn)


def _swa_scores(n, g, q_ref, kv_ref, kvp_ref, bias_ref):
    L = ATTN_BLOCK
    G = ATTN_GROUP
    dh = ATTN_HEAD_DIM
    rows = slice(n * L, (n + 1) * L)
    k_prev = kvp_ref[0, :, :ATTN_KV] if n == 0 else kv_ref[0, (n - 1) * L:n * L, :ATTN_KV]
    k2 = jnp.concatenate([k_prev, kv_ref[0, rows, :ATTN_KV]], axis=0)
    sel = jnp.where(pl.program_id(1) == 0, 0, 1) if n == 0 else 1
    q_t = q_ref[0, rows, g * G * dh:(g + 1) * G * dh].T
    qs_t = jnp.concatenate([q_t[r * dh:(r + 1) * dh, :] for r in range(G)], axis=1)
    s = _dot(k2[:, g * dh:(g + 1) * dh], qs_t) + bias_ref[sel, g]
    return jnp.maximum(s[:L], s[L:])


def _swa_softmax(g, sc, sink_ref):
    L = ATTN_BLOCK
    G = ATTN_GROUP
    key = lax.broadcasted_iota(jnp.int32, (L, G * L), 0)
    qry = lax.broadcasted_iota(jnp.int32, (L, G * L), 1) & (L - 1)
    prev_key = key > qry
    sink = sink_ref[g]
    m = jnp.maximum(jnp.max(sc, axis=0, keepdims=True), sink)
    p_ = jnp.exp2(sc - m)
    denom = jnp.sum(p_, axis=0, keepdims=True) + jnp.exp2(sink - m)
    pb = p_.astype(BF16)
    zero = jnp.zeros((), BF16)
    p2 = jnp.concatenate([jnp.where(prev_key, pb, zero), jnp.where(prev_key, zero, pb)], axis=0)
    return p2, denom


def _swa_pv(n, g, p2, denom, kv_ref, kvp_ref, o_ref):
    L = ATTN_BLOCK
    G = ATTN_GROUP
    dh = ATTN_HEAD_DIM
    rows = slice(n * L, (n + 1) * L)
    v_prev = kvp_ref[0, :, ATTN_KV:] if n == 0 else kv_ref[0, (n - 1) * L:n * L, ATTN_KV:]
    v2 = jnp.concatenate([v_prev, kv_ref[0, rows, ATTN_KV:]], axis=0)
    o = lax.dot_general(v2, p2, TN_DIMS, preferred_element_type=F32)
    o = (o[g * dh:(g + 1) * dh] * (1.0 / denom)).astype(BF16)
    for r in range(G):
        h = g * G + r
        o_ref[0, h * dh:(h + 1) * dh, rows] = o[:, r * L:(r + 1) * L]


def _gla_prep(c, g_ref, la_ref):
    C = GLA_CHUNK
    ri = lax.broadcasted_iota(jnp.int32, (C, C), 0)
    ci = lax.broadcasted_iota(jnp.int32, (C, C), 1)
    tril = (ri >= ci).astype(BF16)
    rows = slice(c * C, (c + 1) * C)
    b = _dot(tril, la_ref[0, rows, :GLA_K]) + _dot(tril, la_ref[0, rows, GLA_K:])
    b_mid = b[C // 2 - 1:C // 2, :]
    b_last = b[C - 1:C, :]
    q = g_ref[0, rows, 0:GLA_K].astype(F32)
    k = g_ref[0, rows, GLA_K:2 * GLA_K].astype(F32)
    q_t = q * jnp.exp2(b - b_mid)
    k_t = k * jnp.exp2(b_mid - b)
    q_in = (q_t * jnp.exp2(b_mid)).astype(BF16)
    k_s = (k_t * jnp.exp2(b_last - b_mid)).astype(BF16)
    return q_in, q_t.astype(BF16), k_t.astype(BF16), k_s, jnp.exp2(b_last)


def _gla_intra(prep):
    C = GLA_CHUNK
    dk = GLA_DK
    _, q_t, k_t, _, _ = prep
    ri2 = lax.broadcasted_iota(jnp.int32, (C, 2 * C), 0)
    ci2 = lax.broadcasted_iota(jnp.int32, (C, 2 * C), 1) & (C - 1)
    causal2 = ri2 >= ci2
    low_k = lax.broadcasted_iota(jnp.int32, (C, 2 * dk), 1) < dk
    zero = jnp.zeros((), BF16)
    out = []
    for p in range(GLA_PAIRS):
        ks = slice(p * 2 * dk, (p + 1) * 2 * dk)
        kt_p = k_t[:, ks]
        kbd = jnp.concatenate([jnp.where(low_k, kt_p, zero), jnp.where(low_k, zero, kt_p)], axis=0)
        a = lax.dot_general(q_t[:, ks], kbd, NT_DIMS, preferred_element_type=F32)
        out.append(jnp.where(causal2, a.astype(BF16), zero))
    return out


def _gla_out(c, prep, a, st, g_ref, o_ref):
    C = GLA_CHUNK
    dk, dv = GLA_DK, GLA_DV
    q_in, _, _, k_s, decay = prep
    low_k = lax.broadcasted_iota(jnp.int32, (C, 2 * dk), 1) < dk
    zero = jnp.zeros((), BF16)
    zeros_v = jnp.zeros((C, dv), BF16)
    rows = slice(c * C, (c + 1) * C)
    st_new = []
    for p in range(GLA_PAIRS):
        ks = slice(p * 2 * dk, (p + 1) * 2 * dk)
        vs = slice(2 * GLA_K + p * 2 * dv, 2 * GLA_K + (p + 1) * 2 * dv)
        gs = slice(2 * GLA_K + GLA_V + p * 2 * dv, 2 * GLA_K + GLA_V + (p + 1) * 2 * dv)
        v_0 = g_ref[0, rows, vs.start:vs.start + dv]
        v_1 = g_ref[0, rows, vs.start + dv:vs.stop]
        vbd = jnp.concatenate([jnp.concatenate([v_0, zeros_v], axis=1),
                               jnp.concatenate([zeros_v, v_1], axis=1)], axis=0)
        o_p = _dot(a[p], vbd) + lax.dot_general(q_in[:, ks], st[p].astype(BF16), NT_DIMS,
                                                preferred_element_type=F32)
        ks_p = k_s[:, ks]
        dst = jnp.concatenate(
            [lax.dot_general(v_0, jnp.where(low_k, ks_p, zero), TN_DIMS, preferred_element_type=F32),
             lax.dot_general(v_1, jnp.where(low_k, zero, ks_p), TN_DIMS, preferred_element_type=F32)],
            axis=0)
        st_new.append(st[p] * decay[:, ks] + dst)
        o_n = jnp.concatenate([_rms(o_p[:, :dv]), _rms(o_p[:, dv:])], axis=1)
        o_ref[0, rows, p * 2 * dv:(p + 1) * 2 * dv] = (o_n * g_ref[0, rows, gs].astype(F32)).astype(BF16)
    return st_new


def _mixer_kernel(q_ref, kv_ref, kvp_ref, bias_ref, sink_ref, g_ref, la_ref,
                  attn_ref, gla_ref, st_ref, *, nblk):
    @pl.when(pl.program_id(1) == 0)
    def _():
        st_ref[...] = jnp.zeros_like(st_ref)

    st = [st_ref[p] for p in range(GLA_PAIRS)]
    units = [(n, g) for n in range(nblk) for g in range(ATTN_KV_HEADS)]
    nu = len(units)
    sc, pd, prep, intra = {}, {}, {}, {}

    def emit(kind, i):
        nonlocal st
        if kind == "prep" and 0 <= i < nblk:
            prep[i] = _gla_prep(i, g_ref, la_ref)
        elif kind == "intra" and 0 <= i < nblk:
            intra[i] = _gla_intra(prep[i])
        elif kind == "out" and 0 <= i < nblk:
            st = _gla_out(i, prep.pop(i), intra.pop(i), st, g_ref, gla_ref)
        elif kind == "scores" and 0 <= i < nu:
            sc[i] = _swa_scores(*units[i], q_ref, kv_ref, kvp_ref, bias_ref)
        elif kind == "softmax" and 0 <= i < nu:
            pd[i] = _swa_softmax(units[i][1], sc.pop(i), sink_ref)
        elif kind == "pv" and 0 <= i < nu:
            _swa_pv(*units[i], *pd.pop(i), kv_ref, kvp_ref, attn_ref)

    for slot in range(nu + 2 * ATTN_KV_HEADS):
        if slot % ATTN_KV_HEADS == 0:
            c = slot // ATTN_KV_HEADS
            emit("out", c - 2)
            emit("intra", c - 1)
            emit("prep", c)
        emit("pv", slot - 2 * ATTN_KV_HEADS)
        emit("softmax", slot - ATTN_KV_HEADS)
        emit("scores", slot)
    for p in range(GLA_PAIRS):
        st_ref[p] = st[p]


def _mixer(qa, kva, bias, sinkrow, gpack, la, tq):
    assert ATTN_BLOCK == GLA_CHUNK == WINDOW
    bsz, t, _ = qa.shape
    assert t % tq == 0 and tq % ATTN_BLOCK == 0
    nblk = tq // ATTN_BLOCK
    whole = lambda a: pl.BlockSpec(a.shape, lambda b, i: (0,) * a.ndim, pipeline_mode=pl.Buffered(1))
    tile = lambda n: pl.BlockSpec((1, tq, n), lambda b, i: (b, i, 0))
    return pl.pallas_call(
        functools.partial(_mixer_kernel, nblk=nblk),
        grid=(bsz, t // tq),
        in_specs=[tile(ATTN_Q), tile(2 * ATTN_KV),
                  pl.BlockSpec((1, ATTN_BLOCK, 2 * ATTN_KV),
                               lambda b, i: (b, jnp.maximum(i * nblk - 1, 0), 0)),
                  whole(bias), whole(sinkrow), tile(GLA_PACK), tile(2 * GLA_K)],
        out_specs=[pl.BlockSpec((1, ATTN_Q, tq), lambda b, i: (b, 0, i)), tile(GLA_V)],
        out_shape=[jax.ShapeDtypeStruct((bsz, ATTN_Q, t), BF16),
                   jax.ShapeDtypeStruct((bsz, t, GLA_V), BF16)],
        scratch_shapes=[pltpu.VMEM((GLA_PAIRS, 2 * GLA_DV, 2 * GLA_DK), F32)],
        compiler_params=pltpu.CompilerParams(
            dimension_semantics=("arbitrary", "arbitrary"), vmem_limit_bytes=VMEM_LIMIT),
        name="mixer",
    )(qa, kva, kva, bias, sinkrow, gpack, la)


def _mlp_kernel(x_ref, attn_ref, gla_ref, mod_ref, wo_ref, w1_ref, w2_ref,
                fg_ref, o_ref, u_ref, *, fchunk, final, nsub):
    rs = x_ref.shape[1] // nsub
    nf = D_FF // fchunk

    def pre(s):
        rows = slice(s * rs, (s + 1) * rs)
        mix = (lax.dot_general(attn_ref[0, :, rows], wo_ref[:ATTN_Q, :], TN_DIMS, preferred_element_type=F32)
               + _dot(gla_ref[0, rows, :], wo_ref[ATTN_Q:, :]))
        x1 = x_ref[0, rows, :] + mod_ref[0, MOD_GATE1] * mix
        return x1, (_rms(x1) * (1.0 + mod_ref[0, MOD_SCALE2]) + mod_ref[0, MOD_SHIFT2]).astype(BF16)

    def up(s, hb, c):
        cols = slice(c * fchunk, (c + 1) * fchunk)
        u = jnp.maximum(_dot(hb, w1_ref[:, cols]), 0.0)
        u_ref[s, :, cols] = (u * u).astype(BF16)

    def post(s, x1, y):
        x2 = x1 + mod_ref[0, MOD_GATE2] * y
        o_ref[0, s * rs:(s + 1) * rs, :] = _rms(x2) * fg_ref[...] if final else x2

    x1, hb = pre(0)
    pending = None
    for s in range(nsub):
        for c in range(nf // 2):
            up(s, hb, c)
        if s + 1 < nsub:
            nxt = pre(s + 1)
        for c in range(nf // 2, nf):
            up(s, hb, c)
        if pending is not None:
            post(*pending)
        pending = (s, x1, _dot(u_ref[s], w2_ref[...]))
        if s + 1 < nsub:
            x1, hb = nxt
    post(*pending)


def _mlp(x, attn, gla, mod, wo, w1, w2, fg, tm, final, nsub):
    bsz, t, d = x.shape
    const = lambda shape: pl.BlockSpec(shape, lambda b, i: (0,) * len(shape),
                                       pipeline_mode=pl.Buffered(1))
    tile = lambda n: pl.BlockSpec((1, tm, n), lambda b, i: (b, i, 0))
    return pl.pallas_call(
        functools.partial(_mlp_kernel, fchunk=MLP_FF_CHUNK, final=final, nsub=nsub),
        grid=(bsz, t // tm),
        in_specs=[tile(d), pl.BlockSpec((1, ATTN_Q, tm), lambda b, i: (b, 0, i)), tile(GLA_V),
                  _mod_spec(d),
                  const(wo.shape), const(w1.shape), const(w2.shape), const(fg.shape)],
        out_specs=tile(d),
        out_shape=jax.ShapeDtypeStruct((bsz, t, d), F32),
        scratch_shapes=[pltpu.VMEM((nsub, tm // nsub, D_FF), BF16)],
        compiler_params=pltpu.CompilerParams(
            dimension_semantics=("arbitrary", "arbitrary"), vmem_limit_bytes=VMEM_LIMIT),
        name="mlp",
    )(x, attn, gla, mod, wo, w1, w2, fg)


def kernel(x, c, w_ada, b_ada, w_in, w_gate_up, b_gate, gla_norm_g, attn_sinks, rel_bias,
           w_out, w_mlp_in, w_mlp_out, final_norm_g):
    bsz, t, d = x.shape
    depth = w_ada.shape[0]
    tile_in, tile_mix, tile_mlp = (min(v, t) for v in (INPROJ_TILE, MIXER_TILE, MLP_TILE))
    for l in range(depth):
        mod = _ada(c, w_ada[l], b_ada[l]).reshape(bsz, N_MOD, 1, d)
        bias, sinkrow = _bias_table(rel_bias, attn_sinks[l])
        p_z = ATTN_Q + 2 * ATTN_KV + GLA_PACK
        rank_pad = LANES - GLA_GATE_RANK
        wzg = _gate_weight(jnp.pad(w_in[l][:, p_z:], ((0, 0), (0, rank_pad))),
                           jnp.pad(w_gate_up[l], ((0, rank_pad), (0, 0))))
        win = jnp.pad(w_in[l], ((0, 0), (0, p_z + GLA_K - w_in.shape[2])))
        win = lax.dynamic_update_slice(win, wzg, (0, p_z)).astype(BF16)
        qa, kva, gpack, la = _inproj(x, mod, win, b_gate[l].reshape(1, GLA_K),
                                     jnp.tile(gla_norm_g[l], GLA_HEADS).reshape(1, GLA_V),
                                     tile_in, max(1, tile_in // INPROJ_SUB_TILE))
        attn, gla = _mixer(qa, kva, bias, sinkrow, gpack, la, tile_mix)
        x = _mlp(x, attn, gla, mod,
                 w_out[l].astype(BF16), w_mlp_in[l].astype(BF16), w_mlp_out[l].astype(BF16),
                 final_norm_g.reshape(1, d), tile_mlp, final=(l == depth - 1),
                 nsub=max(1, tile_mlp // MLP_SUB_TILE))
    return x
```

```python
import functools

import numpy as np
import jax
import jax.numpy as jnp
from jax import lax
from jax.experimental import pallas as pl
from jax.experimental.pallas import tpu as pltpu

F32 = jnp.float32
BF16 = jnp.bfloat16

D_MODEL = 1024
ATTN_HEADS = 8
ATTN_KV_HEADS = 2
ATTN_GROUP = ATTN_HEADS // ATTN_KV_HEADS
ATTN_HEAD_DIM = 64
WINDOW = 128
ATTN_BLOCK = 128
NUM_BUCKETS = 32
MAX_DISTANCE = 128
GLA_HEADS = 4
GLA_PAIRS = GLA_HEADS // 2
GLA_DK = 64
GLA_DV = 128
GLA_GATE_RANK = 16
GLA_GATE_NORM = 16.0
GLA_CHUNK = 128
D_FF = 4 * D_MODEL
EPS = 1e-6
N_MOD = 6

ATTN_Q = ATTN_HEADS * ATTN_HEAD_DIM
ATTN_KV = ATTN_KV_HEADS * ATTN_HEAD_DIM
GLA_K = GLA_HEADS * GLA_DK
GLA_V = GLA_HEADS * GLA_DV
GLA_PACK = 2 * GLA_K + 2 * GLA_V
LANES = 128
LOG2E = 1.4426950408889634

VMEM_LIMIT = 56 * 1024 * 1024
INPROJ_TILE, INPROJ_SUB_TILE = 1024, 128
MIXER_TILE = 2048
MLP_TILE, MLP_SUB_TILE = 1024, 512
MLP_FF_CHUNK = 1024
ADA_COLS_PER_STEP = 1536

NT_DIMS = (((1,), (1,)), ((), ()))
TN_DIMS = (((0,), (0,)), ((), ()))

MOD_SHIFT1, MOD_SCALE1, MOD_GATE1, MOD_SHIFT2, MOD_SCALE2, MOD_GATE2 = range(N_MOD)


def _dot(a, b):
    return jnp.dot(a, b, preferred_element_type=F32)


def _mod_spec(d):
    return pl.BlockSpec((1, N_MOD, 1, d), lambda b, i: (b, 0, 0, 0))


def _rms(x):
    return x * lax.rsqrt(jnp.mean(x * x, axis=-1, keepdims=True) + EPS)


def _ada_kernel(c_ref, w_ref, b_ref, o_ref):
    c = c_ref[...]
    cond = c * jax.nn.sigmoid(c)
    c_hi = cond.astype(BF16)
    c_lo = (cond - c_hi.astype(F32)).astype(BF16)
    w = w_ref[...]
    w_hi = w.astype(BF16)
    w_lo = (w - w_hi.astype(F32)).astype(BF16)
    o_ref[...] = _dot(c_hi, w_hi) + (_dot(c_hi, w_lo) + _dot(c_lo, w_hi)) + b_ref[...]


def _ada(c, w_ada, b_ada):
    bsz, d = c.shape
    n = w_ada.shape[1]
    tn = ADA_COLS_PER_STEP
    assert n % tn == 0
    return pl.pallas_call(
        _ada_kernel,
        grid=(n // tn,),
        in_specs=[pl.BlockSpec((bsz, d), lambda j: (0, 0)),
                  pl.BlockSpec((d, tn), lambda j: (0, j)),
                  pl.BlockSpec((1, tn), lambda j: (0, j))],
        out_specs=pl.BlockSpec((bsz, tn), lambda j: (0, j)),
        out_shape=jax.ShapeDtypeStruct((bsz, n), F32),
        name="ada",
    )(c, w_ada, b_ada.reshape(1, n))


def _t5_causal_bucket(dist):
    max_exact = NUM_BUCKETS // 2
    d = np.maximum(dist, 0)
    large = max_exact + (np.log(np.maximum(d, max_exact) / max_exact)
                         / np.log(MAX_DISTANCE / max_exact) * (NUM_BUCKETS - max_exact)).astype(np.int32)
    large = np.minimum(large, NUM_BUCKETS - 1)
    return np.where(d < max_exact, d, large).astype(np.int32)


def _bias_kernel(bucket_ref, rb_ref, sink_ref, bias_ref, sinkrow_ref):
    L = ATTN_BLOCK
    bucket = bucket_ref[...]
    key = lax.broadcasted_iota(jnp.int32, (L, L), 0)
    qry = lax.broadcasted_iota(jnp.int32, (L, L), 1)
    prev_key = key > qry
    for g in range(ATTN_KV_HEADS):
        for r in range(ATTN_GROUP):
            h = g * ATTN_GROUP + r
            cols = slice(r * L, (r + 1) * L)
            acc = jnp.zeros((L, L), F32)
            for b in range(NUM_BUCKETS):
                acc = jnp.where(bucket == b, rb_ref[b, h], acc)
            acc = acc * LOG2E
            cur = jnp.where(prev_key, -jnp.inf, acc)
            bias_ref[0, g, :L, cols] = jnp.full((L, L), -jnp.inf, F32)
            bias_ref[1, g, :L, cols] = jnp.where(prev_key, acc, -jnp.inf)
            bias_ref[0, g, L:, cols] = cur
            bias_ref[1, g, L:, cols] = cur
            sinkrow_ref[g, :, cols] = jnp.full((1, L), sink_ref[h] * LOG2E, F32)


def _bias_table(rel_bias, sinks):
    L = ATTN_BLOCK
    c = np.arange(L)[:, None]
    i = np.arange(L)[None, :]
    bucket = _t5_causal_bucket((i - c) % L)
    return pl.pallas_call(
        _bias_kernel,
        in_specs=[pl.BlockSpec(memory_space=pltpu.VMEM),
                  pl.BlockSpec(memory_space=pltpu.SMEM),
                  pl.BlockSpec(memory_space=pltpu.SMEM)],
        out_specs=[pl.BlockSpec(memory_space=pltpu.VMEM), pl.BlockSpec(memory_space=pltpu.VMEM)],
        out_shape=[jax.ShapeDtypeStruct((2, ATTN_KV_HEADS, 2 * L, ATTN_GROUP * L), F32),
                   jax.ShapeDtypeStruct((ATTN_KV_HEADS, 1, ATTN_GROUP * L), F32)],
        name="bias_table",
    )(jnp.asarray(bucket), rel_bias.astype(F32), sinks.astype(F32))


def _gate_weight_kernel(wz_ref, wgu_ref, o_ref):
    a, b = wz_ref[...], wgu_ref[...]
    a_hi, b_hi = a.astype(BF16), b.astype(BF16)
    a_lo = (a - a_hi.astype(F32)).astype(BF16)
    b_lo = (b - b_hi.astype(F32)).astype(BF16)
    o_ref[...] = _dot(a_hi, b_hi) + (_dot(a_hi, b_lo) + _dot(a_lo, b_hi))


def _gate_weight(wz, wgu):
    vm = pl.BlockSpec(memory_space=pltpu.VMEM)
    return pl.pallas_call(
        _gate_weight_kernel, in_specs=[vm, vm], out_specs=vm,
        out_shape=jax.ShapeDtypeStruct((wz.shape[0], wgu.shape[1]), F32), name="gate_weight",
    )(wz, wgu)


def _inproj_kernel(x_ref, mod_ref, w_ref, bg_ref, gn_ref, *refs, nsub):
    n_cast = (len(refs) - 4) // 2
    cast_in, (qa_ref, kva_ref, g_ref, la_ref), cast_out = refs[:n_cast], refs[n_cast:n_cast + 4], refs[n_cast + 4:]
    rs = x_ref.shape[1] // nsub
    p_kv = ATTN_Q
    p_g = p_kv + 2 * ATTN_KV
    p_go = p_g + 2 * GLA_K + GLA_V
    p_z = p_g + GLA_PACK

    def norm(s):
        x = x_ref[0, s * rs:(s + 1) * rs, :]
        return (_rms(x) * (1.0 + mod_ref[0, MOD_SCALE1]) + mod_ref[0, MOD_SHIFT1]).astype(BF16)

    def finish(s, y):
        rows = slice(s * rs, (s + 1) * rs)
        gp = y[:, p_z:] + bg_ref[...]
        gp2 = gp * LOG2E
        la = (jnp.minimum(gp2, 0.0) - jnp.log2(1.0 + jnp.exp2(-jnp.abs(gp2)))) * (1.0 / GLA_GATE_NORM)
        la_hi = la.astype(BF16)
        la_ref[0, rows, :GLA_K] = la_hi
        la_ref[0, rows, GLA_K:] = (la - la_hi.astype(F32)).astype(BF16)
        go = y[:, p_go:p_z]
        g_ref[0, rows, 2 * GLA_K + GLA_V:] = (go * (1.0 + jnp.tanh(0.5 * go)) * (0.5 * gn_ref[...])).astype(BF16)
        g_ref[0, rows, :GLA_K] = (y[:, p_g:p_g + GLA_K] * (GLA_DK ** -0.5)).astype(BF16)
        g_ref[0, rows, GLA_K:2 * GLA_K + GLA_V] = y[:, p_g + GLA_K:p_go].astype(BF16)
        qa_ref[0, rows, :] = (y[:, :p_kv] * (ATTN_HEAD_DIM ** -0.5 * LOG2E)).astype(BF16)
        kva_ref[0, rows, :] = y[:, p_kv:p_g].astype(BF16)

    for src, dst in zip(cast_in, cast_out):
        dst[...] = src[...].astype(BF16)

    hb = norm(0)
    y_prev = None
    for s in range(nsub):
        y = _dot(hb, w_ref[...])
        if s + 1 < nsub:
            hb = norm(s + 1)
        if y_prev is not None:
            finish(s - 1, y_prev)
        y_prev = y
    finish(nsub - 1, y_prev)


def _inproj(x, mod, w, bg, gn, tm, nsub, later_weights):
    bsz, t, d = x.shape
    nt = t // tm
    steps = bsz * nt
    const = lambda shape: pl.BlockSpec(shape, lambda b, i: (0,) * len(shape),
                                       pipeline_mode=pl.Buffered(1))
    tile = lambda n: pl.BlockSpec((1, tm, n), lambda b, i: (b, i, 0))
    slab = lambda a: pl.BlockSpec((a.shape[0] // steps, a.shape[1]), lambda b, i: (b * nt + i, 0))
    assert all(a.shape[0] % (16 * steps) == 0 for a in later_weights)
    outs = pl.pallas_call(
        functools.partial(_inproj_kernel, nsub=nsub),
        grid=(bsz, nt),
        in_specs=[tile(d), _mod_spec(d), const(w.shape), const(bg.shape), const(gn.shape)]
                 + [slab(a) for a in later_weights],
        out_specs=[tile(ATTN_Q), tile(2 * ATTN_KV), tile(GLA_PACK), tile(2 * GLA_K)]
                  + [slab(a) for a in later_weights],
        out_shape=[jax.ShapeDtypeStruct((bsz, t, ATTN_Q), BF16),
                   jax.ShapeDtypeStruct((bsz, t, 2 * ATTN_KV), BF16),
                   jax.ShapeDtypeStruct((bsz, t, GLA_PACK), BF16),
                   jax.ShapeDtypeStruct((bsz, t, 2 * GLA_K), BF16)]
                  + [jax.ShapeDtypeStruct(a.shape, BF16) for a in later_weights],
        compiler_params=pltpu.CompilerParams(
            dimension_semantics=("arbitrary", "arbitrary"), vmem_limit_bytes=VMEM_LIMIT),
        name="inproj",
    )(x, mod, w, bg, gn, *later_weights)
    return outs[:4], outs[4:]


def _swa_scores(n, g, q_ref, kv_ref, kvp_ref, bias_ref):
    L = ATTN_BLOCK
    G = ATTN_GROUP
    dh = ATTN_HEAD_DIM
    rows = slice(n * L, (n + 1) * L)
    k_prev = kvp_ref[0, :, :ATTN_KV] if n == 0 else kv_ref[0, (n - 1) * L:n * L, :ATTN_KV]
    k2 = jnp.concatenate([k_prev, kv_ref[0, rows, :ATTN_KV]], axis=0)
    sel = jnp.where(pl.program_id(1) == 0, 0, 1) if n == 0 else 1
    q_t = q_ref[0, rows, g * G * dh:(g + 1) * G * dh].T
    qs_t = jnp.concatenate([q_t[r * dh:(r + 1) * dh, :] for r in range(G)], axis=1)
    s = _dot(k2[:, g * dh:(g + 1) * dh], qs_t) + bias_ref[sel, g]
    return jnp.maximum(s[:L], s[L:])


def _swa_softmax(g, sc, sink_ref):
    L = ATTN_BLOCK
    G = ATTN_GROUP
    key = lax.broadcasted_iota(jnp.int32, (L, G * L), 0)
    qry = lax.broadcasted_iota(jnp.int32, (L, G * L), 1) & (L - 1)
    prev_key = key > qry
    sink = sink_ref[g]
    m = jnp.maximum(jnp.max(sc, axis=0, keepdims=True), sink)
    p_ = jnp.exp2(sc - m)
    denom = jnp.sum(p_, axis=0, keepdims=True) + jnp.exp2(sink - m)
    pb = p_.astype(BF16)
    zero = jnp.zeros((), BF16)
    p2 = jnp.concatenate([jnp.where(prev_key, pb, zero), jnp.where(prev_key, zero, pb)], axis=0)
    return p2, denom


def _swa_pv(n, g, p2, denom, kv_ref, kvp_ref, o_ref):
    L = ATTN_BLOCK
    G = ATTN_GROUP
    dh = ATTN_HEAD_DIM
    rows = slice(n * L, (n + 1) * L)
    v_prev = kvp_ref[0, :, ATTN_KV:] if n == 0 else kv_ref[0, (n - 1) * L:n * L, ATTN_KV:]
    v2 = jnp.concatenate([v_prev, kv_ref[0, rows, ATTN_KV:]], axis=0)
    o = lax.dot_general(v2, p2, TN_DIMS, preferred_element_type=F32)
    o = (o[g * dh:(g + 1) * dh] * (1.0 / denom)).astype(BF16)
    for r in range(G):
        h = g * G + r
        o_ref[0, h * dh:(h + 1) * dh, rows] = o[:, r * L:(r + 1) * L]


def _gla_prep(c, g_ref, la_ref):
    C = GLA_CHUNK
    ri = lax.broadcasted_iota(jnp.int32, (C, C), 0)
    ci = lax.broadcasted_iota(jnp.int32, (C, C), 1)
    tril = (ri >= ci).astype(BF16)
    rows = slice(c * C, (c + 1) * C)
    b = _dot(tril, la_ref[0, rows, :GLA_K]) + _dot(tril, la_ref[0, rows, GLA_K:])
    b_mid = b[C // 2 - 1:C // 2, :]
    b_last = b[C - 1:C, :]
    q = g_ref[0, rows, 0:GLA_K].astype(F32)
    k = g_ref[0, rows, GLA_K:2 * GLA_K].astype(F32)
    q_t = q * jnp.exp2(b - b_mid)
    k_t = k * jnp.exp2(b_mid - b)
    q_in = (q_t * jnp.exp2(b_mid)).astype(BF16)
    k_s = (k_t * jnp.exp2(b_last - b_mid)).astype(BF16)
    return q_in, q_t.astype(BF16), k_t.astype(BF16), k_s, jnp.exp2(b_last)


def _gla_intra(prep):
    C = GLA_CHUNK
    dk = GLA_DK
    _, q_t, k_t, _, _ = prep
    ri2 = lax.broadcasted_iota(jnp.int32, (C, 2 * C), 0)
    ci2 = lax.broadcasted_iota(jnp.int32, (C, 2 * C), 1) & (C - 1)
    causal2 = ri2 >= ci2
    low_k = lax.broadcasted_iota(jnp.int32, (C, 2 * dk), 1) < dk
    zero = jnp.zeros((), BF16)
    out = []
    for p in range(GLA_PAIRS):
        ks = slice(p * 2 * dk, (p + 1) * 2 * dk)
        kt_p = k_t[:, ks]
        kbd = jnp.concatenate([jnp.where(low_k, kt_p, zero), jnp.where(low_k, zero, kt_p)], axis=0)
        a = lax.dot_general(q_t[:, ks], kbd, NT_DIMS, preferred_element_type=F32)
        out.append(jnp.where(causal2, a.astype(BF16), zero))
    return out


def _gla_out(c, prep, a, st, g_ref, o_ref):
    C = GLA_CHUNK
    dk, dv = GLA_DK, GLA_DV
    q_in, _, _, k_s, decay = prep
    low_k = lax.broadcasted_iota(jnp.int32, (C, 2 * dk), 1) < dk
    zero = jnp.zeros((), BF16)
    zeros_v = jnp.zeros((C, dv), BF16)
    rows = slice(c * C, (c + 1) * C)
    st_new = []
    for p in range(GLA_PAIRS):
        ks = slice(p * 2 * dk, (p + 1) * 2 * dk)
        vs = slice(2 * GLA_K + p * 2 * dv, 2 * GLA_K + (p + 1) * 2 * dv)
        gs = slice(2 * GLA_K + GLA_V + p * 2 * dv, 2 * GLA_K + GLA_V + (p + 1) * 2 * dv)
        v_0 = g_ref[0, rows, vs.start:vs.start + dv]
        v_1 = g_ref[0, rows, vs.start + dv:vs.stop]
        vbd = jnp.concatenate([jnp.concatenate([v_0, zeros_v], axis=1),
                               jnp.concatenate([zeros_v, v_1], axis=1)], axis=0)
        o_p = _dot(a[p], vbd) + lax.dot_general(q_in[:, ks], st[p].astype(BF16), NT_DIMS,
                                                preferred_element_type=F32)
        ks_p = k_s[:, ks]
        dst = jnp.concatenate(
            [lax.dot_general(v_0, jnp.where(low_k, ks_p, zero), TN_DIMS, preferred_element_type=F32),
             lax.dot_general(v_1, jnp.where(low_k, zero, ks_p), TN_DIMS, preferred_element_type=F32)],
            axis=0)
        st_new.append(st[p] * decay[:, ks] + dst)
        o_n = jnp.concatenate([_rms(o_p[:, :dv]), _rms(o_p[:, dv:])], axis=1)
        o_ref[0, rows, p * 2 * dv:(p + 1) * 2 * dv] = (o_n * g_ref[0, rows, gs].astype(F32)).astype(BF16)
    return st_new


def _mixer_kernel(q_ref, kv_ref, kvp_ref, bias_ref, sink_ref, g_ref, la_ref,
                  attn_ref, gla_ref, st_ref, *, nblk):
    @pl.when(pl.program_id(1) == 0)
    def _():
        st_ref[...] = jnp.zeros_like(st_ref)

    st = [st_ref[p] for p in range(GLA_PAIRS)]
    units = [(n, g) for n in range(nblk) for g in range(ATTN_KV_HEADS)]
    nu = len(units)
    sc, pd, prep, intra = {}, {}, {}, {}

    def emit(kind, i):
        nonlocal st
        if kind == "prep" and 0 <= i < nblk:
            prep[i] = _gla_prep(i, g_ref, la_ref)
        elif kind == "intra" and 0 <= i < nblk:
            intra[i] = _gla_intra(prep[i])
        elif kind == "out" and 0 <= i < nblk:
            st = _gla_out(i, prep.pop(i), intra.pop(i), st, g_ref, gla_ref)
        elif kind == "scores" and 0 <= i < nu:
            sc[i] = _swa_scores(*units[i], q_ref, kv_ref, kvp_ref, bias_ref)
        elif kind == "softmax" and 0 <= i < nu:
            pd[i] = _swa_softmax(units[i][1], sc.pop(i), sink_ref)
        elif kind == "pv" and 0 <= i < nu:
            _swa_pv(*units[i], *pd.pop(i), kv_ref, kvp_ref, attn_ref)

    for slot in range(nu + 2 * ATTN_KV_HEADS):
        if slot % ATTN_KV_HEADS == 0:
            c = slot // ATTN_KV_HEADS
            emit("out", c - 2)
            emit("intra", c - 1)
            emit("prep", c)
        emit("pv", slot - 2 * ATTN_KV_HEADS)
        emit("softmax", slot - ATTN_KV_HEADS)
        emit("scores", slot)
    for p in range(GLA_PAIRS):
        st_ref[p] = st[p]


def _mixer(qa, kva, bias, sinkrow, gpack, la, tq):
    assert ATTN_BLOCK == GLA_CHUNK == WINDOW
    bsz, t, _ = qa.shape
    assert t % tq == 0 and tq % ATTN_BLOCK == 0
    nblk = tq // ATTN_BLOCK
    whole = lambda a: pl.BlockSpec(a.shape, lambda b, i: (0,) * a.ndim, pipeline_mode=pl.Buffered(1))
    tile = lambda n: pl.BlockSpec((1, tq, n), lambda b, i: (b, i, 0))
    return pl.pallas_call(
        functools.partial(_mixer_kernel, nblk=nblk),
        grid=(bsz, t // tq),
        in_specs=[tile(ATTN_Q), tile(2 * ATTN_KV),
                  pl.BlockSpec((1, ATTN_BLOCK, 2 * ATTN_KV),
                               lambda b, i: (b, jnp.maximum(i * nblk - 1, 0), 0)),
                  whole(bias), whole(sinkrow), tile(GLA_PACK), tile(2 * GLA_K)],
        out_specs=[pl.BlockSpec((1, ATTN_Q, tq), lambda b, i: (b, 0, i)), tile(GLA_V)],
        out_shape=[jax.ShapeDtypeStruct((bsz, ATTN_Q, t), BF16),
                   jax.ShapeDtypeStruct((bsz, t, GLA_V), BF16)],
        scratch_shapes=[pltpu.VMEM((GLA_PAIRS, 2 * GLA_DV, 2 * GLA_DK), F32)],
        compiler_params=pltpu.CompilerParams(
            dimension_semantics=("arbitrary", "arbitrary"), vmem_limit_bytes=VMEM_LIMIT),
        name="mixer",
    )(qa, kva, kva, bias, sinkrow, gpack, la)


def _mlp_kernel(x_ref, attn_ref, gla_ref, mod_ref, wo_ref, w1_ref, w2_ref,
                fg_ref, o_ref, u_ref, *, fchunk, final, nsub):
    rs = x_ref.shape[1] // nsub
    nf = D_FF // fchunk

    def pre(s):
        rows = slice(s * rs, (s + 1) * rs)
        mix = (lax.dot_general(attn_ref[0, :, rows], wo_ref[:ATTN_Q, :], TN_DIMS, preferred_element_type=F32)
               + _dot(gla_ref[0, rows, :], wo_ref[ATTN_Q:, :]))
        x1 = x_ref[0, rows, :] + mod_ref[0, MOD_GATE1] * mix
        return x1, (_rms(x1) * (1.0 + mod_ref[0, MOD_SCALE2]) + mod_ref[0, MOD_SHIFT2]).astype(BF16)

    def up(s, hb, c):
        cols = slice(c * fchunk, (c + 1) * fchunk)
        u = jnp.maximum(_dot(hb, w1_ref[:, cols]), 0.0)
        u_ref[s, :, cols] = (u * u).astype(BF16)

    def post(s, x1, y):
        x2 = x1 + mod_ref[0, MOD_GATE2] * y
        o_ref[0, s * rs:(s + 1) * rs, :] = _rms(x2) * fg_ref[...] if final else x2

    x1, hb = pre(0)
    pending = None
    for s in range(nsub):
        for c in range(nf // 2):
            up(s, hb, c)
        if s + 1 < nsub:
            nxt = pre(s + 1)
        for c in range(nf // 2, nf):
            up(s, hb, c)
        if pending is not None:
            post(*pending)
        pending = (s, x1, _dot(u_ref[s], w2_ref[...]))
        if s + 1 < nsub:
            x1, hb = nxt
    post(*pending)


def _mlp(x, attn, gla, mod, wo, w1, w2, fg, tm, final, nsub):
    bsz, t, d = x.shape
    const = lambda shape: pl.BlockSpec(shape, lambda b, i: (0,) * len(shape),
                                       pipeline_mode=pl.Buffered(1))
    tile = lambda n: pl.BlockSpec((1, tm, n), lambda b, i: (b, i, 0))
    return pl.pallas_call(
        functools.partial(_mlp_kernel, fchunk=MLP_FF_CHUNK, final=final, nsub=nsub),
        grid=(bsz, t // tm),
        in_specs=[tile(d), pl.BlockSpec((1, ATTN_Q, tm), lambda b, i: (b, 0, i)), tile(GLA_V),
                  _mod_spec(d),
                  const(wo.shape), const(w1.shape), const(w2.shape), const(fg.shape)],
        out_specs=tile(d),
        out_shape=jax.ShapeDtypeStruct((bsz, t, d), F32),
        scratch_shapes=[pltpu.VMEM((nsub, tm // nsub, D_FF), BF16)],
        compiler_params=pltpu.CompilerParams(
            dimension_semantics=("arbitrary", "arbitrary"), vmem_limit_bytes=VMEM_LIMIT),
        name="mlp",
    )(x, attn, gla, mod, wo, w1, w2, fg)


def kernel(x, c, w_ada, b_ada, w_in, w_gate_up, b_gate, gla_norm_g, attn_sinks, rel_bias,
           w_out, w_mlp_in, w_mlp_out, final_norm_g):
    bsz, t, d = x.shape
    depth = w_ada.shape[0]
    tile_in, tile_mix, tile_mlp = (min(v, t) for v in (INPROJ_TILE, MIXER_TILE, MLP_TILE))
    for l in range(depth):
        mod = _ada(c, w_ada[l], b_ada[l]).reshape(bsz, N_MOD, 1, d)
        bias, sinkrow = _bias_table(rel_bias, attn_sinks[l])
        p_z = ATTN_Q + 2 * ATTN_KV + GLA_PACK
        rank_pad = LANES - GLA_GATE_RANK
        wzg = _gate_weight(jnp.pad(w_in[l][:, p_z:], ((0, 0), (0, rank_pad))),
                           jnp.pad(w_gate_up[l], ((0, rank_pad), (0, 0))))
        win = jnp.pad(w_in[l], ((0, 0), (0, p_z + GLA_K - w_in.shape[2])))
        win = lax.dynamic_update_slice(win, wzg, (0, p_z)).astype(BF16)
        (qa, kva, gpack, la), (wo, w1, w2) = _inproj(
            x, mod, win, b_gate[l].reshape(1, GLA_K), jnp.tile(gla_norm_g[l], GLA_HEADS).reshape(1, GLA_V),
            tile_in, max(1, tile_in // INPROJ_SUB_TILE), (w_out[l], w_mlp_in[l], w_mlp_out[l]))
        attn, gla = _mixer(qa, kva, bias, sinkrow, gpack, la, tile_mix)
        x = _mlp(x, attn, gla, mod, wo, w1, w2,
                 final_norm_g.reshape(1, d), tile_mlp, final=(l == depth - 1),
                 nsub=max(1, tile_mlp // MLP_SUB_TILE))
    return x
```

```python
import functools

import numpy as np
import jax
import jax.numpy as jnp
from jax import lax
from jax.experimental import pallas as pl
from jax.experimental.pallas import tpu as pltpu

F32 = jnp.float32
BF16 = jnp.bfloat16

D_MODEL = 1024
ATTN_HEADS = 8
ATTN_KV_HEADS = 2
ATTN_GROUP = ATTN_HEADS // ATTN_KV_HEADS
ATTN_HEAD_DIM = 64
WINDOW = 128
ATTN_BLOCK = 128
NUM_BUCKETS = 32
MAX_DISTANCE = 128
GLA_HEADS = 4
GLA_PAIRS = GLA_HEADS // 2
GLA_DK = 64
GLA_DV = 128
GLA_GATE_RANK = 16
GLA_GATE_NORM = 16.0
GLA_CHUNK = 128
D_FF = 4 * D_MODEL
EPS = 1e-6
N_MOD = 6

ATTN_Q = ATTN_HEADS * ATTN_HEAD_DIM
ATTN_KV = ATTN_KV_HEADS * ATTN_HEAD_DIM
GLA_K = GLA_HEADS * GLA_DK
GLA_V = GLA_HEADS * GLA_DV
GLA_PACK = 2 * GLA_K + 2 * GLA_V
LANES = 128
LOG2E = 1.4426950408889634

VMEM_LIMIT = 56 * 1024 * 1024
INPROJ_TILE, INPROJ_SUB_TILE = 1024, 128
MIXER_TILE = 2048
MLP_TILE, MLP_SUB_TILE = 1024, 512
MLP_FF_CHUNK = 1024
INPROJ_WEIGHT_ROWS = 256
ADA_COLS_PER_STEP = 1536

NT_DIMS = (((1,), (1,)), ((), ()))
TN_DIMS = (((0,), (0,)), ((), ()))

MOD_SHIFT1, MOD_SCALE1, MOD_GATE1, MOD_SHIFT2, MOD_SCALE2, MOD_GATE2 = range(N_MOD)


def _dot(a, b):
    return jnp.dot(a, b, preferred_element_type=F32)


def _mod_spec(d):
    return pl.BlockSpec((1, N_MOD, 1, d), lambda b, i: (b, 0, 0, 0))


def _rms(x):
    return x * lax.rsqrt(jnp.mean(x * x, axis=-1, keepdims=True) + EPS)


def _ada_kernel(c_ref, w_ref, b_ref, o_ref):
    c = c_ref[...]
    cond = c * jax.nn.sigmoid(c)
    c_hi = cond.astype(BF16)
    c_lo = (cond - c_hi.astype(F32)).astype(BF16)
    w = w_ref[...]
    w_hi = w.astype(BF16)
    w_lo = (w - w_hi.astype(F32)).astype(BF16)
    o_ref[...] = _dot(c_hi, w_hi) + (_dot(c_hi, w_lo) + _dot(c_lo, w_hi)) + b_ref[...]


def _ada(c, w_ada, b_ada):
    bsz, d = c.shape
    n = w_ada.shape[1]
    tn = ADA_COLS_PER_STEP
    assert n % tn == 0
    return pl.pallas_call(
        _ada_kernel,
        grid=(n // tn,),
        in_specs=[pl.BlockSpec((bsz, d), lambda j: (0, 0)),
                  pl.BlockSpec((d, tn), lambda j: (0, j)),
                  pl.BlockSpec((1, tn), lambda j: (0, j))],
        out_specs=pl.BlockSpec((bsz, tn), lambda j: (0, j)),
        out_shape=jax.ShapeDtypeStruct((bsz, n), F32),
        name="ada",
    )(c, w_ada, b_ada.reshape(1, n))


def _t5_causal_bucket(dist):
    max_exact = NUM_BUCKETS // 2
    d = np.maximum(dist, 0)
    large = max_exact + (np.log(np.maximum(d, max_exact) / max_exact)
                         / np.log(MAX_DISTANCE / max_exact) * (NUM_BUCKETS - max_exact)).astype(np.int32)
    large = np.minimum(large, NUM_BUCKETS - 1)
    return np.where(d < max_exact, d, large).astype(np.int32)


def _bias_kernel(bucket_ref, rb_ref, sink_ref, bias_ref, sinkrow_ref):
    L = ATTN_BLOCK
    bucket = bucket_ref[...]
    key = lax.broadcasted_iota(jnp.int32, (L, L), 0)
    qry = lax.broadcasted_iota(jnp.int32, (L, L), 1)
    prev_key = key > qry
    for g in range(ATTN_KV_HEADS):
        for r in range(ATTN_GROUP):
            h = g * ATTN_GROUP + r
            cols = slice(r * L, (r + 1) * L)
            acc = jnp.zeros((L, L), F32)
            for b in range(NUM_BUCKETS):
                acc = jnp.where(bucket == b, rb_ref[b, h], acc)
            acc = acc * LOG2E
            cur = jnp.where(prev_key, -jnp.inf, acc)
            bias_ref[0, g, :L, cols] = jnp.full((L, L), -jnp.inf, F32)
            bias_ref[1, g, :L, cols] = jnp.where(prev_key, acc, -jnp.inf)
            bias_ref[0, g, L:, cols] = cur
            bias_ref[1, g, L:, cols] = cur
            sinkrow_ref[g, :, cols] = jnp.full((1, L), sink_ref[h] * LOG2E, F32)


def _bias_table(rel_bias, sinks):
    L = ATTN_BLOCK
    c = np.arange(L)[:, None]
    i = np.arange(L)[None, :]
    bucket = _t5_causal_bucket((i - c) % L)
    return pl.pallas_call(
        _bias_kernel,
        in_specs=[pl.BlockSpec(memory_space=pltpu.VMEM),
                  pl.BlockSpec(memory_space=pltpu.SMEM),
                  pl.BlockSpec(memory_space=pltpu.SMEM)],
        out_specs=[pl.BlockSpec(memory_space=pltpu.VMEM), pl.BlockSpec(memory_space=pltpu.VMEM)],
        out_shape=[jax.ShapeDtypeStruct((2, ATTN_KV_HEADS, 2 * L, ATTN_GROUP * L), F32),
                   jax.ShapeDtypeStruct((ATTN_KV_HEADS, 1, ATTN_GROUP * L), F32)],
        name="bias_table",
    )(jnp.asarray(bucket), rel_bias.astype(F32), sinks.astype(F32))


def _inproj_weight_kernel(w_ref, wgu_ref, o_ref):
    p_z = o_ref.shape[1] - wgu_ref.shape[1]
    o_ref[:, :p_z] = w_ref[:, :p_z].astype(BF16)
    a, b = w_ref[:, p_z:], wgu_ref[...]
    a_hi, b_hi = a.astype(BF16), b.astype(BF16)
    a_lo = (a - a_hi.astype(F32)).astype(BF16)
    b_lo = (b - b_hi.astype(F32)).astype(BF16)
    o_ref[:, p_z:] = (_dot(a_hi, b_hi) + (_dot(a_hi, b_lo) + _dot(a_lo, b_hi))).astype(BF16)


def _inproj_weight(w, wgu):
    d, n = w.shape
    rank, gk = wgu.shape
    rows = INPROJ_WEIGHT_ROWS
    assert d % rows == 0
    return pl.pallas_call(
        _inproj_weight_kernel,
        grid=(d // rows,),
        in_specs=[pl.BlockSpec((rows, n), lambda i: (i, 0)), pl.BlockSpec((rank, gk), lambda i: (0, 0))],
        out_specs=pl.BlockSpec((rows, n - rank + gk), lambda i: (i, 0)),
        out_shape=jax.ShapeDtypeStruct((d, n - rank + gk), BF16),
        name="inproj_weight",
    )(w, wgu)


def _inproj_kernel(x_ref, mod_ref, w_ref, bg_ref, gn_ref, *refs, nsub):
    n_cast = (len(refs) - 4) // 2
    cast_in, (qa_ref, kva_ref, g_ref, la_ref), cast_out = refs[:n_cast], refs[n_cast:n_cast + 4], refs[n_cast + 4:]
    rs = x_ref.shape[1] // nsub
    p_kv = ATTN_Q
    p_g = p_kv + 2 * ATTN_KV
    p_go = p_g + 2 * GLA_K + GLA_V
    p_z = p_g + GLA_PACK

    def norm(s):
        x = x_ref[0, s * rs:(s + 1) * rs, :]
        return (_rms(x) * (1.0 + mod_ref[0, MOD_SCALE1]) + mod_ref[0, MOD_SHIFT1]).astype(BF16)

    def finish(s, y):
        rows = slice(s * rs, (s + 1) * rs)
        gp = y[:, p_z:] + bg_ref[...]
        gp2 = gp * LOG2E
        la = (jnp.minimum(gp2, 0.0) - jnp.log2(1.0 + jnp.exp2(-jnp.abs(gp2)))) * (1.0 / GLA_GATE_NORM)
        la_hi = la.astype(BF16)
        la_ref[0, rows, :GLA_K] = la_hi
        la_ref[0, rows, GLA_K:] = (la - la_hi.astype(F32)).astype(BF16)
        go = y[:, p_go:p_z]
        g_ref[0, rows, 2 * GLA_K + GLA_V:] = (go * (1.0 + jnp.tanh(0.5 * go)) * (0.5 * gn_ref[...])).astype(BF16)
        g_ref[0, rows, :GLA_K] = (y[:, p_g:p_g + GLA_K] * (GLA_DK ** -0.5)).astype(BF16)
        g_ref[0, rows, GLA_K:2 * GLA_K + GLA_V] = y[:, p_g + GLA_K:p_go].astype(BF16)
        qa_ref[0, rows, :] = (y[:, :p_kv] * (ATTN_HEAD_DIM ** -0.5 * LOG2E)).astype(BF16)
        kva_ref[0, rows, :] = y[:, p_kv:p_g].astype(BF16)

    for src, dst in zip(cast_in, cast_out):
        dst[...] = src[...].astype(BF16)

    hb = norm(0)
    y_prev = None
    for s in range(nsub):
        y = _dot(hb, w_ref[...])
        if s + 1 < nsub:
            hb = norm(s + 1)
        if y_prev is not None:
            finish(s - 1, y_prev)
        y_prev = y
    finish(nsub - 1, y_prev)


def _inproj(x, mod, w, bg, gn, tm, nsub, later_weights):
    bsz, t, d = x.shape
    nt = t // tm
    steps = bsz * nt
    const = lambda shape: pl.BlockSpec(shape, lambda b, i: (0,) * len(shape),
                                       pipeline_mode=pl.Buffered(1))
    tile = lambda n: pl.BlockSpec((1, tm, n), lambda b, i: (b, i, 0))
    slab = lambda a: pl.BlockSpec((a.shape[0] // steps, a.shape[1]), lambda b, i: (b * nt + i, 0))
    assert all(a.shape[0] % (16 * steps) == 0 for a in later_weights)
    outs = pl.pallas_call(
        functools.partial(_inproj_kernel, nsub=nsub),
        grid=(bsz, nt),
        in_specs=[tile(d), _mod_spec(d), const(w.shape), const(bg.shape), const(gn.shape)]
                 + [slab(a) for a in later_weights],
        out_specs=[tile(ATTN_Q), tile(2 * ATTN_KV), tile(GLA_PACK), tile(2 * GLA_K)]
                  + [slab(a) for a in later_weights],
        out_shape=[jax.ShapeDtypeStruct((bsz, t, ATTN_Q), BF16),
                   jax.ShapeDtypeStruct((bsz, t, 2 * ATTN_KV), BF16),
                   jax.ShapeDtypeStruct((bsz, t, GLA_PACK), BF16),
                   jax.ShapeDtypeStruct((bsz, t, 2 * GLA_K), BF16)]
                  + [jax.ShapeDtypeStruct(a.shape, BF16) for a in later_weights],
        compiler_params=pltpu.CompilerParams(
            dimension_semantics=("arbitrary", "arbitrary"), vmem_limit_bytes=VMEM_LIMIT),
        name="inproj",
    )(x, mod, w, bg, gn, *later_weights)
    return outs[:4], outs[4:]


def _swa_scores(n, g, q_ref, kv_ref, kvp_ref, bias_ref):
    L = ATTN_BLOCK
    G = ATTN_GROUP
    dh = ATTN_HEAD_DIM
    rows = slice(n * L, (n + 1) * L)
    k_prev = kvp_ref[0, :, :ATTN_KV] if n == 0 else kv_ref[0, (n - 1) * L:n * L, :ATTN_KV]
    k2 = jnp.concatenate([k_prev, kv_ref[0, rows, :ATTN_KV]], axis=0)
    sel = jnp.where(pl.program_id(1) == 0, 0, 1) if n == 0 else 1
    q_t = q_ref[0, rows, g * G * dh:(g + 1) * G * dh].T
    qs_t = jnp.concatenate([q_t[r * dh:(r + 1) * dh, :] for r in range(G)], axis=1)
    s = _dot(k2[:, g * dh:(g + 1) * dh], qs_t) + bias_ref[sel, g]
    return jnp.maximum(s[:L], s[L:])


def _swa_softmax(g, sc, sink_ref):
    L = ATTN_BLOCK
    G = ATTN_GROUP
    key = lax.broadcasted_iota(jnp.int32, (L, G * L), 0)
    qry = lax.broadcasted_iota(jnp.int32, (L, G * L), 1) & (L - 1)
    prev_key = key > qry
    sink = sink_ref[g]
    m = jnp.maximum(jnp.max(sc, axis=0, keepdims=True), sink)
    p_ = jnp.exp2(sc - m)
    denom = jnp.sum(p_, axis=0, keepdims=True) + jnp.exp2(sink - m)
    pb = p_.astype(BF16)
    zero = jnp.zeros((), BF16)
    p2 = jnp.concatenate([jnp.where(prev_key, pb, zero), jnp.where(prev_key, zero, pb)], axis=0)
    return p2, denom


def _swa_pv(n, g, p2, denom, kv_ref, kvp_ref, o_ref):
    L = ATTN_BLOCK
    G = ATTN_GROUP
    dh = ATTN_HEAD_DIM
    rows = slice(n * L, (n + 1) * L)
    v_prev = kvp_ref[0, :, ATTN_KV:] if n == 0 else kv_ref[0, (n - 1) * L:n * L, ATTN_KV:]
    v2 = jnp.concatenate([v_prev, kv_ref[0, rows, ATTN_KV:]], axis=0)
    o = lax.dot_general(v2, p2, TN_DIMS, preferred_element_type=F32)
    o = (o[g * dh:(g + 1) * dh] * (1.0 / denom)).astype(BF16)
    for r in range(G):
        h = g * G + r
        o_ref[0, h * dh:(h + 1) * dh, rows] = o[:, r * L:(r + 1) * L]


def _gla_prep(c, g_ref, la_ref):
    C = GLA_CHUNK
    ri = lax.broadcasted_iota(jnp.int32, (C, C), 0)
    ci = lax.broadcasted_iota(jnp.int32, (C, C), 1)
    tril = (ri >= ci).astype(BF16)
    rows = slice(c * C, (c + 1) * C)
    b = _dot(tril, la_ref[0, rows, :GLA_K]) + _dot(tril, la_ref[0, rows, GLA_K:])
    b_mid = b[C // 2 - 1:C // 2, :]
    b_last = b[C - 1:C, :]
    q = g_ref[0, rows, 0:GLA_K].astype(F32)
    k = g_ref[0, rows, GLA_K:2 * GLA_K].astype(F32)
    q_t = q * jnp.exp2(b - b_mid)
    k_t = k * jnp.exp2(b_mid - b)
    q_in = (q_t * jnp.exp2(b_mid)).astype(BF16)
    k_s = (k_t * jnp.exp2(b_last - b_mid)).astype(BF16)
    return q_in, q_t.astype(BF16), k_t.astype(BF16), k_s, jnp.exp2(b_last)


def _gla_intra(prep):
    C = GLA_CHUNK
    dk = GLA_DK
    _, q_t, k_t, _, _ = prep
    ri2 = lax.broadcasted_iota(jnp.int32, (C, 2 * C), 0)
    ci2 = lax.broadcasted_iota(jnp.int32, (C, 2 * C), 1) & (C - 1)
    causal2 = ri2 >= ci2
    low_k = lax.broadcasted_iota(jnp.int32, (C, 2 * dk), 1) < dk
    zero = jnp.zeros((), BF16)
    out = []
    for p in range(GLA_PAIRS):
        ks = slice(p * 2 * dk, (p + 1) * 2 * dk)
        kt_p = k_t[:, ks]
        kbd = jnp.concatenate([jnp.where(low_k, kt_p, zero), jnp.where(low_k, zero, kt_p)], axis=0)
        a = lax.dot_general(q_t[:, ks], kbd, NT_DIMS, preferred_element_type=F32)
        out.append(jnp.where(causal2, a.astype(BF16), zero))
    return out


def _gla_out(c, prep, a, st, g_ref, o_ref):
    C = GLA_CHUNK
    dk, dv = GLA_DK, GLA_DV
    q_in, _, _, k_s, decay = prep
    low_k = lax.broadcasted_iota(jnp.int32, (C, 2 * dk), 1) < dk
    zero = jnp.zeros((), BF16)
    zeros_v = jnp.zeros((C, dv), BF16)
    rows = slice(c * C, (c + 1) * C)
    st_new = []
    for p in range(GLA_PAIRS):
        ks = slice(p * 2 * dk, (p + 1) * 2 * dk)
        vs = slice(2 * GLA_K + p * 2 * dv, 2 * GLA_K + (p + 1) * 2 * dv)
        gs = slice(2 * GLA_K + GLA_V + p * 2 * dv, 2 * GLA_K + GLA_V + (p + 1) * 2 * dv)
        v_0 = g_ref[0, rows, vs.start:vs.start + dv]
        v_1 = g_ref[0, rows, vs.start + dv:vs.stop]
        vbd = jnp.concatenate([jnp.concatenate([v_0, zeros_v], axis=1),
                               jnp.concatenate([zeros_v, v_1], axis=1)], axis=0)
        o_p = _dot(a[p], vbd) + lax.dot_general(q_in[:, ks], st[p].astype(BF16), NT_DIMS,
                                                preferred_element_type=F32)
        ks_p = k_s[:, ks]
        dst = jnp.concatenate(
            [lax.dot_general(v_0, jnp.where(low_k, ks_p, zero), TN_DIMS, preferred_element_type=F32),
             lax.dot_general(v_1, jnp.where(low_k, zero, ks_p), TN_DIMS, preferred_element_type=F32)],
            axis=0)
        st_new.append(st[p] * decay[:, ks] + dst)
        o_n = jnp.concatenate([_rms(o_p[:, :dv]), _rms(o_p[:, dv:])], axis=1)
        o_ref[0, rows, p * 2 * dv:(p + 1) * 2 * dv] = (o_n * g_ref[0, rows, gs].astype(F32)).astype(BF16)
    return st_new


def _mixer_kernel(q_ref, kv_ref, kvp_ref, bias_ref, sink_ref, g_ref, la_ref,
                  attn_ref, gla_ref, st_ref, *, nblk):
    @pl.when(pl.program_id(1) == 0)
    def _():
        st_ref[...] = jnp.zeros_like(st_ref)

    st = [st_ref[p] for p in range(GLA_PAIRS)]
    units = [(n, g) for n in range(nblk) for g in range(ATTN_KV_HEADS)]
    nu = len(units)
    sc, pd, prep, intra = {}, {}, {}, {}

    def emit(kind, i):
        nonlocal st
        if kind == "prep" and 0 <= i < nblk:
            prep[i] = _gla_prep(i, g_ref, la_ref)
        elif kind == "intra" and 0 <= i < nblk:
            intra[i] = _gla_intra(prep[i])
        elif kind == "out" and 0 <= i < nblk:
            st = _gla_out(i, prep.pop(i), intra.pop(i), st, g_ref, gla_ref)
        elif kind == "scores" and 0 <= i < nu:
            sc[i] = _swa_scores(*units[i], q_ref, kv_ref, kvp_ref, bias_ref)
        elif kind == "softmax" and 0 <= i < nu:
            pd[i] = _swa_softmax(units[i][1], sc.pop(i), sink_ref)
        elif kind == "pv" and 0 <= i < nu:
            _swa_pv(*units[i], *pd.pop(i), kv_ref, kvp_ref, attn_ref)

    for slot in range(nu + 2 * ATTN_KV_HEADS):
        if slot % ATTN_KV_HEADS == 0:
            c = slot // ATTN_KV_HEADS
            emit("out", c - 2)
            emit("intra", c - 1)
            emit("prep", c)
        emit("pv", slot - 2 * ATTN_KV_HEADS)
        emit("softmax", slot - ATTN_KV_HEADS)
        emit("scores", slot)
    for p in range(GLA_PAIRS):
        st_ref[p] = st[p]


def _mixer(qa, kva, bias, sinkrow, gpack, la, tq):
    assert ATTN_BLOCK == GLA_CHUNK == WINDOW
    bsz, t, _ = qa.shape
    assert t % tq == 0 and tq % ATTN_BLOCK == 0
    nblk = tq // ATTN_BLOCK
    whole = lambda a: pl.BlockSpec(a.shape, lambda b, i: (0,) * a.ndim, pipeline_mode=pl.Buffered(1))
    tile = lambda n: pl.BlockSpec((1, tq, n), lambda b, i: (b, i, 0))
    return pl.pallas_call(
        functools.partial(_mixer_kernel, nblk=nblk),
        grid=(bsz, t // tq),
        in_specs=[tile(ATTN_Q), tile(2 * ATTN_KV),
                  pl.BlockSpec((1, ATTN_BLOCK, 2 * ATTN_KV),
                               lambda b, i: (b, jnp.maximum(i * nblk - 1, 0), 0)),
                  whole(bias), whole(sinkrow), tile(GLA_PACK), tile(2 * GLA_K)],
        out_specs=[pl.BlockSpec((1, ATTN_Q, tq), lambda b, i: (b, 0, i)), tile(GLA_V)],
        out_shape=[jax.ShapeDtypeStruct((bsz, ATTN_Q, t), BF16),
                   jax.ShapeDtypeStruct((bsz, t, GLA_V), BF16)],
        scratch_shapes=[pltpu.VMEM((GLA_PAIRS, 2 * GLA_DV, 2 * GLA_DK), F32)],
        compiler_params=pltpu.CompilerParams(
            dimension_semantics=("arbitrary", "arbitrary"), vmem_limit_bytes=VMEM_LIMIT),
        name="mixer",
    )(qa, kva, kva, bias, sinkrow, gpack, la)


def _mlp_kernel(x_ref, attn_ref, gla_ref, mod_ref, wo_ref, w1_ref, w2_ref,
                fg_ref, o_ref, u_ref, *, fchunk, final, nsub):
    rs = x_ref.shape[1] // nsub
    nf = D_FF // fchunk

    def pre(s):
        rows = slice(s * rs, (s + 1) * rs)
        mix = (lax.dot_general(attn_ref[0, :, rows], wo_ref[:ATTN_Q, :], TN_DIMS, preferred_element_type=F32)
               + _dot(gla_ref[0, rows, :], wo_ref[ATTN_Q:, :]))
        x1 = x_ref[0, rows, :] + mod_ref[0, MOD_GATE1] * mix
        return x1, (_rms(x1) * (1.0 + mod_ref[0, MOD_SCALE2]) + mod_ref[0, MOD_SHIFT2]).astype(BF16)

    def up(s, hb, c):
        cols = slice(c * fchunk, (c + 1) * fchunk)
        u = jnp.maximum(_dot(hb, w1_ref[:, cols]), 0.0)
        u_ref[s, :, cols] = (u * u).astype(BF16)

    def post(s, x1, y):
        x2 = x1 + mod_ref[0, MOD_GATE2] * y
        o_ref[0, s * rs:(s + 1) * rs, :] = _rms(x2) * fg_ref[...] if final else x2

    x1, hb = pre(0)
    pending = None
    for s in range(nsub):
        for c in range(nf // 2):
            up(s, hb, c)
        if s + 1 < nsub:
            nxt = pre(s + 1)
        for c in range(nf // 2, nf):
            up(s, hb, c)
        if pending is not None:
            post(*pending)
        pending = (s, x1, _dot(u_ref[s], w2_ref[...]))
        if s + 1 < nsub:
            x1, hb = nxt
    post(*pending)


def _mlp(x, attn, gla, mod, wo, w1, w2, fg, tm, final, nsub):
    bsz, t, d = x.shape
    const = lambda shape: pl.BlockSpec(shape, lambda b, i: (0,) * len(shape),
                                       pipeline_mode=pl.Buffered(1))
    tile = lambda n: pl.BlockSpec((1, tm, n), lambda b, i: (b, i, 0))
    return pl.pallas_call(
        functools.partial(_mlp_kernel, fchunk=MLP_FF_CHUNK, final=final, nsub=nsub),
        grid=(bsz, t // tm),
        in_specs=[tile(d), pl.BlockSpec((1, ATTN_Q, tm), lambda b, i: (b, 0, i)), tile(GLA_V),
                  _mod_spec(d),
                  const(wo.shape), const(w1.shape), const(w2.shape), const(fg.shape)],
        out_specs=tile(d),
        out_shape=jax.ShapeDtypeStruct((bsz, t, d), F32),
        scratch_shapes=[pltpu.VMEM((nsub, tm // nsub, D_FF), BF16)],
        compiler_params=pltpu.CompilerParams(
            dimension_semantics=("arbitrary", "arbitrary"), vmem_limit_bytes=VMEM_LIMIT),
        name="mlp",
    )(x, attn, gla, mod, wo, w1, w2, fg)


def kernel(x, c, w_ada, b_ada, w_in, w_gate_up, b_gate, gla_norm_g, attn_sinks, rel_bias,
           w_out, w_mlp_in, w_mlp_out, final_norm_g):
    bsz, t, d = x.shape
    depth = w_ada.shape[0]
    tile_in, tile_mix, tile_mlp = (min(v, t) for v in (INPROJ_TILE, MIXER_TILE, MLP_TILE))
    for l in range(depth):
        mod = _ada(c, w_ada[l], b_ada[l]).reshape(bsz, N_MOD, 1, d)
        bias, sinkrow = _bias_table(rel_bias, attn_sinks[l])
        win = _inproj_weight(w_in[l], w_gate_up[l])
        (qa, kva, gpack, la), (wo, w1, w2) = _inproj(
            x, mod, win, b_gate[l].reshape(1, GLA_K), jnp.tile(gla_norm_g[l], GLA_HEADS).reshape(1, GLA_V),
            tile_in, max(1, tile_in // INPROJ_SUB_TILE), (w_out[l], w_mlp_in[l], w_mlp_out[l]))
        attn, gla = _mixer(qa, kva, bias, sinkrow, gpack, la, tile_mix)
        x = _mlp(x, attn, gla, mod, wo, w1, w2,
                 final_norm_g.reshape(1, d), tile_mlp, final=(l == depth - 1),
                 nsub=max(1, tile_mlp // MLP_SUB_TILE))
    return x
```

```python
import functools

import numpy as np
import jax
import jax.numpy as jnp
from jax import lax
from jax.experimental import pallas as pl
from jax.experimental.pallas import tpu as pltpu

F32 = jnp.float32
BF16 = jnp.bfloat16

D_MODEL = 1024
ATTN_HEADS = 8
ATTN_KV_HEADS = 2
ATTN_GROUP = ATTN_HEADS // ATTN_KV_HEADS
ATTN_HEAD_DIM = 64
WINDOW = 128
ATTN_BLOCK = 128
NUM_BUCKETS = 32
MAX_DISTANCE = 128
GLA_HEADS = 4
GLA_PAIRS = GLA_HEADS // 2
GLA_DK = 64
GLA_DV = 128
GLA_GATE_RANK = 16
GLA_GATE_NORM = 16.0
GLA_CHUNK = 128
D_FF = 4 * D_MODEL
EPS = 1e-6
N_MOD = 6

ATTN_Q = ATTN_HEADS * ATTN_HEAD_DIM
ATTN_KV = ATTN_KV_HEADS * ATTN_HEAD_DIM
GLA_K = GLA_HEADS * GLA_DK
GLA_V = GLA_HEADS * GLA_DV
GLA_PACK = 2 * GLA_K + 2 * GLA_V
LANES = 128
LOG2E = 1.4426950408889634

VMEM_LIMIT = 56 * 1024 * 1024
INPROJ_TILE, INPROJ_SUB_TILE = 1024, 128
MIXER_TILE = 2048
MLP_TILE, MLP_SUB_TILE = 1024, 512
MLP_FF_CHUNK = 1024
INPROJ_WEIGHT_ROWS = 256
ADA_COLS_PER_STEP = 1536

NT_DIMS = (((1,), (1,)), ((), ()))
TN_DIMS = (((0,), (0,)), ((), ()))

MOD_SHIFT1, MOD_SCALE1, MOD_GATE1, MOD_SHIFT2, MOD_SCALE2, MOD_GATE2 = range(N_MOD)


def _dot(a, b):
    return jnp.dot(a, b, preferred_element_type=F32)


def _mod_spec(d):
    return pl.BlockSpec((1, N_MOD, 1, d), lambda b, i: (b, 0, 0, 0))


def _rms(x):
    return x * lax.rsqrt(jnp.mean(x * x, axis=-1, keepdims=True) + EPS)


def _ada_kernel(c_ref, w_ref, b_ref, o_ref):
    c = c_ref[...]
    cond = c * jax.nn.sigmoid(c)
    c_hi = cond.astype(BF16)
    c_lo = (cond - c_hi.astype(F32)).astype(BF16)
    w = w_ref[...]
    w_hi = w.astype(BF16)
    w_lo = (w - w_hi.astype(F32)).astype(BF16)
    o_ref[...] = _dot(c_hi, w_hi) + (_dot(c_hi, w_lo) + _dot(c_lo, w_hi)) + b_ref[...]


def _ada(c, w_ada, b_ada):
    bsz, d = c.shape
    n = w_ada.shape[1]
    tn = ADA_COLS_PER_STEP
    assert n % tn == 0
    return pl.pallas_call(
        _ada_kernel,
        grid=(n // tn,),
        in_specs=[pl.BlockSpec((bsz, d), lambda j: (0, 0)),
                  pl.BlockSpec((d, tn), lambda j: (0, j)),
                  pl.BlockSpec((1, tn), lambda j: (0, j))],
        out_specs=pl.BlockSpec((bsz, tn), lambda j: (0, j)),
        out_shape=jax.ShapeDtypeStruct((bsz, n), F32),
        name="ada",
    )(c, w_ada, b_ada.reshape(1, n))


def _t5_causal_bucket(dist):
    max_exact = NUM_BUCKETS // 2
    d = np.maximum(dist, 0)
    large = max_exact + (np.log(np.maximum(d, max_exact) / max_exact)
                         / np.log(MAX_DISTANCE / max_exact) * (NUM_BUCKETS - max_exact)).astype(np.int32)
    large = np.minimum(large, NUM_BUCKETS - 1)
    return np.where(d < max_exact, d, large).astype(np.int32)


def _bias_kernel(bucket_ref, rb_ref, sink_ref, bias_ref, sinkrow_ref):
    L = ATTN_BLOCK
    bucket = bucket_ref[...]
    key = lax.broadcasted_iota(jnp.int32, (L, L), 0)
    qry = lax.broadcasted_iota(jnp.int32, (L, L), 1)
    prev_key = key > qry
    for g in range(ATTN_KV_HEADS):
        for r in range(ATTN_GROUP):
            h = g * ATTN_GROUP + r
            cols = slice(r * L, (r + 1) * L)
            acc = jnp.zeros((L, L), F32)
            for b in range(NUM_BUCKETS):
                acc = jnp.where(bucket == b, rb_ref[b, h], acc)
            acc = acc * LOG2E
            cur = jnp.where(prev_key, -jnp.inf, acc)
            bias_ref[0, g, :L, cols] = jnp.full((L, L), -jnp.inf, F32)
            bias_ref[1, g, :L, cols] = jnp.where(prev_key, acc, -jnp.inf)
            bias_ref[0, g, L:, cols] = cur
            bias_ref[1, g, L:, cols] = cur
            sinkrow_ref[g, :, cols] = jnp.full((1, L), sink_ref[h] * LOG2E, F32)


def _bias_table(rel_bias, sinks):
    L = ATTN_BLOCK
    c = np.arange(L)[:, None]
    i = np.arange(L)[None, :]
    bucket = _t5_causal_bucket((i - c) % L)
    return pl.pallas_call(
        _bias_kernel,
        in_specs=[pl.BlockSpec(memory_space=pltpu.VMEM),
                  pl.BlockSpec(memory_space=pltpu.SMEM),
                  pl.BlockSpec(memory_space=pltpu.SMEM)],
        out_specs=[pl.BlockSpec(memory_space=pltpu.VMEM), pl.BlockSpec(memory_space=pltpu.VMEM)],
        out_shape=[jax.ShapeDtypeStruct((2, ATTN_KV_HEADS, 2 * L, ATTN_GROUP * L), F32),
                   jax.ShapeDtypeStruct((ATTN_KV_HEADS, 1, ATTN_GROUP * L), F32)],
        name="bias_table",
    )(jnp.asarray(bucket), rel_bias.astype(F32), sinks.astype(F32))


def _inproj_weight_kernel(wt_ref, wgu_ref, o_ref):
    p_z = o_ref.shape[1] - wgu_ref.shape[2]
    w = wt_ref[0].T
    o_ref[:, :p_z] = w[:, :p_z].astype(BF16)
    a, b = w[:, p_z:], wgu_ref[0]
    a_hi, b_hi = a.astype(BF16), b.astype(BF16)
    a_lo = (a - a_hi.astype(F32)).astype(BF16)
    b_lo = (b - b_hi.astype(F32)).astype(BF16)
    o_ref[:, p_z:] = (_dot(a_hi, b_hi) + (_dot(a_hi, b_lo) + _dot(a_lo, b_hi))).astype(BF16)


def _inproj_weight(w, wgu, l):
    _, d, n = w.shape
    _, rank, gk = wgu.shape
    rows = INPROJ_WEIGHT_ROWS
    assert d % rows == 0
    return pl.pallas_call(
        _inproj_weight_kernel,
        grid=(d // rows,),
        in_specs=[pl.BlockSpec((1, n, rows), lambda i: (l, 0, i)),
                  pl.BlockSpec((1, rank, gk), lambda i: (l, 0, 0))],
        out_specs=pl.BlockSpec((rows, n - rank + gk), lambda i: (i, 0)),
        out_shape=jax.ShapeDtypeStruct((d, n - rank + gk), BF16),
        name="inproj_weight",
    )(jnp.swapaxes(w, 1, 2), wgu)


def _inproj_kernel(x_ref, mod_ref, w_ref, bg_ref, gn_ref, *refs, nsub):
    n_cast = (len(refs) - 4) // 2
    cast_in, (qa_ref, kva_ref, g_ref, la_ref), cast_out = refs[:n_cast], refs[n_cast:n_cast + 4], refs[n_cast + 4:]
    rs = x_ref.shape[1] // nsub
    p_kv = ATTN_Q
    p_g = p_kv + 2 * ATTN_KV
    p_go = p_g + 2 * GLA_K + GLA_V
    p_z = p_g + GLA_PACK

    def norm(s):
        x = x_ref[0, s * rs:(s + 1) * rs, :]
        return (_rms(x) * (1.0 + mod_ref[0, MOD_SCALE1]) + mod_ref[0, MOD_SHIFT1]).astype(BF16)

    def finish(s, y):
        rows = slice(s * rs, (s + 1) * rs)
        gp = y[:, p_z:] + bg_ref[...]
        gp2 = gp * LOG2E
        la = (jnp.minimum(gp2, 0.0) - jnp.log2(1.0 + jnp.exp2(-jnp.abs(gp2)))) * (1.0 / GLA_GATE_NORM)
        la_hi = la.astype(BF16)
        la_ref[0, rows, :GLA_K] = la_hi
        la_ref[0, rows, GLA_K:] = (la - la_hi.astype(F32)).astype(BF16)
        go = y[:, p_go:p_z]
        g_ref[0, rows, 2 * GLA_K + GLA_V:] = (go * (1.0 + jnp.tanh(0.5 * go)) * (0.5 * gn_ref[...])).astype(BF16)
        g_ref[0, rows, :GLA_K] = (y[:, p_g:p_g + GLA_K] * (GLA_DK ** -0.5)).astype(BF16)
        g_ref[0, rows, GLA_K:2 * GLA_K + GLA_V] = y[:, p_g + GLA_K:p_go].astype(BF16)
        qa_ref[0, rows, :] = (y[:, :p_kv] * (ATTN_HEAD_DIM ** -0.5 * LOG2E)).astype(BF16)
        kva_ref[0, rows, :] = y[:, p_kv:p_g].astype(BF16)

    for src, dst in zip(cast_in, cast_out):
        dst[...] = src[...].astype(BF16)

    hb = norm(0)
    y_prev = None
    for s in range(nsub):
        y = _dot(hb, w_ref[...])
        if s + 1 < nsub:
            hb = norm(s + 1)
        if y_prev is not None:
            finish(s - 1, y_prev)
        y_prev = y
    finish(nsub - 1, y_prev)


def _inproj(x, mod, w, bg, gn, tm, nsub, later_weights):
    bsz, t, d = x.shape
    nt = t // tm
    steps = bsz * nt
    const = lambda shape: pl.BlockSpec(shape, lambda b, i: (0,) * len(shape),
                                       pipeline_mode=pl.Buffered(1))
    tile = lambda n: pl.BlockSpec((1, tm, n), lambda b, i: (b, i, 0))
    slab = lambda a: pl.BlockSpec((a.shape[0] // steps, a.shape[1]), lambda b, i: (b * nt + i, 0))
    assert all(a.shape[0] % (16 * steps) == 0 for a in later_weights)
    outs = pl.pallas_call(
        functools.partial(_inproj_kernel, nsub=nsub),
        grid=(bsz, nt),
        in_specs=[tile(d), _mod_spec(d), const(w.shape), const(bg.shape), const(gn.shape)]
                 + [slab(a) for a in later_weights],
        out_specs=[tile(ATTN_Q), tile(2 * ATTN_KV), tile(GLA_PACK), tile(2 * GLA_K)]
                  + [slab(a) for a in later_weights],
        out_shape=[jax.ShapeDtypeStruct((bsz, t, ATTN_Q), BF16),
                   jax.ShapeDtypeStruct((bsz, t, 2 * ATTN_KV), BF16),
                   jax.ShapeDtypeStruct((bsz, t, GLA_PACK), BF16),
                   jax.ShapeDtypeStruct((bsz, t, 2 * GLA_K), BF16)]
                  + [jax.ShapeDtypeStruct(a.shape, BF16) for a in later_weights],
        compiler_params=pltpu.CompilerParams(
            dimension_semantics=("arbitrary", "arbitrary"), vmem_limit_bytes=VMEM_LIMIT),
        name="inproj",
    )(x, mod, w, bg, gn, *later_weights)
    return outs[:4], outs[4:]


def _swa_scores(n, g, q_ref, kv_ref, kvp_ref, bias_ref):
    L = ATTN_BLOCK
    G = ATTN_GROUP
    dh = ATTN_HEAD_DIM
    rows = slice(n * L, (n + 1) * L)
    k_prev = kvp_ref[0, :, :ATTN_KV] if n == 0 else kv_ref[0, (n - 1) * L:n * L, :ATTN_KV]
    k2 = jnp.concatenate([k_prev, kv_ref[0, rows, :ATTN_KV]], axis=0)
    sel = jnp.where(pl.program_id(1) == 0, 0, 1) if n == 0 else 1
    q_t = q_ref[0, rows, g * G * dh:(g + 1) * G * dh].T
    qs_t = jnp.concatenate([q_t[r * dh:(r + 1) * dh, :] for r in range(G)], axis=1)
    s = _dot(k2[:, g * dh:(g + 1) * dh], qs_t) + bias_ref[sel, g]
    return jnp.maximum(s[:L], s[L:])


def _swa_softmax(g, sc, sink_ref):
    L = ATTN_BLOCK
    G = ATTN_GROUP
    key = lax.broadcasted_iota(jnp.int32, (L, G * L), 0)
    qry = lax.broadcasted_iota(jnp.int32, (L, G * L), 1) & (L - 1)
    prev_key = key > qry
    sink = sink_ref[g]
    m = jnp.maximum(jnp.max(sc, axis=0, keepdims=True), sink)
    p_ = jnp.exp2(sc - m)
    denom = jnp.sum(p_, axis=0, keepdims=True) + jnp.exp2(sink - m)
    pb = p_.astype(BF16)
    zero = jnp.zeros((), BF16)
    p2 = jnp.concatenate([jnp.where(prev_key, pb, zero), jnp.where(prev_key, zero, pb)], axis=0)
    return p2, denom


def _swa_pv(n, g, p2, denom, kv_ref, kvp_ref, o_ref):
    L = ATTN_BLOCK
    G = ATTN_GROUP
    dh = ATTN_HEAD_DIM
    rows = slice(n * L, (n + 1) * L)
    v_prev = kvp_ref[0, :, ATTN_KV:] if n == 0 else kv_ref[0, (n - 1) * L:n * L, ATTN_KV:]
    v2 = jnp.concatenate([v_prev, kv_ref[0, rows, ATTN_KV:]], axis=0)
    o = lax.dot_general(v2, p2, TN_DIMS, preferred_element_type=F32)
    o = (o[g * dh:(g + 1) * dh] * (1.0 / denom)).astype(BF16)
    for r in range(G):
        h = g * G + r
        o_ref[0, h * dh:(h + 1) * dh, rows] = o[:, r * L:(r + 1) * L]


def _gla_prep(c, g_ref, la_ref):
    C = GLA_CHUNK
    ri = lax.broadcasted_iota(jnp.int32, (C, C), 0)
    ci = lax.broadcasted_iota(jnp.int32, (C, C), 1)
    tril = (ri >= ci).astype(BF16)
    rows = slice(c * C, (c + 1) * C)
    b = _dot(tril, la_ref[0, rows, :GLA_K]) + _dot(tril, la_ref[0, rows, GLA_K:])
    b_mid = b[C // 2 - 1:C // 2, :]
    b_last = b[C - 1:C, :]
    q = g_ref[0, rows, 0:GLA_K].astype(F32)
    k = g_ref[0, rows, GLA_K:2 * GLA_K].astype(F32)
    q_t = q * jnp.exp2(b - b_mid)
    k_t = k * jnp.exp2(b_mid - b)
    q_in = (q_t * jnp.exp2(b_mid)).astype(BF16)
    k_s = (k_t * jnp.exp2(b_last - b_mid)).astype(BF16)
    return q_in, q_t.astype(BF16), k_t.astype(BF16), k_s, jnp.exp2(b_last)


def _gla_intra(prep):
    C = GLA_CHUNK
    dk = GLA_DK
    _, q_t, k_t, _, _ = prep
    ri2 = lax.broadcasted_iota(jnp.int32, (C, 2 * C), 0)
    ci2 = lax.broadcasted_iota(jnp.int32, (C, 2 * C), 1) & (C - 1)
    causal2 = ri2 >= ci2
    low_k = lax.broadcasted_iota(jnp.int32, (C, 2 * dk), 1) < dk
    zero = jnp.zeros((), BF16)
    out = []
    for p in range(GLA_PAIRS):
        ks = slice(p * 2 * dk, (p + 1) * 2 * dk)
        kt_p = k_t[:, ks]
        kbd = jnp.concatenate([jnp.where(low_k, kt_p, zero), jnp.where(low_k, zero, kt_p)], axis=0)
        a = lax.dot_general(q_t[:, ks], kbd, NT_DIMS, preferred_element_type=F32)
        out.append(jnp.where(causal2, a.astype(BF16), zero))
    return out


def _gla_out(c, prep, a, st, g_ref, o_ref):
    C = GLA_CHUNK
    dk, dv = GLA_DK, GLA_DV
    q_in, _, _, k_s, decay = prep
    low_k = lax.broadcasted_iota(jnp.int32, (C, 2 * dk), 1) < dk
    zero = jnp.zeros((), BF16)
    zeros_v = jnp.zeros((C, dv), BF16)
    rows = slice(c * C, (c + 1) * C)
    st_new = []
    for p in range(GLA_PAIRS):
        ks = slice(p * 2 * dk, (p + 1) * 2 * dk)
        vs = slice(2 * GLA_K + p * 2 * dv, 2 * GLA_K + (p + 1) * 2 * dv)
        gs = slice(2 * GLA_K + GLA_V + p * 2 * dv, 2 * GLA_K + GLA_V + (p + 1) * 2 * dv)
        v_0 = g_ref[0, rows, vs.start:vs.start + dv]
        v_1 = g_ref[0, rows, vs.start + dv:vs.stop]
        vbd = jnp.concatenate([jnp.concatenate([v_0, zeros_v], axis=1),
                               jnp.concatenate([zeros_v, v_1], axis=1)], axis=0)
        o_p = _dot(a[p], vbd) + lax.dot_general(q_in[:, ks], st[p].astype(BF16), NT_DIMS,
                                                preferred_element_type=F32)
        ks_p = k_s[:, ks]
        dst = jnp.concatenate(
            [lax.dot_general(v_0, jnp.where(low_k, ks_p, zero), TN_DIMS, preferred_element_type=F32),
             lax.dot_general(v_1, jnp.where(low_k, zero, ks_p), TN_DIMS, preferred_element_type=F32)],
            axis=0)
        st_new.append(st[p] * decay[:, ks] + dst)
        o_n = jnp.concatenate([_rms(o_p[:, :dv]), _rms(o_p[:, dv:])], axis=1)
        o_ref[0, rows, p * 2 * dv:(p + 1) * 2 * dv] = (o_n * g_ref[0, rows, gs].astype(F32)).astype(BF16)
    return st_new


def _mixer_kernel(q_ref, kv_ref, kvp_ref, bias_ref, sink_ref, g_ref, la_ref,
                  attn_ref, gla_ref, st_ref, *, nblk):
    @pl.when(pl.program_id(1) == 0)
    def _():
        st_ref[...] = jnp.zeros_like(st_ref)

    st = [st_ref[p] for p in range(GLA_PAIRS)]
    units = [(n, g) for n in range(nblk) for g in range(ATTN_KV_HEADS)]
    nu = len(units)
    sc, pd, prep, intra = {}, {}, {}, {}

    def emit(kind, i):
        nonlocal st
        if kind == "prep" and 0 <= i < nblk:
            prep[i] = _gla_prep(i, g_ref, la_ref)
        elif kind == "intra" and 0 <= i < nblk:
            intra[i] = _gla_intra(prep[i])
        elif kind == "out" and 0 <= i < nblk:
            st = _gla_out(i, prep.pop(i), intra.pop(i), st, g_ref, gla_ref)
        elif kind == "scores" and 0 <= i < nu:
            sc[i] = _swa_scores(*units[i], q_ref, kv_ref, kvp_ref, bias_ref)
        elif kind == "softmax" and 0 <= i < nu:
            pd[i] = _swa_softmax(units[i][1], sc.pop(i), sink_ref)
        elif kind == "pv" and 0 <= i < nu:
            _swa_pv(*units[i], *pd.pop(i), kv_ref, kvp_ref, attn_ref)

    for slot in range(nu + 2 * ATTN_KV_HEADS):
        if slot % ATTN_KV_HEADS == 0:
            c = slot // ATTN_KV_HEADS
            emit("out", c - 2)
            emit("intra", c - 1)
            emit("prep", c)
        emit("pv", slot - 2 * ATTN_KV_HEADS)
        emit("softmax", slot - ATTN_KV_HEADS)
        emit("scores", slot)
    for p in range(GLA_PAIRS):
        st_ref[p] = st[p]


def _mixer(qa, kva, bias, sinkrow, gpack, la, tq):
    assert ATTN_BLOCK == GLA_CHUNK == WINDOW
    bsz, t, _ = qa.shape
    assert t % tq == 0 and tq % ATTN_BLOCK == 0
    nblk = tq // ATTN_BLOCK
    whole = lambda a: pl.BlockSpec(a.shape, lambda b, i: (0,) * a.ndim, pipeline_mode=pl.Buffered(1))
    tile = lambda n: pl.BlockSpec((1, tq, n), lambda b, i: (b, i, 0))
    return pl.pallas_call(
        functools.partial(_mixer_kernel, nblk=nblk),
        grid=(bsz, t // tq),
        in_specs=[tile(ATTN_Q), tile(2 * ATTN_KV),
                  pl.BlockSpec((1, ATTN_BLOCK, 2 * ATTN_KV),
                               lambda b, i: (b, jnp.maximum(i * nblk - 1, 0), 0)),
                  whole(bias), whole(sinkrow), tile(GLA_PACK), tile(2 * GLA_K)],
        out_specs=[pl.BlockSpec((1, ATTN_Q, tq), lambda b, i: (b, 0, i)), tile(GLA_V)],
        out_shape=[jax.ShapeDtypeStruct((bsz, ATTN_Q, t), BF16),
                   jax.ShapeDtypeStruct((bsz, t, GLA_V), BF16)],
        scratch_shapes=[pltpu.VMEM((GLA_PAIRS, 2 * GLA_DV, 2 * GLA_DK), F32)],
        compiler_params=pltpu.CompilerParams(
            dimension_semantics=("arbitrary", "arbitrary"), vmem_limit_bytes=VMEM_LIMIT),
        name="mixer",
    )(qa, kva, kva, bias, sinkrow, gpack, la)


def _mlp_kernel(x_ref, attn_ref, gla_ref, mod_ref, wo_ref, w1_ref, w2_ref,
                fg_ref, o_ref, u_ref, *, fchunk, final, nsub):
    rs = x_ref.shape[1] // nsub
    nf = D_FF // fchunk

    def pre(s):
        rows = slice(s * rs, (s + 1) * rs)
        mix = (lax.dot_general(attn_ref[0, :, rows], wo_ref[:ATTN_Q, :], TN_DIMS, preferred_element_type=F32)
               + _dot(gla_ref[0, rows, :], wo_ref[ATTN_Q:, :]))
        x1 = x_ref[0, rows, :] + mod_ref[0, MOD_GATE1] * mix
        return x1, (_rms(x1) * (1.0 + mod_ref[0, MOD_SCALE2]) + mod_ref[0, MOD_SHIFT2]).astype(BF16)

    def up(s, hb, c):
        cols = slice(c * fchunk, (c + 1) * fchunk)
        u = jnp.maximum(_dot(hb, w1_ref[:, cols]), 0.0)
        u_ref[s, :, cols] = (u * u).astype(BF16)

    def post(s, x1, y):
        x2 = x1 + mod_ref[0, MOD_GATE2] * y
        o_ref[0, s * rs:(s + 1) * rs, :] = _rms(x2) * fg_ref[...] if final else x2

    x1, hb = pre(0)
    pending = None
    for s in range(nsub):
        for c in range(nf // 2):
            up(s, hb, c)
        if s + 1 < nsub:
            nxt = pre(s + 1)
        for c in range(nf // 2, nf):
            up(s, hb, c)
        if pending is not None:
            post(*pending)
        pending = (s, x1, _dot(u_ref[s], w2_ref[...]))
        if s + 1 < nsub:
            x1, hb = nxt
    post(*pending)


def _mlp(x, attn, gla, mod, wo, w1, w2, fg, tm, final, nsub):
    bsz, t, d = x.shape
    const = lambda shape: pl.BlockSpec(shape, lambda b, i: (0,) * len(shape),
                                       pipeline_mode=pl.Buffered(1))
    tile = lambda n: pl.BlockSpec((1, tm, n), lambda b, i: (b, i, 0))
    return pl.pallas_call(
        functools.partial(_mlp_kernel, fchunk=MLP_FF_CHUNK, final=final, nsub=nsub),
        grid=(bsz, t // tm),
        in_specs=[tile(d), pl.BlockSpec((1, ATTN_Q, tm), lambda b, i: (b, 0, i)), tile(GLA_V),
                  _mod_spec(d),
                  const(wo.shape), const(w1.shape), const(w2.shape), const(fg.shape)],
        out_specs=tile(d),
        out_shape=jax.ShapeDtypeStruct((bsz, t, d), F32),
        scratch_shapes=[pltpu.VMEM((nsub, tm // nsub, D_FF), BF16)],
        compiler_params=pltpu.CompilerParams(
            dimension_semantics=("arbitrary", "arbitrary"), vmem_limit_bytes=VMEM_LIMIT),
        name="mlp",
    )(x, attn, gla, mod, wo, w1, w2, fg)


def kernel(x, c, w_ada, b_ada, w_in, w_gate_up, b_gate, gla_norm_g, attn_sinks, rel_bias,
           w_out, w_mlp_in, w_mlp_out, final_norm_g):
    bsz, t, d = x.shape
    depth = w_ada.shape[0]
    tile_in, tile_mix, tile_mlp = (min(v, t) for v in (INPROJ_TILE, MIXER_TILE, MLP_TILE))
    for l in range(depth):
        mod = _ada(c, w_ada[l], b_ada[l]).reshape(bsz, N_MOD, 1, d)
        bias, sinkrow = _bias_table(rel_bias, attn_sinks[l])
        win = _inproj_weight(w_in, w_gate_up, l)
        (qa, kva, gpack, la), (wo, w1, w2) = _inproj(
            x, mod, win, b_gate[l].reshape(1, GLA_K), jnp.tile(gla_norm_g[l], GLA_HEADS).reshape(1, GLA_V),
            tile_in, max(1, tile_in // INPROJ_SUB_TILE), (w_out[l], w_mlp_in[l], w_mlp_out[l]))
        attn, gla = _mixer(qa, kva, bias, sinkrow, gpack, la, tile_mix)
        x = _mlp(x, attn, gla, mod, wo, w1, w2,
                 final_norm_g.reshape(1, d), tile_mlp, final=(l == depth - 1),
                 nsub=max(1, tile_mlp // MLP_SUB_TILE))
    return x
```

```python
import functools

import numpy as np
import jax
import jax.numpy as jnp
from jax import lax
from jax.experimental import pallas as pl
from jax.experimental.pallas import tpu as pltpu

F32 = jnp.float32
BF16 = jnp.bfloat16

D_MODEL = 1024
ATTN_HEADS = 8
ATTN_KV_HEADS = 2
ATTN_GROUP = ATTN_HEADS // ATTN_KV_HEADS
ATTN_HEAD_DIM = 64
WINDOW = 128
ATTN_BLOCK = 128
NUM_BUCKETS = 32
MAX_DISTANCE = 128
GLA_HEADS = 4
GLA_PAIRS = GLA_HEADS // 2
GLA_DK = 64
GLA_DV = 128
GLA_GATE_RANK = 16
GLA_GATE_NORM = 16.0
GLA_CHUNK = 128
D_FF = 4 * D_MODEL
EPS = 1e-6
N_MOD = 6

ATTN_Q = ATTN_HEADS * ATTN_HEAD_DIM
ATTN_KV = ATTN_KV_HEADS * ATTN_HEAD_DIM
GLA_K = GLA_HEADS * GLA_DK
GLA_V = GLA_HEADS * GLA_DV
GLA_PACK = 2 * GLA_K + 2 * GLA_V
LANES = 128
LOG2E = 1.4426950408889634

VMEM_LIMIT = 56 * 1024 * 1024
INPROJ_TILE, INPROJ_SUB_TILE = 1024, 128
MIXER_TILE = 2048
MLP_TILE, MLP_SUB_TILE = 1024, 512
MLP_FF_CHUNK = 1024
INPROJ_WEIGHT_ROWS = 256
ADA_COLS_PER_STEP = 1536

NT_DIMS = (((1,), (1,)), ((), ()))
TN_DIMS = (((0,), (0,)), ((), ()))

MOD_SHIFT1, MOD_SCALE1, MOD_GATE1, MOD_SHIFT2, MOD_SCALE2, MOD_GATE2 = range(N_MOD)


def _dot(a, b):
    return jnp.dot(a, b, preferred_element_type=F32)


def _mod_spec(d):
    return pl.BlockSpec((1, N_MOD, 1, d), lambda b, i: (b, 0, 0, 0))


def _rms(x):
    return x * lax.rsqrt(jnp.mean(x * x, axis=-1, keepdims=True) + EPS)


def _ada_kernel(c_ref, w_ref, b_ref, o_ref):
    c = c_ref[...]
    cond = c * jax.nn.sigmoid(c)
    c_hi = cond.astype(BF16)
    c_lo = (cond - c_hi.astype(F32)).astype(BF16)
    w = w_ref[...]
    w_hi = w.astype(BF16)
    w_lo = (w - w_hi.astype(F32)).astype(BF16)
    o_ref[...] = _dot(c_hi, w_hi) + (_dot(c_hi, w_lo) + _dot(c_lo, w_hi)) + b_ref[...]


def _ada(c, w_ada, b_ada):
    bsz, d = c.shape
    n = w_ada.shape[1]
    tn = ADA_COLS_PER_STEP
    assert n % tn == 0
    return pl.pallas_call(
        _ada_kernel,
        grid=(n // tn,),
        in_specs=[pl.BlockSpec((bsz, d), lambda j: (0, 0)),
                  pl.BlockSpec((d, tn), lambda j: (0, j)),
                  pl.BlockSpec((1, tn), lambda j: (0, j))],
        out_specs=pl.BlockSpec((bsz, tn), lambda j: (0, j)),
        out_shape=jax.ShapeDtypeStruct((bsz, n), F32),
        name="ada",
    )(c, w_ada, b_ada.reshape(1, n))


def _t5_causal_bucket(dist):
    max_exact = NUM_BUCKETS // 2
    d = np.maximum(dist, 0)
    large = max_exact + (np.log(np.maximum(d, max_exact) / max_exact)
                         / np.log(MAX_DISTANCE / max_exact) * (NUM_BUCKETS - max_exact)).astype(np.int32)
    large = np.minimum(large, NUM_BUCKETS - 1)
    return np.where(d < max_exact, d, large).astype(np.int32)


def _bias_kernel(bucket_ref, rb_ref, sink_ref, bias_ref, sinkrow_ref):
    L = ATTN_BLOCK
    bucket = bucket_ref[...]
    key = lax.broadcasted_iota(jnp.int32, (L, L), 0)
    qry = lax.broadcasted_iota(jnp.int32, (L, L), 1)
    prev_key = key > qry
    for g in range(ATTN_KV_HEADS):
        for r in range(ATTN_GROUP):
            h = g * ATTN_GROUP + r
            cols = slice(r * L, (r + 1) * L)
            acc = jnp.zeros((L, L), F32)
            for b in range(NUM_BUCKETS):
                acc = jnp.where(bucket == b, rb_ref[b, h], acc)
            acc = acc * LOG2E
            cur = jnp.where(prev_key, -jnp.inf, acc)
            bias_ref[0, g, :L, cols] = jnp.full((L, L), -jnp.inf, F32)
            bias_ref[1, g, :L, cols] = jnp.where(prev_key, acc, -jnp.inf)
            bias_ref[0, g, L:, cols] = cur
            bias_ref[1, g, L:, cols] = cur
            sinkrow_ref[g, :, cols] = jnp.full((1, L), sink_ref[h] * LOG2E, F32)


def _bias_table(rel_bias, sinks):
    L = ATTN_BLOCK
    c = np.arange(L)[:, None]
    i = np.arange(L)[None, :]
    bucket = _t5_causal_bucket((i - c) % L)
    return pl.pallas_call(
        _bias_kernel,
        in_specs=[pl.BlockSpec(memory_space=pltpu.VMEM),
                  pl.BlockSpec(memory_space=pltpu.SMEM),
                  pl.BlockSpec(memory_space=pltpu.SMEM)],
        out_specs=[pl.BlockSpec(memory_space=pltpu.VMEM), pl.BlockSpec(memory_space=pltpu.VMEM)],
        out_shape=[jax.ShapeDtypeStruct((2, ATTN_KV_HEADS, 2 * L, ATTN_GROUP * L), F32),
                   jax.ShapeDtypeStruct((ATTN_KV_HEADS, 1, ATTN_GROUP * L), F32)],
        name="bias_table",
    )(jnp.asarray(bucket), rel_bias.astype(F32), sinks.astype(F32))


def _inproj_weight_kernel(wt_ref, wgu_ref, o_ref):
    p_z = o_ref.shape[1] - wgu_ref.shape[2]
    w = wt_ref[0].T
    o_ref[:, :p_z] = w[:, :p_z].astype(BF16)
    a, b = w[:, p_z:], wgu_ref[0]
    a_hi, b_hi = a.astype(BF16), b.astype(BF16)
    a_lo = (a - a_hi.astype(F32)).astype(BF16)
    b_lo = (b - b_hi.astype(F32)).astype(BF16)
    o_ref[:, p_z:] = (_dot(a_hi, b_hi) + (_dot(a_hi, b_lo) + _dot(a_lo, b_hi))).astype(BF16)


def _inproj_weight(w, wgu, l):
    _, d, n = w.shape
    _, rank, gk = wgu.shape
    rows = INPROJ_WEIGHT_ROWS
    assert d % rows == 0
    return pl.pallas_call(
        _inproj_weight_kernel,
        grid=(d // rows,),
        in_specs=[pl.BlockSpec((1, n, rows), lambda i: (l, 0, i)),
                  pl.BlockSpec((1, rank, gk), lambda i: (l, 0, 0))],
        out_specs=pl.BlockSpec((rows, n - rank + gk), lambda i: (i, 0)),
        out_shape=jax.ShapeDtypeStruct((d, n - rank + gk), BF16),
        name="inproj_weight",
    )(jnp.swapaxes(w, 1, 2), wgu)


def _inproj_kernel(x_ref, mod_ref, w_ref, bg_ref, gn_ref, *refs, nsub):
    n_cast = (len(refs) - 4) // 2
    cast_in, (qa_ref, kva_ref, g_ref, la_ref), cast_out = refs[:n_cast], refs[n_cast:n_cast + 4], refs[n_cast + 4:]
    rs = x_ref.shape[1] // nsub
    p_kv = ATTN_Q
    p_g = p_kv + 2 * ATTN_KV
    p_go = p_g + 2 * GLA_K + GLA_V
    p_z = p_g + GLA_PACK

    def norm(s):
        x = x_ref[0, s * rs:(s + 1) * rs, :]
        return (_rms(x) * (1.0 + mod_ref[0, MOD_SCALE1]) + mod_ref[0, MOD_SHIFT1]).astype(BF16)

    def finish(s, y):
        rows = slice(s * rs, (s + 1) * rs)
        gp = y[:, p_z:] + bg_ref[...]
        gp2 = gp * LOG2E
        la = (jnp.minimum(gp2, 0.0) - jnp.log2(1.0 + jnp.exp2(-jnp.abs(gp2)))) * (1.0 / GLA_GATE_NORM)
        la_hi = la.astype(BF16)
        la_ref[0, rows, :GLA_K] = la_hi
        la_ref[0, rows, GLA_K:] = (la - la_hi.astype(F32)).astype(BF16)
        go = y[:, p_go:p_z]
        g_ref[0, rows, 2 * GLA_K + GLA_V:] = (go * (1.0 + jnp.tanh(0.5 * go)) * (0.5 * gn_ref[...])).astype(BF16)
        g_ref[0, rows, :GLA_K] = (y[:, p_g:p_g + GLA_K] * (GLA_DK ** -0.5)).astype(BF16)
        g_ref[0, rows, GLA_K:2 * GLA_K + GLA_V] = y[:, p_g + GLA_K:p_go].astype(BF16)
        qa_ref[0, rows, :] = (y[:, :p_kv] * (ATTN_HEAD_DIM ** -0.5 * LOG2E)).astype(BF16)
        kva_ref[0, rows, :] = y[:, p_kv:p_g].astype(BF16)

    for src, dst in zip(cast_in, cast_out):
        dst[...] = src[...].astype(BF16)

    hb = norm(0)
    y_prev = None
    for s in range(nsub):
        y = _dot(hb, w_ref[...])
        if s + 1 < nsub:
            hb = norm(s + 1)
        if y_prev is not None:
            finish(s - 1, y_prev)
        y_prev = y
    finish(nsub - 1, y_prev)


def _inproj(x, mod, w, bg, gn, tm, nsub, later_weights):
    bsz, t, d = x.shape
    nt = t // tm
    steps = bsz * nt
    const = lambda shape: pl.BlockSpec(shape, lambda b, i: (0,) * len(shape),
                                       pipeline_mode=pl.Buffered(1))
    tile = lambda n: pl.BlockSpec((1, tm, n), lambda b, i: (b, i, 0))
    slab = lambda a: pl.BlockSpec((a.shape[0] // steps, a.shape[1]), lambda b, i: (b * nt + i, 0))
    assert all(a.shape[0] % (16 * steps) == 0 for a in later_weights)
    outs = pl.pallas_call(
        functools.partial(_inproj_kernel, nsub=nsub),
        grid=(bsz, nt),
        in_specs=[tile(d), _mod_spec(d), const(w.shape), const(bg.shape), const(gn.shape)]
                 + [slab(a) for a in later_weights],
        out_specs=[tile(ATTN_Q), tile(2 * ATTN_KV), tile(GLA_PACK), tile(2 * GLA_K)]
                  + [slab(a) for a in later_weights],
        out_shape=[jax.ShapeDtypeStruct((bsz, t, ATTN_Q), BF16),
                   jax.ShapeDtypeStruct((bsz, t, 2 * ATTN_KV), BF16),
                   jax.ShapeDtypeStruct((bsz, t, GLA_PACK), BF16),
                   jax.ShapeDtypeStruct((bsz, t, 2 * GLA_K), BF16)]
                  + [jax.ShapeDtypeStruct(a.shape, BF16) for a in later_weights],
        compiler_params=pltpu.CompilerParams(
            dimension_semantics=("arbitrary", "arbitrary"), vmem_limit_bytes=VMEM_LIMIT),
        name="inproj",
    )(x, mod, w, bg, gn, *later_weights)
    return outs[:4], outs[4:]


def _swa_scores(n, g, q_ref, kv_ref, kvp_ref, bias_ref):
    L = ATTN_BLOCK
    G = ATTN_GROUP
    dh = ATTN_HEAD_DIM
    rows = slice(n * L, (n + 1) * L)
    k_prev = kvp_ref[0, :, :ATTN_KV] if n == 0 else kv_ref[0, (n - 1) * L:n * L, :ATTN_KV]
    k2 = jnp.concatenate([k_prev, kv_ref[0, rows, :ATTN_KV]], axis=0)
    sel = jnp.where(pl.program_id(1) == 0, 0, 1) if n == 0 else 1
    q_t = q_ref[0, rows, g * G * dh:(g + 1) * G * dh].T
    qs_t = jnp.concatenate([q_t[r * dh:(r + 1) * dh, :] for r in range(G)], axis=1)
    s = _dot(k2[:, g * dh:(g + 1) * dh], qs_t) + bias_ref[sel, g]
    return jnp.maximum(s[:L], s[L:])


def _swa_softmax(g, sc, sink_ref):
    L = ATTN_BLOCK
    G = ATTN_GROUP
    key = lax.broadcasted_iota(jnp.int32, (L, G * L), 0)
    qry = lax.broadcasted_iota(jnp.int32, (L, G * L), 1) & (L - 1)
    prev_key = key > qry
    sink = sink_ref[g]
    m = jnp.maximum(jnp.max(sc, axis=0, keepdims=True), sink)
    p_ = jnp.exp2(sc - m)
    denom = jnp.sum(p_, axis=0, keepdims=True) + jnp.exp2(sink - m)
    pb = p_.astype(BF16)
    zero = jnp.zeros((), BF16)
    p2 = jnp.concatenate([jnp.where(prev_key, pb, zero), jnp.where(prev_key, zero, pb)], axis=0)
    return p2, denom


def _swa_pv(n, g, p2, denom, kv_ref, kvp_ref, o_ref):
    L = ATTN_BLOCK
    G = ATTN_GROUP
    dh = ATTN_HEAD_DIM
    rows = slice(n * L, (n + 1) * L)
    v_prev = kvp_ref[0, :, ATTN_KV:] if n == 0 else kv_ref[0, (n - 1) * L:n * L, ATTN_KV:]
    v2 = jnp.concatenate([v_prev, kv_ref[0, rows, ATTN_KV:]], axis=0)
    o = lax.dot_general(v2, p2, TN_DIMS, preferred_element_type=F32)
    o = (o[g * dh:(g + 1) * dh] * (1.0 / denom)).astype(BF16)
    for r in range(G):
        h = g * G + r
        o_ref[0, h * dh:(h + 1) * dh, rows] = o[:, r * L:(r + 1) * L]


def _gla_prep(c, g_ref, la_ref):
    C = GLA_CHUNK
    ri = lax.broadcasted_iota(jnp.int32, (C, C), 0)
    ci = lax.broadcasted_iota(jnp.int32, (C, C), 1)
    tril = (ri >= ci).astype(BF16)
    rows = slice(c * C, (c + 1) * C)
    b = _dot(tril, la_ref[0, rows, :GLA_K]) + _dot(tril, la_ref[0, rows, GLA_K:])
    b_mid = b[C // 2 - 1:C // 2, :]
    b_last = b[C - 1:C, :]
    q = g_ref[0, rows, 0:GLA_K].astype(F32)
    k = g_ref[0, rows, GLA_K:2 * GLA_K].astype(F32)
    q_t = q * jnp.exp2(b - b_mid)
    k_t = k * jnp.exp2(b_mid - b)
    q_in = (q_t * jnp.exp2(b_mid)).astype(BF16)
    k_s = (k_t * jnp.exp2(b_last - b_mid)).astype(BF16)
    return q_in, q_t.astype(BF16), k_t.astype(BF16), k_s, jnp.exp2(b_last)


def _gla_intra(prep):
    C = GLA_CHUNK
    dk = GLA_DK
    _, q_t, k_t, _, _ = prep
    ri2 = lax.broadcasted_iota(jnp.int32, (C, 2 * C), 0)
    ci2 = lax.broadcasted_iota(jnp.int32, (C, 2 * C), 1) & (C - 1)
    causal2 = ri2 >= ci2
    low_k = lax.broadcasted_iota(jnp.int32, (C, 2 * dk), 1) < dk
    zero = jnp.zeros((), BF16)
    out = []
    for p in range(GLA_PAIRS):
        ks = slice(p * 2 * dk, (p + 1) * 2 * dk)
        kt_p = k_t[:, ks]
        kbd = jnp.concatenate([jnp.where(low_k, kt_p, zero), jnp.where(low_k, zero, kt_p)], axis=0)
        a = lax.dot_general(q_t[:, ks], kbd, NT_DIMS, preferred_element_type=F32)
        out.append(jnp.where(causal2, a.astype(BF16), zero))
    return out


def _gla_out(c, prep, a, st, g_ref, o_ref):
    C = GLA_CHUNK
    dk, dv = GLA_DK, GLA_DV
    q_in, _, _, k_s, decay = prep
    low_k = lax.broadcasted_iota(jnp.int32, (C, 2 * dk), 1) < dk
    zero = jnp.zeros((), BF16)
    zeros_v = jnp.zeros((C, dv), BF16)
    rows = slice(c * C, (c + 1) * C)
    st_new = []
    for p in range(GLA_PAIRS):
        ks = slice(p * 2 * dk, (p + 1) * 2 * dk)
        vs = slice(2 * GLA_K + p * 2 * dv, 2 * GLA_K + (p + 1) * 2 * dv)
        gs = slice(2 * GLA_K + GLA_V + p * 2 * dv, 2 * GLA_K + GLA_V + (p + 1) * 2 * dv)
        v_0 = g_ref[0, rows, vs.start:vs.start + dv]
        v_1 = g_ref[0, rows, vs.start + dv:vs.stop]
        vbd = jnp.concatenate([jnp.concatenate([v_0, zeros_v], axis=1),
                               jnp.concatenate([zeros_v, v_1], axis=1)], axis=0)
        o_p = _dot(a[p], vbd) + lax.dot_general(q_in[:, ks], st[p].astype(BF16), NT_DIMS,
                                                preferred_element_type=F32)
        ks_p = k_s[:, ks]
        dst = jnp.concatenate(
            [lax.dot_general(v_0, jnp.where(low_k, ks_p, zero), TN_DIMS, preferred_element_type=F32),
             lax.dot_general(v_1, jnp.where(low_k, zero, ks_p), TN_DIMS, preferred_element_type=F32)],
            axis=0)
        st_new.append(st[p] * decay[:, ks] + dst)
        o_n = jnp.concatenate([_rms(o_p[:, :dv]), _rms(o_p[:, dv:])], axis=1)
        o_ref[0, rows, p * 2 * dv:(p + 1) * 2 * dv] = (o_n * g_ref[0, rows, gs].astype(F32)).astype(BF16)
    return st_new


def _mixer_kernel(q_ref, kv_ref, bias_ref, sink_ref, g_ref, la_ref,
                  attn_ref, gla_ref, st_ref, kvp_ref, *, nblk):
    @pl.when(pl.program_id(1) == 0)
    def _():
        st_ref[...] = jnp.zeros_like(st_ref)
        kvp_ref[...] = jnp.zeros_like(kvp_ref)

    st = [st_ref[p] for p in range(GLA_PAIRS)]
    units = [(n, g) for n in range(nblk) for g in range(ATTN_KV_HEADS)]
    nu = len(units)
    sc, pd, prep, intra = {}, {}, {}, {}

    def emit(kind, i):
        nonlocal st
        if kind == "prep" and 0 <= i < nblk:
            prep[i] = _gla_prep(i, g_ref, la_ref)
        elif kind == "intra" and 0 <= i < nblk:
            intra[i] = _gla_intra(prep[i])
        elif kind == "out" and 0 <= i < nblk:
            st = _gla_out(i, prep.pop(i), intra.pop(i), st, g_ref, gla_ref)
        elif kind == "scores" and 0 <= i < nu:
            sc[i] = _swa_scores(*units[i], q_ref, kv_ref, kvp_ref, bias_ref)
        elif kind == "softmax" and 0 <= i < nu:
            pd[i] = _swa_softmax(units[i][1], sc.pop(i), sink_ref)
        elif kind == "pv" and 0 <= i < nu:
            _swa_pv(*units[i], *pd.pop(i), kv_ref, kvp_ref, attn_ref)

    for slot in range(nu + 2 * ATTN_KV_HEADS):
        if slot % ATTN_KV_HEADS == 0:
            c = slot // ATTN_KV_HEADS
            emit("out", c - 2)
            emit("intra", c - 1)
            emit("prep", c)
        emit("pv", slot - 2 * ATTN_KV_HEADS)
        emit("softmax", slot - ATTN_KV_HEADS)
        emit("scores", slot)
    for p in range(GLA_PAIRS):
        st_ref[p] = st[p]
    kvp_ref[0] = kv_ref[0, (nblk - 1) * ATTN_BLOCK:, :]


def _mixer(qa, kva, bias, sinkrow, gpack, la, tq):
    assert ATTN_BLOCK == GLA_CHUNK == WINDOW
    bsz, t, _ = qa.shape
    assert t % tq == 0 and tq % ATTN_BLOCK == 0
    nblk = tq // ATTN_BLOCK
    whole = lambda a: pl.BlockSpec(a.shape, lambda b, i: (0,) * a.ndim, pipeline_mode=pl.Buffered(1))
    tile = lambda n: pl.BlockSpec((1, tq, n), lambda b, i: (b, i, 0))
    return pl.pallas_call(
        functools.partial(_mixer_kernel, nblk=nblk),
        grid=(bsz, t // tq),
        in_specs=[tile(ATTN_Q), tile(2 * ATTN_KV),
                  whole(bias), whole(sinkrow), tile(GLA_PACK), tile(2 * GLA_K)],
        out_specs=[pl.BlockSpec((1, ATTN_Q, tq), lambda b, i: (b, 0, i)), tile(GLA_V)],
        out_shape=[jax.ShapeDtypeStruct((bsz, ATTN_Q, t), BF16),
                   jax.ShapeDtypeStruct((bsz, t, GLA_V), BF16)],
        scratch_shapes=[pltpu.VMEM((GLA_PAIRS, 2 * GLA_DV, 2 * GLA_DK), F32),
                        pltpu.VMEM((1, ATTN_BLOCK, 2 * ATTN_KV), BF16)],
        compiler_params=pltpu.CompilerParams(
            dimension_semantics=("arbitrary", "arbitrary"), vmem_limit_bytes=VMEM_LIMIT),
        name="mixer",
    )(qa, kva, bias, sinkrow, gpack, la)


def _mlp_kernel(x_ref, attn_ref, gla_ref, mod_ref, wo_ref, w1_ref, w2_ref,
                fg_ref, o_ref, u_ref, *, fchunk, final, nsub):
    rs = x_ref.shape[1] // nsub
    nf = D_FF // fchunk

    def pre(s):
        rows = slice(s * rs, (s + 1) * rs)
        mix = (lax.dot_general(attn_ref[0, :, rows], wo_ref[:ATTN_Q, :], TN_DIMS, preferred_element_type=F32)
               + _dot(gla_ref[0, rows, :], wo_ref[ATTN_Q:, :]))
        x1 = x_ref[0, rows, :] + mod_ref[0, MOD_GATE1] * mix
        return x1, (_rms(x1) * (1.0 + mod_ref[0, MOD_SCALE2]) + mod_ref[0, MOD_SHIFT2]).astype(BF16)

    def up(s, hb, c):
        cols = slice(c * fchunk, (c + 1) * fchunk)
        u = jnp.maximum(_dot(hb, w1_ref[:, cols]), 0.0)
        u_ref[s, :, cols] = (u * u).astype(BF16)

    def post(s, x1, y):
        x2 = x1 + mod_ref[0, MOD_GATE2] * y
        o_ref[0, s * rs:(s + 1) * rs, :] = _rms(x2) * fg_ref[...] if final else x2

    x1, hb = pre(0)
    pending = None
    for s in range(nsub):
        for c in range(nf // 2):
            up(s, hb, c)
        if s + 1 < nsub:
            nxt = pre(s + 1)
        for c in range(nf // 2, nf):
            up(s, hb, c)
        if pending is not None:
            post(*pending)
        pending = (s, x1, _dot(u_ref[s], w2_ref[...]))
        if s + 1 < nsub:
            x1, hb = nxt
    post(*pending)


def _mlp(x, attn, gla, mod, wo, w1, w2, fg, tm, final, nsub):
    bsz, t, d = x.shape
    const = lambda shape: pl.BlockSpec(shape, lambda b, i: (0,) * len(shape),
                                       pipeline_mode=pl.Buffered(1))
    tile = lambda n: pl.BlockSpec((1, tm, n), lambda b, i: (b, i, 0))
    return pl.pallas_call(
        functools.partial(_mlp_kernel, fchunk=MLP_FF_CHUNK, final=final, nsub=nsub),
        grid=(bsz, t // tm),
        in_specs=[tile(d), pl.BlockSpec((1, ATTN_Q, tm), lambda b, i: (b, 0, i)), tile(GLA_V),
                  _mod_spec(d),
                  const(wo.shape), const(w1.shape), const(w2.shape), const(fg.shape)],
        out_specs=tile(d),
        out_shape=jax.ShapeDtypeStruct((bsz, t, d), F32),
        scratch_shapes=[pltpu.VMEM((nsub, tm // nsub, D_FF), BF16)],
        compiler_params=pltpu.CompilerParams(
            dimension_semantics=("arbitrary", "arbitrary"), vmem_limit_bytes=VMEM_LIMIT),
        name="mlp",
    )(x, attn, gla, mod, wo, w1, w2, fg)


def kernel(x, c, w_ada, b_ada, w_in, w_gate_up, b_gate, gla_norm_g, attn_sinks, rel_bias,
           w_out, w_mlp_in, w_mlp_out, final_norm_g):
    bsz, t, d = x.shape
    depth = w_ada.shape[0]
    tile_in, tile_mix, tile_mlp = (min(v, t) for v in (INPROJ_TILE, MIXER_TILE, MLP_TILE))
    for l in range(depth):
        mod = _ada(c, w_ada[l], b_ada[l]).reshape(bsz, N_MOD, 1, d)
        bias, sinkrow = _bias_table(rel_bias, attn_sinks[l])
        win = _inproj_weight(w_in, w_gate_up, l)
        (qa, kva, gpack, la), (wo, w1, w2) = _inproj(
            x, mod, win, b_gate[l].reshape(1, GLA_K), jnp.tile(gla_norm_g[l], GLA_HEADS).reshape(1, GLA_V),
            tile_in, max(1, tile_in // INPROJ_SUB_TILE), (w_out[l], w_mlp_in[l], w_mlp_out[l]))
        attn, gla = _mixer(qa, kva, bias, sinkrow, gpack, la, tile_mix)
        x = _mlp(x, attn, gla, mod, wo, w1, w2,
                 final_norm_g.reshape(1, d), tile_mlp, final=(l == depth - 1),
                 nsub=max(1, tile_mlp // MLP_SUB_TILE))
    return x
```
